```python
import math
import jax, jax.numpy as jnp
from jax import lax
import numpy as np

D_MODEL = 1024
BATCH = 4
SEQ = 8192
DEPTH = 1
DEC_BATCH = 128
DEC_SEQ = 8
PAST_LEN = 16384
PAGE_SIZE = 128

N_HEADS = 8
HEAD_DIM = 64
N_KV_HEADS = 2
GROUP = N_HEADS // N_KV_HEADS
ATTN_WIDTH = N_HEADS * HEAD_DIM
KV_WIDTH = N_KV_HEADS * HEAD_DIM
WINDOW = 128
BLOCK = WINDOW
POOL_WIDTH = D_MODEL // 2
POOL_WINDOWS = (2, 4, 8, 16)
N_POOL_GROUPS = len(POOL_WINDOWS)
POOL_GROUP_WIDTH = POOL_WIDTH // N_POOL_GROUPS
POOL_STATE = max(POOL_WINDOWS) - 1
D_FF = 4 * D_MODEL
N_BUCKETS = 32
MAX_DISTANCE = 128
RMS_EPS = 1e-6
NEG_INF = -1e30
IN_WIDTH = ATTN_WIDTH + 2 * KV_WIDTH + POOL_WIDTH + 2 * D_MODEL
SPLITS = (ATTN_WIDTH, ATTN_WIDTH + KV_WIDTH, ATTN_WIDTH + 2 * KV_WIDTH,
          ATTN_WIDTH + 2 * KV_WIDTH + POOL_WIDTH,
          ATTN_WIDTH + 2 * KV_WIDTH + POOL_WIDTH + D_MODEL)

kernel_name = "hybrid_swa_sink_pool_gated_decode_step"


def rms_norm(x, g):
    xf = x.astype(jnp.float32)
    y = xf * lax.rsqrt(jnp.mean(xf * xf, axis=-1, keepdims=True) + RMS_EPS)
    return (y * g.astype(jnp.float32)).astype(x.dtype)


def rel_bucket(dist):
    d = jnp.maximum(dist, 0)
    max_exact = N_BUCKETS // 2
    large = max_exact + (jnp.log(jnp.maximum(d, 1).astype(jnp.float32) / max_exact)
                         / math.log(MAX_DISTANCE / max_exact)
                         * (N_BUCKETS - max_exact)).astype(jnp.int32)
    large = jnp.minimum(large, N_BUCKETS - 1)
    return jnp.where(d < max_exact, d, large)


def rel_bias(dist, table):
    b = table[rel_bucket(dist)]
    b = jnp.transpose(b, (2, 0, 1)).astype(jnp.float32)
    return b.reshape(N_KV_HEADS, GROUP, dist.shape[0], dist.shape[1])


def window_attention(q, k, v, valid, bias, sinks):
    s = jnp.einsum('bnqkgd,bnskd->bnkgqs', q, k,
                   preferred_element_type=jnp.float32) * (HEAD_DIM ** -0.5) + bias
    s = jnp.where(valid[None, :, None, None], s, NEG_INF)
    sink = sinks.astype(jnp.float32).reshape(1, 1, N_KV_HEADS, GROUP, 1, 1)
    m = jnp.maximum(jnp.max(s, axis=-1, keepdims=True), sink)
    p = jnp.exp(s - m)
    denom = jnp.sum(p, axis=-1, keepdims=True) + jnp.exp(sink - m)
    p = (p / denom).astype(v.dtype)
    return jnp.einsum('bnkgqs,bnskd->bnqkgd', p, v)


def pool_mix(ext, start_pos, w_grp, scale):
    B = ext.shape[0]
    T = ext.shape[1] - POOL_STATE
    cs = jnp.cumsum(jnp.pad(ext.astype(jnp.float32), ((0, 0), (1, 0), (0, 0))), axis=1)
    hi = cs[:, POOL_STATE + 1:]
    pos = start_pos + jnp.arange(T)
    parts = []
    for g, w in enumerate(POOL_WINDOWS):
        c = slice(g * POOL_GROUP_WIDTH, (g + 1) * POOL_GROUP_WIDTH)
        lo = cs[:, POOL_STATE + 1 - w:POOL_STATE + 1 - w + T, c]
        cnt = jnp.minimum(pos + 1, w).astype(jnp.float32)[None, :, None]
        parts.append((hi[..., c] - lo) / cnt)
    z = jnp.concatenate(parts, axis=-1) - ext[:, POOL_STATE:].astype(jnp.float32)
    z = z.astype(ext.dtype).reshape(B, T, N_POOL_GROUPS, POOL_GROUP_WIDTH)
    z = jnp.einsum('btgc,gcd->btgd', z, w_grp).reshape(B, T, POOL_WIDTH)
    return z * scale


def project_in(x, g_pre, w_in, b_in):
    u = rms_norm(x, g_pre) @ w_in + b_in
    return jnp.split(u, SPLITS, axis=-1)


def finish_layer(x, attn_o, pool_z, ga, gp, w_attn_up, w_pool_up, w_out,
                 g_post_mix, g_pre_mlp, g_post_mlp, w_mlp_up, w_mlp_down):
    a = attn_o @ w_attn_up
    p = pool_z @ w_pool_up
    m = jax.nn.sigmoid(ga) * a + jax.nn.sigmoid(gp) * p
    h = x + rms_norm(m @ w_out, g_post_mix)
    f = jnp.square(jax.nn.relu(rms_norm(h, g_pre_mlp) @ w_mlp_up)) @ w_mlp_down
    return h + rms_norm(f, g_post_mlp)


def setup_inputs(seed: int = 0) -> dict:
    key = jax.random.key(seed)
    ks = jax.random.split(key, 24)
    nrm = lambda k, shape, s: jax.random.normal(k, shape, jnp.float32) * s
    n_pages = PAST_LEN // PAGE_SIZE
    del n_pages
    return {
        "x_prompt": nrm(ks[0], (BATCH, SEQ, D_MODEL), 1.0),
        "x_sample": nrm(ks[1], (DEC_BATCH, DEC_SEQ, D_MODEL), 1.0),
        "cache_k_win": nrm(ks[2], (DEPTH, DEC_BATCH, WINDOW, N_KV_HEADS, HEAD_DIM), 1.0),
        "cache_v_win": nrm(ks[3], (DEPTH, DEC_BATCH, WINDOW, N_KV_HEADS, HEAD_DIM), 1.0),
        "state_pool": nrm(ks[4], (DEPTH, DEC_BATCH, POOL_STATE, POOL_WIDTH), 1.0),
        "norm_pre_mix": 1.0 + nrm(ks[5], (DEPTH, D_MODEL), 0.05),
        "norm_post_mix": 1.0 + nrm(ks[6], (DEPTH, D_MODEL), 0.05),
        "norm_pre_mlp": 1.0 + nrm(ks[7], (DEPTH, D_MODEL), 0.05),
        "norm_post_mlp": 1.0 + nrm(ks[8], (DEPTH, D_MODEL), 0.05),
        "w_in": nrm(ks[9], (DEPTH, D_MODEL, IN_WIDTH), D_MODEL ** -0.5),
        "b_in": nrm(ks[10], (DEPTH, IN_WIDTH), 0.02),
        "attn_sinks": nrm(ks[11], (DEPTH, N_HEADS), 0.5),
        "rel_bias_table": nrm(ks[12], (N_BUCKETS, N_HEADS), 0.5),
        "w_attn_up": nrm(ks[13], (DEPTH, ATTN_WIDTH, D_MODEL), ATTN_WIDTH ** -0.5),
        "w_pool_grp": nrm(ks[14], (DEPTH, N_POOL_GROUPS, POOL_GROUP_WIDTH, POOL_GROUP_WIDTH), POOL_GROUP_WIDTH ** -0.5),
        "pool_scale": 1.0 + nrm(ks[15], (DEPTH, POOL_WIDTH), 0.1),
        "w_pool_up": nrm(ks[16], (DEPTH, POOL_WIDTH, D_MODEL), POOL_WIDTH ** -0.5),
        "w_out": nrm(ks[17], (DEPTH, D_MODEL, D_MODEL), D_MODEL ** -0.5),
        "w_mlp_up": nrm(ks[18], (DEPTH, D_MODEL, D_FF), D_MODEL ** -0.5),
        "w_mlp_down": nrm(ks[19], (DEPTH, D_FF, D_MODEL), D_FF ** -0.5),
    }


def reference(x_prompt, x_sample, cache_k_win, cache_v_win, state_pool,
              norm_pre_mix, norm_post_mix, norm_pre_mlp, norm_post_mlp,
              w_in, b_in, attn_sinks, rel_bias_table, w_attn_up, w_pool_grp,
              pool_scale, w_pool_up, w_out, w_mlp_up, w_mlp_down):
    B, S, _ = x_prompt.shape
    NB = S // BLOCK
    DB, T, _ = x_sample.shape

    r = jnp.arange(BLOCK)[:, None]
    j = jnp.arange(2 * BLOCK)[None, :]
    dist_p = r + BLOCK - j
    key_pos_p = jnp.arange(NB)[:, None, None] * BLOCK - BLOCK + j[None]
    valid_p = (dist_p >= 0)[None] & (dist_p < WINDOW)[None] & (key_pos_p >= 0)
    bias_p = rel_bias(dist_p, rel_bias_table)

    i_s = jnp.arange(T)[:, None]
    j_s = jnp.arange(WINDOW + T)[None, :]
    dist_s = i_s + WINDOW - j_s
    valid_s = ((dist_s >= 0) & (dist_s < WINDOW) & (PAST_LEN - WINDOW + j_s >= 0))[None]
    bias_s = rel_bias(dist_s, rel_bias_table)

    def band(t):
        tp = jnp.pad(t, ((0, 0), (BLOCK, 0), (0, 0), (0, 0)))
        tp = tp.reshape(B, NB + 1, BLOCK, N_KV_HEADS, HEAD_DIM)
        return jnp.concatenate([tp[:, :-1], tp[:, 1:]], axis=2)

    hp, hs = x_prompt, x_sample
    kp_l, vp_l, pp_l, ks_l, vs_l, ps_l = [], [], [], [], [], []
    for l in range(DEPTH):
        tail = (w_attn_up[l], w_pool_up[l], w_out[l], norm_post_mix[l],
                norm_pre_mlp[l], norm_post_mlp[l], w_mlp_up[l], w_mlp_down[l])

        q, k, v, up, ga, gp = project_in(hp, norm_pre_mix[l], w_in[l], b_in[l])
        q = q.reshape(B, NB, BLOCK, N_KV_HEADS, GROUP, HEAD_DIM)
        k = k.reshape(B, S, N_KV_HEADS, HEAD_DIM)
        v = v.reshape(B, S, N_KV_HEADS, HEAD_DIM)
        o = window_attention(q, band(k), band(v), valid_p, bias_p, attn_sinks[l])
        o = o.reshape(B, S, ATTN_WIDTH)
        ext = jnp.pad(up, ((0, 0), (POOL_STATE, 0), (0, 0)))
        z = pool_mix(ext, 0, w_pool_grp[l], pool_scale[l])
        kp_l.append(k[:, -WINDOW:])
        vp_l.append(v[:, -WINDOW:])
        pp_l.append(up[:, -POOL_STATE:])
        hp = finish_layer(hp, o, z, ga, gp, *tail)

        q, k, v, up, ga, gp = project_in(hs, norm_pre_mix[l], w_in[l], b_in[l])
        q = q.reshape(DB, 1, T, N_KV_HEADS, GROUP, HEAD_DIM)
        kx = jnp.concatenate([cache_k_win[l], k.reshape(DB, T, N_KV_HEADS, HEAD_DIM)], axis=1)
        vx = jnp.concatenate([cache_v_win[l], v.reshape(DB, T, N_KV_HEADS, HEAD_DIM)], axis=1)
        o = window_attention(q, kx[:, None], vx[:, None], valid_s, bias_s, attn_sinks[l])
        o = o.reshape(DB, T, ATTN_WIDTH)
        ext = jnp.concatenate([state_pool[l], up], axis=1)
        z = pool_mix(ext, PAST_LEN, w_pool_grp[l], pool_scale[l])
        ks_l.append(kx[:, -WINDOW:])
        vs_l.append(vx[:, -WINDOW:])
        ps_l.append(ext[:, -POOL_STATE:])
        hs = finish_layer(hs, o, z, ga, gp, *tail)

    return (hp, hs, jnp.stack(kp_l), jnp.stack(vp_l), jnp.stack(pp_l),
            jnp.stack(ks_l), jnp.stack(vs_l), jnp.stack(ps_l))
```

```python
import functools
import math

import jax
import jax.numpy as jnp
from jax import lax
from jax.experimental import pallas as pl
from jax.experimental.pallas import tpu as pltpu

D_MODEL = 1024
N_HEADS = 8
HEAD_DIM = 64
N_KV_HEADS = 2
GROUP = N_HEADS // N_KV_HEADS
ATTN_WIDTH = N_HEADS * HEAD_DIM
KV_WIDTH = N_KV_HEADS * HEAD_DIM
WINDOW = 128
POOL_WIDTH = D_MODEL // 2
POOL_WINDOWS = (2, 4, 8, 16)
POOL_GROUP_WIDTH = POOL_WIDTH // len(POOL_WINDOWS)
POOL_STATE = max(POOL_WINDOWS) - 1
POOL_PAD = POOL_STATE + 1
D_FF = 4 * D_MODEL
N_BUCKETS = 32
MAX_DISTANCE = 128
RMS_EPS = 1e-6
NEG_INF = -1e30
PAST_LEN = 16384
IN_WIDTH = ATTN_WIDTH + 2 * KV_WIDTH + POOL_WIDTH + 2 * D_MODEL
OFF_K = ATTN_WIDTH
OFF_V = OFF_K + KV_WIDTH
OFF_POOL = OFF_V + KV_WIDTH
OFF_GA = OFF_POOL + POOL_WIDTH
OFF_GP = OFF_GA + D_MODEL
N_KEYS = 2 * WINDOW

LANES = 128
SUBLANES = 8
VMEM_LIMIT_BYTES = 56 * 1024 * 1024

TOKEN_TILE = 512
DEC_SEQ = 8
SAMPLE_TILE = 256
SEQ_PER_TILE = SAMPLE_TILE // DEC_SEQ

BF16 = jnp.bfloat16
F32 = jnp.float32


def _rms(x, g):
    return x * lax.rsqrt(jnp.mean(x * x, axis=-1, keepdims=True) + RMS_EPS) * g


def _dot(a, b):
    return jnp.dot(a, b, preferred_element_type=F32)


def _dot_nt(a, b):
    return lax.dot_general(a, b, (((1,), (1,)), ((), ())), preferred_element_type=F32)


def _build_tables(rows, bucket_ref, table_ref, sinks_ref, bias_ref, sink_ref, valid_ref):
    bucket = bucket_ref[...]
    for h in range(N_HEADS):
        acc = jnp.zeros(bucket.shape, F32)
        for b in range(N_BUCKETS):
            acc = jnp.where(bucket == b, table_ref[b, h], acc)
        bias_ref[h * rows:(h + 1) * rows, :] = acc
        sink_ref[h * rows:(h + 1) * rows, :] = jnp.full((rows, LANES), sinks_ref[0, h], F32)
    r = lax.broadcasted_iota(jnp.int32, (rows, N_KEYS), 0)
    j = lax.broadcasted_iota(jnp.int32, (rows, N_KEYS), 1)
    cur = jnp.where(j >= WINDOW, jnp.where(j - WINDOW <= r, 1.0, 0.0), 0.0)
    prev = jnp.where(j < WINDOW, jnp.where(j > r, 1.0, 0.0), 0.0)
    valid_ref[0] = cur
    valid_ref[1] = cur + prev


def _attend(rows, chunk, qh, kcat, vcat, valid, bias_ref, sink_ref):
    s_all = _dot_nt(qh, kcat)
    reps = chunk // rows
    if reps > 1:
        valid = jnp.concatenate([valid] * reps, axis=0)
    ok = valid > 0.5
    ps = []
    for c in range(N_HEADS * rows // chunk):
        sl = slice(c * chunk, (c + 1) * chunk)
        s = jnp.where(ok, s_all[sl] + bias_ref[sl, :], NEG_INF)
        sink = sink_ref[sl, 0:1]
        m = jnp.maximum(jnp.max(s, axis=-1, keepdims=True), sink)
        p = jnp.exp(s - m)
        denom = jnp.sum(p, axis=-1, keepdims=True) + jnp.exp(sink - m)
        ps.append(p / denom)
    p_all = jnp.concatenate(ps, axis=0).astype(BF16) if len(ps) > 1 else ps[0].astype(BF16)
    return _dot(p_all, vcat)


def _head_queries(q):
    lane = lax.broadcasted_iota(jnp.int32, (q.shape[0], LANES), 1)
    lo = lane < HEAD_DIM
    out = []
    for c in range(ATTN_WIDTH // LANES):
        qc = q[:, c * LANES:(c + 1) * LANES]
        qr = pltpu.roll(qc, HEAD_DIM, axis=1)
        g = (2 * c) // GROUP
        if g == 0:
            out.append(jnp.where(lo, qc, 0.0))
            out.append(jnp.where(lo, qr, 0.0))
        else:
            out.append(jnp.where(lo, 0.0, qr))
            out.append(jnp.where(lo, 0.0, qc))
    return out


def _merge_heads(o_heads):
    lane = lax.broadcasted_iota(jnp.int32, o_heads[0].shape, 1)
    lo = lane < HEAD_DIM
    chunks = []
    for c in range(ATTN_WIDTH // LANES):
        e, o = o_heads[2 * c], o_heads[2 * c + 1]
        g = (2 * c) // GROUP
        if g == 0:
            chunks.append(jnp.where(lo, e, pltpu.roll(o, HEAD_DIM, axis=1)))
        else:
            chunks.append(jnp.where(lo, pltpu.roll(e, HEAD_DIM, axis=1), o))
    return jnp.concatenate(chunks, axis=1)


def _pool_means(ext, first_pos):
    del first_pos
    parts = []
    for gi, w in enumerate(POOL_WINDOWS):
        acc = ext[:, gi * POOL_GROUP_WIDTH:(gi + 1) * POOL_GROUP_WIDTH]
        shift = 1
        while shift < w:
            acc = acc + pltpu.roll(acc, shift, axis=0)
            shift *= 2
        parts.append(acc)
    return jnp.concatenate(parts, axis=1)


def _pool_counts(pos):
    lane = lax.broadcasted_iota(jnp.int32, (pos.shape[0], POOL_WIDTH), 1)
    w = jnp.left_shift(2, lane // POOL_GROUP_WIDTH)
    return jnp.minimum(pos + 1, w).astype(F32)


def _mix_tail(x, xn, attn_o, pool_sum, up, cnt, w_in_ref, b_in_ref, w_grp_ref, scale_ref,
              w_attn_up_ref, w_pool_up_ref, w_out_ref, g_post_ref):
    z = (pool_sum / cnt - up).astype(BF16)
    pool_z = (_dot(z, w_grp_ref[...]) * scale_ref[...]).astype(BF16)
    ga = _dot(xn, w_in_ref[:, OFF_GA:OFF_GP]) + b_in_ref[:, OFF_GA:OFF_GP]
    m = jax.nn.sigmoid(ga) * _dot(attn_o, w_attn_up_ref[...])
    gp = _dot(xn, w_in_ref[:, OFF_GP:IN_WIDTH]) + b_in_ref[:, OFF_GP:IN_WIDTH]
    m = m + jax.nn.sigmoid(gp) * _dot(pool_z, w_pool_up_ref[...])
    mo = _dot(m.astype(BF16), w_out_ref[...])
    return x + _rms(mo, g_post_ref[...])


def _mix_prompt_kernel(x_ref, g_pre_ref, w_in_ref, b_in_ref, table_ref, sinks_ref, bucket_ref,
                       w_grp_ref, scale_ref, w_attn_up_ref, w_pool_up_ref, w_out_ref, g_post_ref,
                       h_ref, kwin_ref, vwin_ref, pstate_ref,
                       bias_ref, sink_ref, valid_ref, kc_ref, vc_ref, ext_ref, qh_ref, ao_ref):
    tm = TOKEN_TILE
    t = pl.program_id(1)

    @pl.when((pl.program_id(0) == 0) & (t == 0))
    def _():
        _build_tables(WINDOW, bucket_ref, table_ref, sinks_ref, bias_ref, sink_ref, valid_ref)

    @pl.when(t == 0)
    def _():
        kc_ref[0:WINDOW, :] = jnp.zeros((WINDOW, LANES), BF16)
        vc_ref[0:WINDOW, :] = jnp.zeros((WINDOW, LANES), BF16)
        ext_ref[0:POOL_PAD, :] = jnp.zeros((POOL_PAD, POOL_WIDTH), F32)

    x = x_ref[...]
    xn = _rms(x, g_pre_ref[...]).astype(BF16)
    u = _dot(xn, w_in_ref[:, 0:OFF_GA]) + b_in_ref[:, 0:OFF_GA]
    q = u[:, 0:OFF_K] * (HEAD_DIM ** -0.5)
    k = u[:, OFF_K:OFF_V]
    v = u[:, OFF_V:OFF_POOL]
    up = u[:, OFF_POOL:OFF_GA]

    kwin_ref[...] = k[tm - WINDOW:, :]
    vwin_ref[...] = v[tm - WINDOW:, :]
    pstate_ref[...] = up[tm - POOL_PAD:, :]
    kc_ref[WINDOW:, :] = k.astype(BF16)
    vc_ref[WINDOW:, :] = v.astype(BF16)
    ext_ref[POOL_PAD:, :] = up
    for h, qh in enumerate(_head_queries(q)):
        qh_ref[h] = qh.astype(BF16)

    def block(i, carry):
        base = pl.multiple_of(i * WINDOW, WINDOW)
        qh = qh_ref[:, pl.ds(base, WINDOW), :].reshape(N_HEADS * WINDOW, LANES)
        kcat = kc_ref[pl.ds(base, N_KEYS), :]
        vcat = vc_ref[pl.ds(base, N_KEYS), :]
        has_prev = jnp.where((t == 0) & (i == 0), 0, 1)
        o_all = _attend(WINDOW, WINDOW, qh, kcat, vcat, valid_ref[has_prev], bias_ref, sink_ref)
        o = _merge_heads([o_all[h * WINDOW:(h + 1) * WINDOW] for h in range(N_HEADS)])
        ao_ref[pl.ds(base, WINDOW), :] = o.astype(BF16)
        return carry

    lax.fori_loop(0, tm // WINDOW, block, 0)

    pool_sum = _pool_means(ext_ref[...], None)[POOL_PAD:, :]
    pos = t * tm + lax.broadcasted_iota(jnp.int32, (tm, 1), 0)
    h_ref[...] = _mix_tail(x, xn, ao_ref[...], pool_sum, up, _pool_counts(pos),
                           w_in_ref, b_in_ref, w_grp_ref, scale_ref,
                           w_attn_up_ref, w_pool_up_ref, w_out_ref, g_post_ref)

    kc_ref[0:WINDOW, :] = kc_ref[tm:tm + WINDOW, :]
    vc_ref[0:WINDOW, :] = vc_ref[tm:tm + WINDOW, :]
    ext_ref[0:POOL_PAD, :] = ext_ref[tm:tm + POOL_PAD, :]


def _mix_sample_kernel(x_ref, ck_ref, cv_ref, state_ref, g_pre_ref, w_in_ref, b_in_ref, table_ref,
                       sinks_ref, bucket_ref, w_grp_ref, scale_ref, w_attn_up_ref, w_pool_up_ref,
                       w_out_ref, g_post_ref,
                       h_ref, knew_ref, vnew_ref, up_ref,
                       bias_ref, sink_ref, valid_ref, kx_ref, vx_ref, ext_ref, qh_ref, oh_ref):
    tm = SAMPLE_TILE
    ns = SEQ_PER_TILE
    T = DEC_SEQ

    @pl.when(pl.program_id(0) == 0)
    def _():
        _build_tables(T, bucket_ref, table_ref, sinks_ref, bias_ref, sink_ref, valid_ref)

    x = x_ref[...]
    xn = _rms(x, g_pre_ref[...]).astype(BF16)
    u = _dot(xn, w_in_ref[:, 0:OFF_GA]) + b_in_ref[:, 0:OFF_GA]
    q = u[:, 0:OFF_K] * (HEAD_DIM ** -0.5)
    k = u[:, OFF_K:OFF_V]
    v = u[:, OFF_V:OFF_POOL]
    up = u[:, OFF_POOL:OFF_GA]

    knew_ref[...] = k
    vnew_ref[...] = v
    up_ref[...] = up
    kx_ref[:, 0:WINDOW, :] = ck_ref[...].astype(BF16)
    vx_ref[:, 0:WINDOW, :] = cv_ref[...].astype(BF16)
    pad = jnp.zeros((ns, WINDOW - T, LANES), F32)
    kx_ref[:, WINDOW:, :] = jnp.concatenate([k.reshape(ns, T, LANES), pad], axis=1).astype(BF16)
    vx_ref[:, WINDOW:, :] = jnp.concatenate([v.reshape(ns, T, LANES), pad], axis=1).astype(BF16)
    ext_ref[:, 0:POOL_PAD, :] = state_ref[...]
    ext_ref[:, POOL_PAD:, :] = up.reshape(ns, T, POOL_WIDTH)
    for h, qh in enumerate(_head_queries(q)):
        qh_ref[:, h * T:(h + 1) * T, :] = qh.reshape(ns, T, LANES)

    valid = valid_ref[1]

    def seq(s, carry):
        qh = qh_ref[s].astype(BF16)
        oh_ref[s] = _attend(T, N_HEADS * T, qh, kx_ref[s], vx_ref[s], valid, bias_ref, sink_ref)
        return carry

    lax.fori_loop(0, ns, seq, 0, unroll=4)

    o_heads = [oh_ref[:, h * T:(h + 1) * T, :].reshape(tm, LANES) for h in range(N_HEADS)]
    attn_o = _merge_heads(o_heads).astype(BF16)

    ext = ext_ref[...].reshape(ns * (POOL_PAD + T), POOL_WIDTH)
    pool_sum = _pool_means(ext, None).reshape(ns, POOL_PAD + T, POOL_WIDTH)[:, POOL_PAD:, :]
    pool_sum = pool_sum.reshape(tm, POOL_WIDTH)
    row = lax.broadcasted_iota(jnp.int32, (tm, 1), 0)
    pos = PAST_LEN + (row & (T - 1))
    h_ref[...] = _mix_tail(x, xn, attn_o, pool_sum, up, _pool_counts(pos),
                           w_in_ref, b_in_ref, w_grp_ref, scale_ref,
                           w_attn_up_ref, w_pool_up_ref, w_out_ref, g_post_ref)


def _mlp_kernel(h_ref, g_pre_ref, w_up_ref, w_down_ref, g_post_ref, y_ref):
    h = h_ref[...]
    hn = _rms(h, g_pre_ref[...]).astype(BF16)
    f = jnp.zeros(h.shape, F32)
    chunk = D_FF // 4
    for c in range(D_FF // chunk):
        a = jnp.maximum(_dot(hn, w_up_ref[:, c * chunk:(c + 1) * chunk]), 0.0)
        f = f + _dot((a * a).astype(BF16), w_down_ref[c * chunk:(c + 1) * chunk, :])
    y_ref[...] = h + _rms(f, g_post_ref[...])


def _full(shape):
    return pl.BlockSpec(shape, lambda *_: (0,) * len(shape))


def _smem():
    return pl.BlockSpec(memory_space=pltpu.SMEM)


def _rel_bucket(dist):
    d = jnp.maximum(dist, 0)
    max_exact = N_BUCKETS // 2
    large = max_exact + (jnp.log(jnp.maximum(d, 1).astype(F32) / max_exact)
                         / math.log(MAX_DISTANCE / max_exact)
                         * (N_BUCKETS - max_exact)).astype(jnp.int32)
    large = jnp.minimum(large, N_BUCKETS - 1)
    return jnp.where(d < max_exact, d, large)


def _bucket_matrix(rows):
    r = jnp.arange(rows)[:, None]
    j = jnp.arange(N_KEYS)[None, :]
    return _rel_bucket(r + WINDOW - j).astype(jnp.int32)


def _mlp(h2d, g_pre, w_up, w_down, g_post):
    n = h2d.shape[0]
    return pl.pallas_call(
        _mlp_kernel,
        grid=(n // TOKEN_TILE,),
        in_specs=[pl.BlockSpec((TOKEN_TILE, D_MODEL), lambda i: (i, 0)),
                  _full((1, D_MODEL)), _full((D_MODEL, D_FF)), _full((D_FF, D_MODEL)),
                  _full((1, D_MODEL))],
        out_specs=pl.BlockSpec((TOKEN_TILE, D_MODEL), lambda i: (i, 0)),
        out_shape=jax.ShapeDtypeStruct((n, D_MODEL), F32),
        compiler_params=pltpu.CompilerParams(dimension_semantics=("arbitrary",),
                                             vmem_limit_bytes=VMEM_LIMIT_BYTES),
        name="mlp",
    )(h2d, g_pre, w_up, w_down, g_post)


def kernel(x_prompt, x_sample, cache_k_win, cache_v_win, state_pool, norm_pre_mix, norm_post_mix,
           norm_pre_mlp, norm_post_mlp, w_in, b_in, attn_sinks, rel_bias_table, w_attn_up,
           w_pool_grp, pool_scale, w_pool_up, w_out, w_mlp_up, w_mlp_down):
    B, S, _ = x_prompt.shape
    DB, T, _ = x_sample.shape
    depth = w_in.shape[0]
    assert depth == 1 and S % TOKEN_TILE == 0 and (DB * T) % TOKEN_TILE == 0
    assert T == DEC_SEQ and (DB * T) % SAMPLE_TILE == 0

    l = 0
    w_in_b = w_in[l].astype(BF16)
    w_grp_b = jax.scipy.linalg.block_diag(*[w_pool_grp[l, g] for g in range(len(POOL_WINDOWS))]
                                          ).astype(BF16)
    w_attn_up_b = w_attn_up[l].astype(BF16)
    w_pool_up_b = w_pool_up[l].astype(BF16)
    w_out_b = w_out[l].astype(BF16)
    w_mlp_up_b = w_mlp_up[l].astype(BF16)
    w_mlp_down_b = w_mlp_down[l].astype(BF16)
    b_in2 = b_in[l][None, :]
    sinks2 = attn_sinks[l][None, :]
    scale2 = pool_scale[l][None, :]
    g_pre, g_post = norm_pre_mix[l][None, :], norm_post_mix[l][None, :]
    g_pre_mlp, g_post_mlp = norm_pre_mlp[l][None, :], norm_post_mlp[l][None, :]

    weight_specs = [_full((POOL_WIDTH, POOL_WIDTH)), _full((1, POOL_WIDTH)),
                    _full((ATTN_WIDTH, D_MODEL)), _full((POOL_WIDTH, D_MODEL)),
                    _full((D_MODEL, D_MODEL)), _full((1, D_MODEL))]
    weights = (w_grp_b, scale2, w_attn_up_b, w_pool_up_b, w_out_b, g_post)

    tm = TOKEN_TILE
    h_p, kwin, vwin, pstate = pl.pallas_call(
        _mix_prompt_kernel,
        grid=(B, S // tm),
        in_specs=[pl.BlockSpec((None, tm, D_MODEL), lambda b, t: (b, t, 0)),
                  _full((1, D_MODEL)), _full((D_MODEL, IN_WIDTH)), _full((1, IN_WIDTH)),
                  _smem(), _smem(), _full((WINDOW, N_KEYS))] + weight_specs,
        out_specs=[pl.BlockSpec((None, tm, D_MODEL), lambda b, t: (b, t, 0)),
                   pl.BlockSpec((None, WINDOW, KV_WIDTH), lambda b, t: (b, 0, 0)),
                   pl.BlockSpec((None, WINDOW, KV_WIDTH), lambda b, t: (b, 0, 0)),
                   pl.BlockSpec((None, POOL_PAD, POOL_WIDTH), lambda b, t: (b, 0, 0))],
        out_shape=[jax.ShapeDtypeStruct((B, S, D_MODEL), F32),
                   jax.ShapeDtypeStruct((B, WINDOW, KV_WIDTH), F32),
                   jax.ShapeDtypeStruct((B, WINDOW, KV_WIDTH), F32),
                   jax.ShapeDtypeStruct((B, POOL_PAD, POOL_WIDTH), F32)],
        scratch_shapes=[pltpu.VMEM((N_HEADS * WINDOW, N_KEYS), F32),
                        pltpu.VMEM((N_HEADS * WINDOW, LANES), F32),
                        pltpu.VMEM((2, WINDOW, N_KEYS), F32),
                        pltpu.VMEM((WINDOW + tm, LANES), BF16),
                        pltpu.VMEM((WINDOW + tm, LANES), BF16),
                        pltpu.VMEM((POOL_PAD + tm, POOL_WIDTH), F32),
                        pltpu.VMEM((N_HEADS, tm, LANES), BF16),
                        pltpu.VMEM((tm, ATTN_WIDTH), BF16)],
        compiler_params=pltpu.CompilerParams(dimension_semantics=("arbitrary", "arbitrary"),
                                             vmem_limit_bytes=VMEM_LIMIT_BYTES),
        name="mix_prompt",
    )(x_prompt, g_pre, w_in_b, b_in2, rel_bias_table, sinks2, _bucket_matrix(WINDOW), *weights)
    y_p = _mlp(h_p.reshape(B * S, D_MODEL), g_pre_mlp, w_mlp_up_b, w_mlp_down_b, g_post_mlp)

    tm, ns = SAMPLE_TILE, SEQ_PER_TILE
    n_tok = DB * T
    state16 = jnp.pad(state_pool[l], ((0, 0), (POOL_PAD - POOL_STATE, 0), (0, 0)))
    ck = cache_k_win[l].reshape(DB, WINDOW, KV_WIDTH)
    cv = cache_v_win[l].reshape(DB, WINDOW, KV_WIDTH)
    h_s, k_new, v_new, up_new = pl.pallas_call(
        _mix_sample_kernel,
        grid=(n_tok // tm,),
        in_specs=[pl.BlockSpec((tm, D_MODEL), lambda i: (i, 0)),
                  pl.BlockSpec((ns, WINDOW, KV_WIDTH), lambda i: (i, 0, 0)),
                  pl.BlockSpec((ns, WINDOW, KV_WIDTH), lambda i: (i, 0, 0)),
                  pl.BlockSpec((ns, POOL_PAD, POOL_WIDTH), lambda i: (i, 0, 0)),
                  _full((1, D_MODEL)), _full((D_MODEL, IN_WIDTH)), _full((1, IN_WIDTH)),
                  _smem(), _smem(), _full((T, N_KEYS))] + weight_specs,
        out_specs=[pl.BlockSpec((tm, D_MODEL), lambda i: (i, 0)),
                   pl.BlockSpec((tm, KV_WIDTH), lambda i: (i, 0)),
                   pl.BlockSpec((tm, KV_WIDTH), lambda i: (i, 0)),
                   pl.BlockSpec((tm, POOL_WIDTH), lambda i: (i, 0))],
        out_shape=[jax.ShapeDtypeStruct((n_tok, D_MODEL), F32),
                   jax.ShapeDtypeStruct((n_tok, KV_WIDTH), F32),
                   jax.ShapeDtypeStruct((n_tok, KV_WIDTH), F32),
                   jax.ShapeDtypeStruct((n_tok, POOL_WIDTH), F32)],
        scratch_shapes=[pltpu.VMEM((N_HEADS * T, N_KEYS), F32),
                        pltpu.VMEM((N_HEADS * T, LANES), F32),
                        pltpu.VMEM((2, T, N_KEYS), F32),
                        pltpu.VMEM((ns, N_KEYS, LANES), BF16),
                        pltpu.VMEM((ns, N_KEYS, LANES), BF16),
                        pltpu.VMEM((ns, POOL_PAD + T, POOL_WIDTH), F32),
                        pltpu.VMEM((ns, N_HEADS * T, LANES), F32),
                        pltpu.VMEM((ns, N_HEADS * T, LANES), F32)],
        compiler_params=pltpu.CompilerParams(dimension_semantics=("arbitrary",),
                                             vmem_limit_bytes=VMEM_LIMIT_BYTES),
        name="mix_sample",
    )(x_sample.reshape(n_tok, D_MODEL), ck, cv, state16, g_pre, w_in_b, b_in2, rel_bias_table,
      sinks2, _bucket_matrix(T), *weights)
    y_s = _mlp(h_s, g_pre_mlp, w_mlp_up_b, w_mlp_down_b, g_post_mlp)

    kv_shape = (1, DB, WINDOW, N_KV_HEADS, HEAD_DIM)
    k_s = jnp.concatenate([ck[:, T:], k_new.reshape(DB, T, KV_WIDTH)], axis=1).reshape(kv_shape)
    v_s = jnp.concatenate([cv[:, T:], v_new.reshape(DB, T, KV_WIDTH)], axis=1).reshape(kv_shape)
    p_s = jnp.concatenate([state_pool[l][:, T:], up_new.reshape(DB, T, POOL_WIDTH)], axis=1)[None]

    return (y_p.reshape(B, S, D_MODEL), y_s.reshape(DB, T, D_MODEL),
            kwin.reshape(1, B, WINDOW, N_KV_HEADS, HEAD_DIM),
            vwin.reshape(1, B, WINDOW, N_KV_HEADS, HEAD_DIM),
            pstate[:, POOL_PAD - POOL_STATE:][None],
            k_s, v_s, p_s)
```

```python
import functools
import math

import jax
import jax.numpy as jnp
from jax import lax
from jax.experimental import pallas as pl
from jax.experimental.pallas import tpu as pltpu

D_MODEL = 1024
N_HEADS = 8
HEAD_DIM = 64
N_KV_HEADS = 2
GROUP = N_HEADS // N_KV_HEADS
ATTN_WIDTH = N_HEADS * HEAD_DIM
KV_WIDTH = N_KV_HEADS * HEAD_DIM
WINDOW = 128
POOL_WIDTH = D_MODEL // 2
POOL_WINDOWS = (2, 4, 8, 16)
POOL_GROUP_WIDTH = POOL_WIDTH // len(POOL_WINDOWS)
POOL_STATE = max(POOL_WINDOWS) - 1
POOL_PAD = POOL_STATE + 1
D_FF = 4 * D_MODEL
N_BUCKETS = 32
MAX_DISTANCE = 128
RMS_EPS = 1e-6
NEG_INF = -1e30
LOG2E = math.log2(math.e)
PAST_LEN = 16384
IN_WIDTH = ATTN_WIDTH + 2 * KV_WIDTH + POOL_WIDTH + 2 * D_MODEL
OFF_K = ATTN_WIDTH
OFF_V = OFF_K + KV_WIDTH
OFF_POOL = OFF_V + KV_WIDTH
OFF_GA = OFF_POOL + POOL_WIDTH
OFF_GP = OFF_GA + D_MODEL
N_KEYS = 2 * WINDOW

LANES = 128
SUBLANES = 8
VMEM_LIMIT_BYTES = 56 * 1024 * 1024

TOKEN_TILE = 512
DEC_SEQ = 8
SAMPLE_TILE = 256
SEQ_PER_TILE = SAMPLE_TILE // DEC_SEQ

BF16 = jnp.bfloat16
F32 = jnp.float32


def _rms(x, g):
    return x * lax.rsqrt(jnp.mean(x * x, axis=-1, keepdims=True) + RMS_EPS) * g


def _dot(a, b):
    return jnp.dot(a, b, preferred_element_type=F32)


def _dot_nt(a, b):
    return lax.dot_general(a, b, (((1,), (1,)), ((), ())), preferred_element_type=F32)


def _build_tables(rows, scale, bucket_ref, table_ref, sinks_ref, bias_ref, sink_ref, valid_ref):
    bucket = bucket_ref[...]
    for h in range(N_HEADS):
        acc = jnp.zeros(bucket.shape, F32)
        for b in range(N_BUCKETS):
            acc = jnp.where(bucket == b, table_ref[b, h] * scale, acc)
        bias_ref[h * rows:(h + 1) * rows, :] = acc
        if sink_ref is not None:
            sink_ref[h * rows:(h + 1) * rows, :] = jnp.full((rows, LANES), sinks_ref[0, h], F32)
    r = lax.broadcasted_iota(jnp.int32, (rows, N_KEYS), 0)
    j = lax.broadcasted_iota(jnp.int32, (rows, N_KEYS), 1)
    cur = jnp.where(j >= WINDOW, jnp.where(j - WINDOW <= r, 1.0, 0.0), 0.0)
    prev = jnp.where(j < WINDOW, jnp.where(j > r, 1.0, 0.0), 0.0)
    valid_ref[0] = cur
    valid_ref[1] = cur + prev


def _attend(rows, chunk, qh, kcat, vcat, valid, bias_ref, sink_ref):
    s_all = _dot_nt(qh, kcat)
    reps = chunk // rows
    if reps > 1:
        valid = jnp.concatenate([valid] * reps, axis=0)
    ok = valid > 0.5
    ps = []
    for c in range(N_HEADS * rows // chunk):
        sl = slice(c * chunk, (c + 1) * chunk)
        s = jnp.where(ok, s_all[sl] + bias_ref[sl, :], NEG_INF)
        sink = sink_ref[sl, 0:1]
        m = jnp.maximum(jnp.max(s, axis=-1, keepdims=True), sink)
        p = jnp.exp(s - m)
        denom = jnp.sum(p, axis=-1, keepdims=True) + jnp.exp(sink - m)
        ps.append(p / denom)
    p_all = jnp.concatenate(ps, axis=0).astype(BF16) if len(ps) > 1 else ps[0].astype(BF16)
    return _dot(p_all, vcat)


def _head_queries(q):
    lane = lax.broadcasted_iota(jnp.int32, (q.shape[0], LANES), 1)
    lo = lane < HEAD_DIM
    out = []
    for c in range(ATTN_WIDTH // LANES):
        qc = q[:, c * LANES:(c + 1) * LANES]
        qr = pltpu.roll(qc, HEAD_DIM, axis=1)
        g = (2 * c) // GROUP
        if g == 0:
            out.append(jnp.where(lo, qc, 0.0))
            out.append(jnp.where(lo, qr, 0.0))
        else:
            out.append(jnp.where(lo, 0.0, qr))
            out.append(jnp.where(lo, 0.0, qc))
    return out


def _merge_heads(o_heads):
    lane = lax.broadcasted_iota(jnp.int32, o_heads[0].shape, 1)
    lo = lane < HEAD_DIM
    chunks = []
    for c in range(ATTN_WIDTH // LANES):
        e, o = o_heads[2 * c], o_heads[2 * c + 1]
        g = (2 * c) // GROUP
        if g == 0:
            chunks.append(jnp.where(lo, e, pltpu.roll(o, HEAD_DIM, axis=1)))
        else:
            chunks.append(jnp.where(lo, pltpu.roll(e, HEAD_DIM, axis=1), o))
    return jnp.concatenate(chunks, axis=1)


def _pool_means(ext, first_pos):
    del first_pos
    parts = []
    for gi, w in enumerate(POOL_WINDOWS):
        acc = ext[:, gi * POOL_GROUP_WIDTH:(gi + 1) * POOL_GROUP_WIDTH]
        shift = 1
        while shift < w:
            acc = acc + pltpu.roll(acc, shift, axis=0)
            shift *= 2
        parts.append(acc)
    return jnp.concatenate(parts, axis=1)


def _pool_counts(pos):
    lane = lax.broadcasted_iota(jnp.int32, (pos.shape[0], POOL_WIDTH), 1)
    w = jnp.left_shift(2, lane // POOL_GROUP_WIDTH)
    return jnp.minimum(pos + 1, w).astype(F32)


def _mix_tail(x, xn, attn_o, pool_sum, up, cnt, w_in_ref, b_in_ref, w_grp_ref, scale_ref,
              w_attn_up_ref, w_pool_up_ref, w_out_ref, g_post_ref):
    z = (pool_sum / cnt - up).astype(BF16)
    pool_z = (_dot(z, w_grp_ref[...]) * scale_ref[...]).astype(BF16)
    ga = _dot(xn, w_in_ref[:, OFF_GA:OFF_GP]) + b_in_ref[:, OFF_GA:OFF_GP]
    m = jax.nn.sigmoid(ga) * _dot(attn_o, w_attn_up_ref[...])
    gp = _dot(xn, w_in_ref[:, OFF_GP:IN_WIDTH]) + b_in_ref[:, OFF_GP:IN_WIDTH]
    m = m + jax.nn.sigmoid(gp) * _dot(pool_z, w_pool_up_ref[...])
    mo = _dot(m.astype(BF16), w_out_ref[...])
    return x + _rms(mo, g_post_ref[...])


def _attend_prompt_block(base, has_prev, q_ref, ke_ref, ko_ref, ve_ref, vo_ref, valid_ref, bias_ref,
                         sinks_ref, p_ref, ao_ref):
    ok = valid_ref[has_prev] > 0.5
    lane = lax.broadcasted_iota(jnp.int32, (WINDOW, LANES), 1)
    lo = lane < HEAD_DIM
    rows = pl.ds(base, WINDOW)
    keys = pl.ds(base, N_KEYS)
    for g in range(N_KV_HEADS):
        qg = jnp.concatenate([q_ref[rows, (2 * g) * LANES:(2 * g + 1) * LANES],
                              q_ref[rows, (2 * g + 1) * LANES:(2 * g + 2) * LANES]], axis=0)
        s_par = (_dot_nt(qg, ke_ref[g, keys, :]), _dot_nt(qg, ko_ref[g, keys, :]))
        es = {}
        for a in range(2):
            for b in range(2):
                h = GROUP * g + 2 * a + b
                s = s_par[b][a * WINDOW:(a + 1) * WINDOW]
                s = jnp.where(ok, s + bias_ref[h * WINDOW:(h + 1) * WINDOW, :], NEG_INF)
                sink = sinks_ref[0, h] * LOG2E
                m = jnp.maximum(jnp.max(s, axis=-1, keepdims=True), sink)
                p_ref[2 * g + a, :, b * N_KEYS:(b + 1) * N_KEYS] = jnp.exp2(s - m).astype(BF16)
                es[(a, b)] = jnp.exp2(sink - m)
        pg = p_ref[2 * g:2 * g + 2].reshape(2 * WINDOW, 2 * N_KEYS)
        o_ext = (_dot(pg[:, :N_KEYS], ve_ref[g, keys, :]) + _dot(pg[:, N_KEYS:], vo_ref[g, keys, :]))
        for a in range(2):
            c = 2 * g + a
            o = o_ext[a * WINDOW:(a + 1) * WINDOW]
            denom = o[:, LANES:] + jnp.where(lo, es[(a, 0)], es[(a, 1)])
            ao_ref[rows, c * LANES:(c + 1) * LANES] = (o[:, :LANES] / denom).astype(BF16)


def _mix_prompt_kernel(x_ref, g_pre_ref, w_in_ref, b_in_ref, table_ref, sinks_ref, bucket_ref,
                       w_grp_ref, scale_ref, w_attn_up_ref, w_pool_up_ref, w_out_ref, g_post_ref,
                       h_ref, kwin_ref, vwin_ref, pstate_ref,
                       bias_ref, valid_ref, ke_ref, ko_ref, ve_ref, vo_ref, ext_ref, q_ref, p_ref,
                       ao_ref):
    tm = TOKEN_TILE
    t = pl.program_id(1)
    lane = lax.broadcasted_iota(jnp.int32, (tm, LANES), 1)
    lo = lane < HEAD_DIM

    @pl.when((pl.program_id(0) == 0) & (t == 0))
    def _():
        _build_tables(WINDOW, LOG2E, bucket_ref, table_ref, sinks_ref, bias_ref, None, valid_ref)
        ones_lo = jnp.where(lax.broadcasted_iota(jnp.int32, (WINDOW + tm, LANES), 1) < HEAD_DIM,
                            1.0, 0.0).astype(BF16)
        for g in range(N_KV_HEADS):
            ve_ref[g, :, LANES:] = ones_lo
            vo_ref[g, :, LANES:] = 1.0 - ones_lo

    @pl.when(t == 0)
    def _():
        zeros = jnp.zeros((WINDOW, LANES), BF16)
        for g in range(N_KV_HEADS):
            ke_ref[g, 0:WINDOW, :] = zeros
            ko_ref[g, 0:WINDOW, :] = zeros
            ve_ref[g, 0:WINDOW, 0:LANES] = zeros
            vo_ref[g, 0:WINDOW, 0:LANES] = zeros
        ext_ref[0:POOL_PAD, :] = jnp.zeros((POOL_PAD, POOL_WIDTH), F32)

    x = x_ref[...]
    xn = _rms(x, g_pre_ref[...]).astype(BF16)
    u = _dot(xn, w_in_ref[:, 0:OFF_GA]) + b_in_ref[:, 0:OFF_GA]
    k = u[:, OFF_K:OFF_V]
    v = u[:, OFF_V:OFF_POOL]
    up = u[:, OFF_POOL:OFF_GA]

    q_ref[...] = (u[:, 0:OFF_K] * (HEAD_DIM ** -0.5 * LOG2E)).astype(BF16)
    kwin_ref[...] = k[tm - WINDOW:, :]
    vwin_ref[...] = v[tm - WINDOW:, :]
    pstate_ref[...] = up[tm - POOL_PAD:, :]
    ext_ref[POOL_PAD:, :] = up
    kr = pltpu.roll(k, HEAD_DIM, axis=1)
    vr = pltpu.roll(v, HEAD_DIM, axis=1)
    ke_ref[0, WINDOW:, :] = jnp.where(lo, k, 0.0).astype(BF16)
    ko_ref[0, WINDOW:, :] = jnp.where(lo, 0.0, kr).astype(BF16)
    ke_ref[1, WINDOW:, :] = jnp.where(lo, kr, 0.0).astype(BF16)
    ko_ref[1, WINDOW:, :] = jnp.where(lo, 0.0, k).astype(BF16)
    ve_ref[0, WINDOW:, 0:LANES] = jnp.where(lo, v, 0.0).astype(BF16)
    vo_ref[0, WINDOW:, 0:LANES] = jnp.where(lo, 0.0, vr).astype(BF16)
    ve_ref[1, WINDOW:, 0:LANES] = jnp.where(lo, vr, 0.0).astype(BF16)
    vo_ref[1, WINDOW:, 0:LANES] = jnp.where(lo, 0.0, v).astype(BF16)

    def block(i, carry):
        base = pl.multiple_of(i * WINDOW, WINDOW)
        has_prev = jnp.where((t == 0) & (i == 0), 0, 1)
        _attend_prompt_block(base, has_prev, q_ref, ke_ref, ko_ref, ve_ref, vo_ref, valid_ref,
                             bias_ref, sinks_ref, p_ref, ao_ref)
        return carry

    lax.fori_loop(0, tm // WINDOW, block, 0)

    pool_sum = _pool_means(ext_ref[...], None)[POOL_PAD:, :]
    pos = t * tm + lax.broadcasted_iota(jnp.int32, (tm, 1), 0)
    h_ref[...] = _mix_tail(x, xn, ao_ref[...], pool_sum, up, _pool_counts(pos),
                           w_in_ref, b_in_ref, w_grp_ref, scale_ref,
                           w_attn_up_ref, w_pool_up_ref, w_out_ref, g_post_ref)

    for g in range(N_KV_HEADS):
        ke_ref[g, 0:WINDOW, :] = ke_ref[g, tm:tm + WINDOW, :]
        ko_ref[g, 0:WINDOW, :] = ko_ref[g, tm:tm + WINDOW, :]
        ve_ref[g, 0:WINDOW, 0:LANES] = ve_ref[g, tm:tm + WINDOW, 0:LANES]
        vo_ref[g, 0:WINDOW, 0:LANES] = vo_ref[g, tm:tm + WINDOW, 0:LANES]
    ext_ref[0:POOL_PAD, :] = ext_ref[tm:tm + POOL_PAD, :]


def _mix_sample_kernel(x_ref, ck_ref, cv_ref, state_ref, g_pre_ref, w_in_ref, b_in_ref, table_ref,
                       sinks_ref, bucket_ref, w_grp_ref, scale_ref, w_attn_up_ref, w_pool_up_ref,
                       w_out_ref, g_post_ref,
                       h_ref, knew_ref, vnew_ref, up_ref,
                       bias_ref, sink_ref, valid_ref, kx_ref, vx_ref, ext_ref, qh_ref, oh_ref):
    tm = SAMPLE_TILE
    ns = SEQ_PER_TILE
    T = DEC_SEQ

    @pl.when(pl.program_id(0) == 0)
    def _():
        _build_tables(T, 1.0, bucket_ref, table_ref, sinks_ref, bias_ref, sink_ref, valid_ref)

    x = x_ref[...]
    xn = _rms(x, g_pre_ref[...]).astype(BF16)
    u = _dot(xn, w_in_ref[:, 0:OFF_GA]) + b_in_ref[:, 0:OFF_GA]
    q = u[:, 0:OFF_K] * (HEAD_DIM ** -0.5)
    k = u[:, OFF_K:OFF_V]
    v = u[:, OFF_V:OFF_POOL]
    up = u[:, OFF_POOL:OFF_GA]

    knew_ref[...] = k
    vnew_ref[...] = v
    up_ref[...] = up
    kx_ref[:, 0:WINDOW, :] = ck_ref[...].astype(BF16)
    vx_ref[:, 0:WINDOW, :] = cv_ref[...].astype(BF16)
    pad = jnp.zeros((ns, WINDOW - T, LANES), F32)
    kx_ref[:, WINDOW:, :] = jnp.concatenate([k.reshape(ns, T, LANES), pad], axis=1).astype(BF16)
    vx_ref[:, WINDOW:, :] = jnp.concatenate([v.reshape(ns, T, LANES), pad], axis=1).astype(BF16)
    ext_ref[:, 0:POOL_PAD, :] = state_ref[...]
    ext_ref[:, POOL_PAD:, :] = up.reshape(ns, T, POOL_WIDTH)
    for h, qh in enumerate(_head_queries(q)):
        qh_ref[:, h * T:(h + 1) * T, :] = qh.reshape(ns, T, LANES)

    valid = valid_ref[1]

    def seq(s, carry):
        qh = qh_ref[s].astype(BF16)
        oh_ref[s] = _attend(T, N_HEADS * T, qh, kx_ref[s], vx_ref[s], valid, bias_ref, sink_ref)
        return carry

    lax.fori_loop(0, ns, seq, 0, unroll=4)

    o_heads = [oh_ref[:, h * T:(h + 1) * T, :].reshape(tm, LANES) for h in range(N_HEADS)]
    attn_o = _merge_heads(o_heads).astype(BF16)

    ext = ext_ref[...].reshape(ns * (POOL_PAD + T), POOL_WIDTH)
    pool_sum = _pool_means(ext, None).reshape(ns, POOL_PAD + T, POOL_WIDTH)[:, POOL_PAD:, :]
    pool_sum = pool_sum.reshape(tm, POOL_WIDTH)
    row = lax.broadcasted_iota(jnp.int32, (tm, 1), 0)
    pos = PAST_LEN + (row & (T - 1))
    h_ref[...] = _mix_tail(x, xn, attn_o, pool_sum, up, _pool_counts(pos),
                           w_in_ref, b_in_ref, w_grp_ref, scale_ref,
                           w_attn_up_ref, w_pool_up_ref, w_out_ref, g_post_ref)


def _mlp_kernel(h_ref, g_pre_ref, w_up_ref, w_down_ref, g_post_ref, y_ref):
    h = h_ref[...]
    hn = _rms(h, g_pre_ref[...]).astype(BF16)
    f = jnp.zeros(h.shape, F32)
    chunk = D_FF // 4
    for c in range(D_FF // chunk):
        a = jnp.maximum(_dot(hn, w_up_ref[:, c * chunk:(c + 1) * chunk]), 0.0)
        f = f + _dot((a * a).astype(BF16), w_down_ref[c * chunk:(c + 1) * chunk, :])
    y_ref[...] = h + _rms(f, g_post_ref[...])


def _full(shape):
    return pl.BlockSpec(shape, lambda *_: (0,) * len(shape))


def _smem():
    return pl.BlockSpec(memory_space=pltpu.SMEM)


def _rel_bucket(dist):
    d = jnp.maximum(dist, 0)
    max_exact = N_BUCKETS // 2
    large = max_exact + (jnp.log(jnp.maximum(d, 1).astype(F32) / max_exact)
                         / math.log(MAX_DISTANCE / max_exact)
                         * (N_BUCKETS - max_exact)).astype(jnp.int32)
    large = jnp.minimum(large, N_BUCKETS - 1)
    return jnp.where(d < max_exact, d, large)


def _bucket_matrix(rows):
    r = jnp.arange(rows)[:, None]
    j = jnp.arange(N_KEYS)[None, :]
    return _rel_bucket(r + WINDOW - j).astype(jnp.int32)


def _mlp(h2d, g_pre, w_up, w_down, g_post):
    n = h2d.shape[0]
    return pl.pallas_call(
        _mlp_kernel,
        grid=(n // TOKEN_TILE,),
        in_specs=[pl.BlockSpec((TOKEN_TILE, D_MODEL), lambda i: (i, 0)),
                  _full((1, D_MODEL)), _full((D_MODEL, D_FF)), _full((D_FF, D_MODEL)),
                  _full((1, D_MODEL))],
        out_specs=pl.BlockSpec((TOKEN_TILE, D_MODEL), lambda i: (i, 0)),
        out_shape=jax.ShapeDtypeStruct((n, D_MODEL), F32),
        compiler_params=pltpu.CompilerParams(dimension_semantics=("arbitrary",),
                                             vmem_limit_bytes=VMEM_LIMIT_BYTES),
        name="mlp",
    )(h2d, g_pre, w_up, w_down, g_post)


def kernel(x_prompt, x_sample, cache_k_win, cache_v_win, state_pool, norm_pre_mix, norm_post_mix,
           norm_pre_mlp, norm_post_mlp, w_in, b_in, attn_sinks, rel_bias_table, w_attn_up,
           w_pool_grp, pool_scale, w_pool_up, w_out, w_mlp_up, w_mlp_down):
    B, S, _ = x_prompt.shape
    DB, T, _ = x_sample.shape
    depth = w_in.shape[0]
    assert depth == 1 and S % TOKEN_TILE == 0 and (DB * T) % TOKEN_TILE == 0
    assert T == DEC_SEQ and (DB * T) % SAMPLE_TILE == 0

    l = 0
    w_in_b = w_in[l].astype(BF16)
    w_grp_b = jax.scipy.linalg.block_diag(*[w_pool_grp[l, g] for g in range(len(POOL_WINDOWS))]
                                          ).astype(BF16)
    w_attn_up_b = w_attn_up[l].astype(BF16)
    w_pool_up_b = w_pool_up[l].astype(BF16)
    w_out_b = w_out[l].astype(BF16)
    w_mlp_up_b = w_mlp_up[l].astype(BF16)
    w_mlp_down_b = w_mlp_down[l].astype(BF16)
    b_in2 = b_in[l][None, :]
    sinks2 = attn_sinks[l][None, :]
    scale2 = pool_scale[l][None, :]
    g_pre, g_post = norm_pre_mix[l][None, :], norm_post_mix[l][None, :]
    g_pre_mlp, g_post_mlp = norm_pre_mlp[l][None, :], norm_post_mlp[l][None, :]

    weight_specs = [_full((POOL_WIDTH, POOL_WIDTH)), _full((1, POOL_WIDTH)),
                    _full((ATTN_WIDTH, D_MODEL)), _full((POOL_WIDTH, D_MODEL)),
                    _full((D_MODEL, D_MODEL)), _full((1, D_MODEL))]
    weights = (w_grp_b, scale2, w_attn_up_b, w_pool_up_b, w_out_b, g_post)

    tm = TOKEN_TILE
    h_p, kwin, vwin, pstate = pl.pallas_call(
        _mix_prompt_kernel,
        grid=(B, S // tm),
        in_specs=[pl.BlockSpec((None, tm, D_MODEL), lambda b, t: (b, t, 0)),
                  _full((1, D_MODEL)), _full((D_MODEL, IN_WIDTH)), _full((1, IN_WIDTH)),
                  _smem(), _smem(), _full((WINDOW, N_KEYS))] + weight_specs,
        out_specs=[pl.BlockSpec((None, tm, D_MODEL), lambda b, t: (b, t, 0)),
                   pl.BlockSpec((None, WINDOW, KV_WIDTH), lambda b, t: (b, 0, 0)),
                   pl.BlockSpec((None, WINDOW, KV_WIDTH), lambda b, t: (b, 0, 0)),
                   pl.BlockSpec((None, POOL_PAD, POOL_WIDTH), lambda b, t: (b, 0, 0))],
        out_shape=[jax.ShapeDtypeStruct((B, S, D_MODEL), F32),
                   jax.ShapeDtypeStruct((B, WINDOW, KV_WIDTH), F32),
                   jax.ShapeDtypeStruct((B, WINDOW, KV_WIDTH), F32),
                   jax.ShapeDtypeStruct((B, POOL_PAD, POOL_WIDTH), F32)],
        scratch_shapes=[pltpu.VMEM((N_HEADS * WINDOW, N_KEYS), F32),
                        pltpu.VMEM((2, WINDOW, N_KEYS), F32),
                        pltpu.VMEM((N_KV_HEADS, WINDOW + tm, LANES), BF16),
                        pltpu.VMEM((N_KV_HEADS, WINDOW + tm, LANES), BF16),
                        pltpu.VMEM((N_KV_HEADS, WINDOW + tm, 2 * LANES), BF16),
                        pltpu.VMEM((N_KV_HEADS, WINDOW + tm, 2 * LANES), BF16),
                        pltpu.VMEM((POOL_PAD + tm, POOL_WIDTH), F32),
                        pltpu.VMEM((tm, ATTN_WIDTH), BF16),
                        pltpu.VMEM((ATTN_WIDTH // LANES, WINDOW, 2 * N_KEYS), BF16),
                        pltpu.VMEM((tm, ATTN_WIDTH), BF16)],
        compiler_params=pltpu.CompilerParams(dimension_semantics=("arbitrary", "arbitrary"),
                                             vmem_limit_bytes=VMEM_LIMIT_BYTES),
        name="mix_prompt",
    )(x_prompt, g_pre, w_in_b, b_in2, rel_bias_table, sinks2, _bucket_matrix(WINDOW), *weights)
    y_p = _mlp(h_p.reshape(B * S, D_MODEL), g_pre_mlp, w_mlp_up_b, w_mlp_down_b, g_post_mlp)

    tm, ns = SAMPLE_TILE, SEQ_PER_TILE
    n_tok = DB * T
    state16 = jnp.pad(state_pool[l], ((0, 0), (POOL_PAD - POOL_STATE, 0), (0, 0)))
    ck = cache_k_win[l].reshape(DB, WINDOW, KV_WIDTH)
    cv = cache_v_win[l].reshape(DB, WINDOW, KV_WIDTH)
    h_s, k_new, v_new, up_new = pl.pallas_call(
        _mix_sample_kernel,
        grid=(n_tok // tm,),
        in_specs=[pl.BlockSpec((tm, D_MODEL), lambda i: (i, 0)),
                  pl.BlockSpec((ns, WINDOW, KV_WIDTH), lambda i: (i, 0, 0)),
                  pl.BlockSpec((ns, WINDOW, KV_WIDTH), lambda i: (i, 0, 0)),
                  pl.BlockSpec((ns, POOL_PAD, POOL_WIDTH), lambda i: (i, 0, 0)),
                  _full((1, D_MODEL)), _full((D_MODEL, IN_WIDTH)), _full((1, IN_WIDTH)),
                  _smem(), _smem(), _full((T, N_KEYS))] + weight_specs,
        out_specs=[pl.BlockSpec((tm, D_MODEL), lambda i: (i, 0)),
                   pl.BlockSpec((tm, KV_WIDTH), lambda i: (i, 0)),
                   pl.BlockSpec((tm, KV_WIDTH), lambda i: (i, 0)),
                   pl.BlockSpec((tm, POOL_WIDTH), lambda i: (i, 0))],
        out_shape=[jax.ShapeDtypeStruct((n_tok, D_MODEL), F32),
                   jax.ShapeDtypeStruct((n_tok, KV_WIDTH), F32),
                   jax.ShapeDtypeStruct((n_tok, KV_WIDTH), F32),
                   jax.ShapeDtypeStruct((n_tok, POOL_WIDTH), F32)],
        scratch_shapes=[pltpu.VMEM((N_HEADS * T, N_KEYS), F32),
                        pltpu.VMEM((N_HEADS * T, LANES), F32),
                        pltpu.VMEM((2, T, N_KEYS), F32),
                        pltpu.VMEM((ns, N_KEYS, LANES), BF16),
                        pltpu.VMEM((ns, N_KEYS, LANES), BF16),
                        pltpu.VMEM((ns, POOL_PAD + T, POOL_WIDTH), F32),
                        pltpu.VMEM((ns, N_HEADS * T, LANES), F32),
                        pltpu.VMEM((ns, N_HEADS * T, LANES), F32)],
        compiler_params=pltpu.CompilerParams(dimension_semantics=("arbitrary",),
                                             vmem_limit_bytes=VMEM_LIMIT_BYTES),
        name="mix_sample",
    )(x_sample.reshape(n_tok, D_MODEL), ck, cv, state16, g_pre, w_in_b, b_in2, rel_bias_table,
      sinks2, _bucket_matrix(T), *weights)
    y_s = _mlp(h_s, g_pre_mlp, w_mlp_up_b, w_mlp_down_b, g_post_mlp)

    kv_shape = (1, DB, WINDOW, N_KV_HEADS, HEAD_DIM)
    k_s = jnp.concatenate([ck[:, T:], k_new.reshape(DB, T, KV_WIDTH)], axis=1).reshape(kv_shape)
    v_s = jnp.concatenate([cv[:, T:], v_new.reshape(DB, T, KV_WIDTH)], axis=1).reshape(kv_shape)
    p_s = jnp.concatenate([state_pool[l][:, T:], up_new.reshape(DB, T, POOL_WIDTH)], axis=1)[None]

    return (y_p.reshape(B, S, D_MODEL), y_s.reshape(DB, T, D_MODEL),
            kwin.reshape(1, B, WINDOW, N_KV_HEADS, HEAD_DIM),
            vwin.reshape(1, B, WINDOW, N_KV_HEADS, HEAD_DIM),
            pstate[:, POOL_PAD - POOL_STATE:][None],
            k_s, v_s, p_s)
```

```python
import functools
import math

import jax
import jax.numpy as jnp
from jax import lax
from jax.experimental import pallas as pl
from jax.experimental.pallas import tpu as pltpu

D_MODEL = 1024
N_HEADS = 8
HEAD_DIM = 64
N_KV_HEADS = 2
GROUP = N_HEADS // N_KV_HEADS
ATTN_WIDTH = N_HEADS * HEAD_DIM
KV_WIDTH = N_KV_HEADS * HEAD_DIM
WINDOW = 128
POOL_WIDTH = D_MODEL // 2
POOL_WINDOWS = (2, 4, 8, 16)
POOL_GROUP_WIDTH = POOL_WIDTH // len(POOL_WINDOWS)
POOL_STATE = max(POOL_WINDOWS) - 1
POOL_PAD = POOL_STATE + 1
D_FF = 4 * D_MODEL
N_BUCKETS = 32
MAX_DISTANCE = 128
RMS_EPS = 1e-6
NEG_INF = -1e30
LOG2E = math.log2(math.e)
PAST_LEN = 16384
IN_WIDTH = ATTN_WIDTH + 2 * KV_WIDTH + POOL_WIDTH + 2 * D_MODEL
OFF_K = ATTN_WIDTH
OFF_V = OFF_K + KV_WIDTH
OFF_POOL = OFF_V + KV_WIDTH
OFF_GA = OFF_POOL + POOL_WIDTH
OFF_GP = OFF_GA + D_MODEL
N_KEYS = 2 * WINDOW
GATE_CHUNKS = 4
GATE_CHUNK = (IN_WIDTH - OFF_GA) // GATE_CHUNKS

LANES = 128
SUBLANES = 8
VMEM_LIMIT_BYTES = 56 * 1024 * 1024

TOKEN_TILE = 512
DEC_SEQ = 8
SAMPLE_TILE = 256
SEQ_PER_TILE = SAMPLE_TILE // DEC_SEQ
SEQ_GROUP = 8

BF16 = jnp.bfloat16
F32 = jnp.float32


def _rms(x, g):
    return x * lax.rsqrt(jnp.mean(x * x, axis=-1, keepdims=True) + RMS_EPS) * g


def _dot(a, b):
    return jnp.dot(a, b, preferred_element_type=F32)


def _dot_nt(a, b):
    return lax.dot_general(a, b, (((1,), (1,)), ((), ())), preferred_element_type=F32)


def _build_tables(rows, scale, bucket_ref, table_ref, sinks_ref, bias_ref, sink_ref, valid_ref):
    bucket = bucket_ref[...]
    for h in range(N_HEADS):
        acc = jnp.zeros(bucket.shape, F32)
        for b in range(N_BUCKETS):
            acc = jnp.where(bucket == b, table_ref[b, h] * scale, acc)
        bias_ref[h * rows:(h + 1) * rows, :] = acc
        if sink_ref is not None:
            sink_ref[h * rows:(h + 1) * rows, :] = jnp.full((rows, LANES), sinks_ref[0, h] * scale, F32)
    r = lax.broadcasted_iota(jnp.int32, (rows, N_KEYS), 0)
    j = lax.broadcasted_iota(jnp.int32, (rows, N_KEYS), 1)
    cur = jnp.where(j >= WINDOW, jnp.where(j - WINDOW <= r, 1.0, 0.0), 0.0)
    prev = jnp.where(j < WINDOW, jnp.where(j > r, 1.0, 0.0), 0.0)
    valid_ref[0] = cur
    valid_ref[1] = cur + prev


def _head_queries(q):
    lane = lax.broadcasted_iota(jnp.int32, (q.shape[0], LANES), 1)
    lo = lane < HEAD_DIM
    out = []
    for c in range(ATTN_WIDTH // LANES):
        qc = q[:, c * LANES:(c + 1) * LANES]
        qr = pltpu.roll(qc, HEAD_DIM, axis=1)
        g = (2 * c) // GROUP
        if g == 0:
            out.append(jnp.where(lo, qc, 0.0))
            out.append(jnp.where(lo, qr, 0.0))
        else:
            out.append(jnp.where(lo, 0.0, qr))
            out.append(jnp.where(lo, 0.0, qc))
    return out


def _merge_heads(o_heads):
    lane = lax.broadcasted_iota(jnp.int32, o_heads[0].shape, 1)
    lo = lane < HEAD_DIM
    chunks = []
    for c in range(ATTN_WIDTH // LANES):
        e, o = o_heads[2 * c], o_heads[2 * c + 1]
        g = (2 * c) // GROUP
        if g == 0:
            chunks.append(jnp.where(lo, e, pltpu.roll(o, HEAD_DIM, axis=1)))
        else:
            chunks.append(jnp.where(lo, pltpu.roll(e, HEAD_DIM, axis=1), o))
    return jnp.concatenate(chunks, axis=1)


def _pool_means(ext, first_pos):
    del first_pos
    parts = []
    for gi, w in enumerate(POOL_WINDOWS):
        acc = ext[:, gi * POOL_GROUP_WIDTH:(gi + 1) * POOL_GROUP_WIDTH]
        shift = 1
        while shift < w:
            acc = acc + pltpu.roll(acc, shift, axis=0)
            shift *= 2
        parts.append(acc)
    return jnp.concatenate(parts, axis=1)


def _pool_counts(pos):
    lane = lax.broadcasted_iota(jnp.int32, (pos.shape[0], POOL_WIDTH), 1)
    w = jnp.left_shift(2, lane // POOL_GROUP_WIDTH)
    return jnp.minimum(pos + 1, w).astype(F32)


def _mix_tail(x, xn, attn_o, pool_sum, up, cnt, w_in_ref, b_in_ref, w_grp_ref, scale_ref,
              w_attn_up_ref, w_pool_up_ref, w_out_ref, g_post_ref):
    z = (pool_sum / cnt - up).astype(BF16)
    pool_z = (_dot(z, w_grp_ref[...]) * scale_ref[...]).astype(BF16)
    ga = _dot(xn, w_in_ref[:, OFF_GA:OFF_GP]) + b_in_ref[:, OFF_GA:OFF_GP]
    m = jax.nn.sigmoid(ga) * _dot(attn_o, w_attn_up_ref[...])
    gp = _dot(xn, w_in_ref[:, OFF_GP:IN_WIDTH]) + b_in_ref[:, OFF_GP:IN_WIDTH]
    m = m + jax.nn.sigmoid(gp) * _dot(pool_z, w_pool_up_ref[...])
    mo = _dot(m.astype(BF16), w_out_ref[...])
    return x + _rms(mo, g_post_ref[...])


def _attend_prompt_block(base, has_prev, q_ref, ke_ref, ko_ref, ve_ref, vo_ref, valid_ref, bias_ref,
                         sinks_ref, ao_ref):
    if isinstance(has_prev, int):
        ok = valid_ref[has_prev] > 0.5
    else:
        ok = jnp.where(has_prev > 0, valid_ref[1], valid_ref[0]) > 0.5
    lane = lax.broadcasted_iota(jnp.int32, (WINDOW, LANES), 1)
    lo = lane < HEAD_DIM
    rows = pl.ds(base, WINDOW)
    keys = pl.ds(base, N_KEYS)
    for g in range(N_KV_HEADS):
        qg = jnp.concatenate([q_ref[rows, (2 * g) * LANES:(2 * g + 1) * LANES],
                              q_ref[rows, (2 * g + 1) * LANES:(2 * g + 2) * LANES]], axis=0)
        s_par = (_dot_nt(qg, ke_ref[g, keys, :]), _dot_nt(qg, ko_ref[g, keys, :]))
        es, ps = {}, {}
        for a in range(2):
            for b in range(2):
                h = GROUP * g + 2 * a + b
                s = s_par[b][a * WINDOW:(a + 1) * WINDOW]
                s = jnp.where(ok, s + bias_ref[h * WINDOW:(h + 1) * WINDOW, :], NEG_INF)
                sink = sinks_ref[0, h] * LOG2E
                m = jnp.maximum(jnp.max(s, axis=-1, keepdims=True), sink)
                ps[(a, b)] = jnp.exp2(s - m).astype(BF16)
                es[(a, b)] = jnp.exp2(sink - m)
        o_ext = (_dot(jnp.concatenate([ps[(0, 0)], ps[(1, 0)]], axis=0), ve_ref[g, keys, :])
                 + _dot(jnp.concatenate([ps[(0, 1)], ps[(1, 1)]], axis=0), vo_ref[g, keys, :]))
        for a in range(2):
            c = 2 * g + a
            o = o_ext[a * WINDOW:(a + 1) * WINDOW]
            denom = o[:, LANES:] + jnp.where(lo, es[(a, 0)], es[(a, 1)])
            ao_ref[rows, c * LANES:(c + 1) * LANES] = (o[:, :LANES] / denom).astype(BF16)


def _mix_prompt_kernel(x_ref, g_pre_ref, w_in_ref, b_in_ref, w_gate_ref, b_gate_ref, table_ref,
                       sinks_ref, bucket_ref, pat_ref,
                       w_grp_ref, scale_ref, w_attn_up_ref, w_pool_up_ref, w_out_ref, g_post_ref,
                       h_ref, kwin_ref, vwin_ref, pstate_ref,
                       bias_ref, valid_ref, ke_ref, ko_ref, ve_ref, vo_ref, ext_ref, q_ref, ao_ref,
                       sg_ref, xn_ref):
    tm = TOKEN_TILE
    t = pl.program_id(1)
    lane = lax.broadcasted_iota(jnp.int32, (tm, LANES), 1)
    lo = lane < HEAD_DIM

    ones_lo, ones_hi = pat_ref[1], pat_ref[2]

    @pl.when((pl.program_id(0) == 0) & (t == 0))
    def _():
        _build_tables(WINDOW, LOG2E, bucket_ref, table_ref, sinks_ref, bias_ref, None, valid_ref)

    @pl.when(t == 0)
    def _():
        zeros = pat_ref[0, 0:WINDOW, :]
        for g in range(N_KV_HEADS):
            ke_ref[g, 0:WINDOW, :] = zeros
            ko_ref[g, 0:WINDOW, :] = zeros
            ve_ref[g, 0:WINDOW, :] = jnp.concatenate([zeros, pat_ref[1, 0:WINDOW, :]], axis=1)
            vo_ref[g, 0:WINDOW, :] = jnp.concatenate([zeros, pat_ref[2, 0:WINDOW, :]], axis=1)
        ext_ref[0:POOL_PAD, :] = jnp.zeros((POOL_PAD, POOL_WIDTH), F32)

    x = x_ref[...]
    xn = _rms(x, g_pre_ref[...]).astype(BF16)
    xn_ref[...] = xn
    u = _dot(xn, w_in_ref[...]) + b_in_ref[...]
    k = u[:, OFF_K:OFF_V]
    v = u[:, OFF_V:OFF_POOL]
    up = u[:, OFF_POOL:OFF_GA]

    q_ref[...] = (u[:, 0:OFF_K] * (HEAD_DIM ** -0.5 * LOG2E)).astype(BF16)
    kwin_ref[...] = k[tm - WINDOW:, :]
    vwin_ref[...] = v[tm - WINDOW:, :]
    pstate_ref[...] = up[tm - POOL_PAD:, :]
    ext_ref[POOL_PAD:, :] = up
    kr = pltpu.roll(k, HEAD_DIM, axis=1)
    vr = pltpu.roll(v, HEAD_DIM, axis=1)
    ke_ref[0, WINDOW:, :] = jnp.where(lo, k, 0.0).astype(BF16)
    ko_ref[0, WINDOW:, :] = jnp.where(lo, 0.0, kr).astype(BF16)
    ke_ref[1, WINDOW:, :] = jnp.where(lo, kr, 0.0).astype(BF16)
    ko_ref[1, WINDOW:, :] = jnp.where(lo, 0.0, k).astype(BF16)
    ve_ref[0, WINDOW:, :] = jnp.concatenate([jnp.where(lo, v, 0.0).astype(BF16), ones_lo], axis=1)
    vo_ref[0, WINDOW:, :] = jnp.concatenate([jnp.where(lo, 0.0, vr).astype(BF16), ones_hi], axis=1)
    ve_ref[1, WINDOW:, :] = jnp.concatenate([jnp.where(lo, vr, 0.0).astype(BF16), ones_lo], axis=1)
    vo_ref[1, WINDOW:, :] = jnp.concatenate([jnp.where(lo, 0.0, v).astype(BF16), ones_hi], axis=1)

    pool_sum = _pool_means(ext_ref[...], None)[POOL_PAD:, :]
    pos = t * tm + lax.broadcasted_iota(jnp.int32, (tm, 1), 0)
    z = (pool_sum / _pool_counts(pos) - up).astype(BF16)
    pool_z = (_dot(z, w_grp_ref[...]) * scale_ref[...]).astype(BF16)
    pp = _dot(pool_z, w_pool_up_ref[...])

    def block(i, carry):
        base = pl.multiple_of(i * WINDOW, WINDOW)
        has_prev = jnp.where((t == 0) & (i == 0), 0, 1)
        sg_ref[i] = jax.nn.sigmoid(_dot(xn_ref[...], w_gate_ref[i]) + b_gate_ref[i])
        _attend_prompt_block(base, has_prev, q_ref, ke_ref, ko_ref, ve_ref, vo_ref, valid_ref,
                             bias_ref, sinks_ref, ao_ref)
        return carry

    lax.fori_loop(0, GATE_CHUNKS, block, 0, unroll=2)

    a = _dot(ao_ref[...], w_attn_up_ref[...])
    half = D_MODEL // 2
    m = jnp.concatenate([sg_ref[0] * a[:, :half] + sg_ref[2] * pp[:, :half],
                         sg_ref[1] * a[:, half:] + sg_ref[3] * pp[:, half:]], axis=1)
    mo = _dot(m.astype(BF16), w_out_ref[...])
    h_ref[...] = x + _rms(mo, g_post_ref[...])

    for g in range(N_KV_HEADS):
        ke_ref[g, 0:WINDOW, :] = ke_ref[g, tm:tm + WINDOW, :]
        ko_ref[g, 0:WINDOW, :] = ko_ref[g, tm:tm + WINDOW, :]
        ve_ref[g, 0:WINDOW, :] = ve_ref[g, tm:tm + WINDOW, :]
        vo_ref[g, 0:WINDOW, :] = vo_ref[g, tm:tm + WINDOW, :]
    ext_ref[0:POOL_PAD, :] = ext_ref[tm:tm + POOL_PAD, :]


def _mix_sample_kernel(x_ref, ck_ref, cv_ref, state_ref, g_pre_ref, w_in_ref, b_in_ref, table_ref,
                       sinks_ref, bucket_ref, w_grp_ref, scale_ref, w_attn_up_ref, w_pool_up_ref,
                       w_out_ref, g_post_ref,
                       h_ref, kout_ref, vout_ref, up_ref,
                       bias_ref, sink_ref, valid_ref, vfull_ref, kx_ref, vx_ref, ext_ref, qh_ref,
                       oh_ref):
    tm = SAMPLE_TILE
    ns = SEQ_PER_TILE
    T = DEC_SEQ
    R = N_HEADS * T

    @pl.when(pl.program_id(0) == 0)
    def _():
        _build_tables(T, LOG2E, bucket_ref, table_ref, sinks_ref, bias_ref, sink_ref, valid_ref)
        for j in range(N_HEADS):
            vfull_ref[j * T:(j + 1) * T, :] = valid_ref[1]
        for j in range(1, SEQ_GROUP):
            bias_ref[j * R:(j + 1) * R, :] = bias_ref[0:R, :]
            sink_ref[j * R:(j + 1) * R, :] = sink_ref[0:R, :]
            vfull_ref[j * R:(j + 1) * R, :] = vfull_ref[0:R, :]
        vx_ref[:, :, LANES:] = jnp.ones((ns, N_KEYS, LANES), BF16)

    x = x_ref[...]
    xn = _rms(x, g_pre_ref[...]).astype(BF16)
    u = _dot(xn, w_in_ref[:, 0:OFF_GA]) + b_in_ref[:, 0:OFF_GA]
    q = u[:, 0:OFF_K] * (HEAD_DIM ** -0.5 * LOG2E)
    k = u[:, OFF_K:OFF_V]
    v = u[:, OFF_V:OFF_POOL]
    up = u[:, OFF_POOL:OFF_GA]
    k3 = k.reshape(ns, T, LANES)
    v3 = v.reshape(ns, T, LANES)
    ck = ck_ref[...]
    cv = cv_ref[...]

    kout_ref[:, 0:WINDOW - T, :] = ck[:, T:, :]
    kout_ref[:, WINDOW - T:, :] = k3
    vout_ref[:, 0:WINDOW - T, :] = cv[:, T:, :]
    vout_ref[:, WINDOW - T:, :] = v3
    up_ref[...] = up

    pad = jnp.zeros((ns, WINDOW - T, LANES), F32)
    kx_ref[:, 0:WINDOW, :] = ck.astype(BF16)
    kx_ref[:, WINDOW:, :] = jnp.concatenate([k3, pad], axis=1).astype(BF16)
    vx_ref[:, 0:WINDOW, 0:LANES] = cv.astype(BF16)
    vx_ref[:, WINDOW:, 0:LANES] = jnp.concatenate([v3, pad], axis=1).astype(BF16)
    ext_ref[:, 0:POOL_PAD, :] = state_ref[...]
    ext_ref[:, POOL_PAD:, :] = up.reshape(ns, T, POOL_WIDTH)
    for h, qh in enumerate(_head_queries(q)):
        qh_ref[:, h * T:(h + 1) * T, :] = qh.reshape(ns, T, LANES)

    def group(gi, carry):
        s0 = gi * SEQ_GROUP
        s = jnp.concatenate([_dot_nt(qh_ref[s0 + j].astype(BF16), kx_ref[s0 + j])
                             for j in range(SEQ_GROUP)], axis=0)
        s = jnp.where(vfull_ref[...] > 0.5, s + bias_ref[...], NEG_INF)
        sink = sink_ref[:, 0:1]
        m = jnp.maximum(jnp.max(s, axis=-1, keepdims=True), sink)
        p = jnp.exp2(s - m).astype(BF16)
        es = jnp.exp2(sink - m)
        for j in range(SEQ_GROUP):
            o = _dot(p[j * R:(j + 1) * R], vx_ref[s0 + j])
            oh_ref[s0 + j] = o[:, :LANES] / (o[:, LANES:] + es[j * R:(j + 1) * R])
        return carry

    lax.fori_loop(0, ns // SEQ_GROUP, group, 0)

    o_heads = [oh_ref[:, h * T:(h + 1) * T, :].reshape(tm, LANES) for h in range(N_HEADS)]
    attn_o = _merge_heads(o_heads).astype(BF16)

    ext = ext_ref[...].reshape(ns * (POOL_PAD + T), POOL_WIDTH)
    pool_sum = _pool_means(ext, None).reshape(ns, POOL_PAD + T, POOL_WIDTH)[:, POOL_PAD:, :]
    pool_sum = pool_sum.reshape(tm, POOL_WIDTH)
    row = lax.broadcasted_iota(jnp.int32, (tm, 1), 0)
    pos = PAST_LEN + (row & (T - 1))
    h_ref[...] = _mix_tail(x, xn, attn_o, pool_sum, up, _pool_counts(pos),
                           w_in_ref, b_in_ref, w_grp_ref, scale_ref,
                           w_attn_up_ref, w_pool_up_ref, w_out_ref, g_post_ref)


def _mlp_kernel(h_ref, g_pre_ref, w_up_ref, w_down_ref, g_post_ref, y_ref):
    h = h_ref[...]
    hn = _rms(h, g_pre_ref[...]).astype(BF16)
    f = jnp.zeros(h.shape, F32)
    chunk = D_FF // 4
    for c in range(D_FF // chunk):
        a = jnp.maximum(_dot(hn, w_up_ref[:, c * chunk:(c + 1) * chunk]), 0.0)
        f = f + _dot((a * a).astype(BF16), w_down_ref[c * chunk:(c + 1) * chunk, :])
    y_ref[...] = h + _rms(f, g_post_ref[...])


def _full(shape):
    return pl.BlockSpec(shape, lambda *_: (0,) * len(shape))


def _smem():
    return pl.BlockSpec(memory_space=pltpu.SMEM)


def _rel_bucket(dist):
    d = jnp.maximum(dist, 0)
    max_exact = N_BUCKETS // 2
    large = max_exact + (jnp.log(jnp.maximum(d, 1).astype(F32) / max_exact)
                         / math.log(MAX_DISTANCE / max_exact)
                         * (N_BUCKETS - max_exact)).astype(jnp.int32)
    large = jnp.minimum(large, N_BUCKETS - 1)
    return jnp.where(d < max_exact, d, large)


def _bucket_matrix(rows):
    r = jnp.arange(rows)[:, None]
    j = jnp.arange(N_KEYS)[None, :]
    return _rel_bucket(r + WINDOW - j).astype(jnp.int32)


def _lane_patterns(rows):
    lo = (jnp.arange(LANES) < HEAD_DIM).astype(BF16)
    pats = jnp.stack([jnp.zeros_like(lo), lo, 1 - lo])
    return jnp.broadcast_to(pats[:, None, :], (3, rows, LANES))


def _mlp(h2d, g_pre, w_up, w_down, g_post):
    n = h2d.shape[0]
    return pl.pallas_call(
        _mlp_kernel,
        grid=(n // TOKEN_TILE,),
        in_specs=[pl.BlockSpec((TOKEN_TILE, D_MODEL), lambda i: (i, 0)),
                  _full((1, D_MODEL)), _full((D_MODEL, D_FF)), _full((D_FF, D_MODEL)),
                  _full((1, D_MODEL))],
        out_specs=pl.BlockSpec((TOKEN_TILE, D_MODEL), lambda i: (i, 0)),
        out_shape=jax.ShapeDtypeStruct((n, D_MODEL), F32),
        compiler_params=pltpu.CompilerParams(dimension_semantics=("arbitrary",),
                                             vmem_limit_bytes=VMEM_LIMIT_BYTES),
        name="mlp",
    )(h2d, g_pre, w_up, w_down, g_post)


def kernel(x_prompt, x_sample, cache_k_win, cache_v_win, state_pool, norm_pre_mix, norm_post_mix,
           norm_pre_mlp, norm_post_mlp, w_in, b_in, attn_sinks, rel_bias_table, w_attn_up,
           w_pool_grp, pool_scale, w_pool_up, w_out, w_mlp_up, w_mlp_down):
    B, S, _ = x_prompt.shape
    DB, T, _ = x_sample.shape
    depth = w_in.shape[0]
    assert depth == 1 and S % TOKEN_TILE == 0 and (DB * T) % TOKEN_TILE == 0
    assert T == DEC_SEQ and (DB * T) % SAMPLE_TILE == 0
    assert GATE_CHUNKS == TOKEN_TILE // WINDOW

    l = 0
    w_in_b = w_in[l].astype(BF16)
    w_grp_b = jax.scipy.linalg.block_diag(*[w_pool_grp[l, g] for g in range(len(POOL_WINDOWS))]
                                          ).astype(BF16)
    w_attn_up_b = w_attn_up[l].astype(BF16)
    w_pool_up_b = w_pool_up[l].astype(BF16)
    w_out_b = w_out[l].astype(BF16)
    w_mlp_up_b = w_mlp_up[l].astype(BF16)
    w_mlp_down_b = w_mlp_down[l].astype(BF16)
    b_in2 = b_in[l][None, :]
    w_gate_b = w_in_b[:, OFF_GA:].reshape(D_MODEL, GATE_CHUNKS, GATE_CHUNK).transpose(1, 0, 2)
    b_gate = b_in[l][OFF_GA:].reshape(GATE_CHUNKS, 1, GATE_CHUNK)
    sinks2 = attn_sinks[l][None, :]
    scale2 = pool_scale[l][None, :]
    g_pre, g_post = norm_pre_mix[l][None, :], norm_post_mix[l][None, :]
    g_pre_mlp, g_post_mlp = norm_pre_mlp[l][None, :], norm_post_mlp[l][None, :]

    weight_specs = [_full((POOL_WIDTH, POOL_WIDTH)), _full((1, POOL_WIDTH)),
                    _full((ATTN_WIDTH, D_MODEL)), _full((POOL_WIDTH, D_MODEL)),
                    _full((D_MODEL, D_MODEL)), _full((1, D_MODEL))]
    weights = (w_grp_b, scale2, w_attn_up_b, w_pool_up_b, w_out_b, g_post)

    tm = TOKEN_TILE
    h_p, kwin, vwin, pstate = pl.pallas_call(
        _mix_prompt_kernel,
        grid=(B, S // tm),
        in_specs=[pl.BlockSpec((None, tm, D_MODEL), lambda b, t: (b, t, 0)),
                  _full((1, D_MODEL)), _full((D_MODEL, OFF_GA)), _full((1, OFF_GA)),
                  _full((GATE_CHUNKS, D_MODEL, GATE_CHUNK)), _full((GATE_CHUNKS, 1, GATE_CHUNK)),
                  _smem(), _smem(), _full((WINDOW, N_KEYS)), _full((3, tm, LANES))] + weight_specs,
        out_specs=[pl.BlockSpec((None, tm, D_MODEL), lambda b, t: (b, t, 0)),
                   pl.BlockSpec((None, WINDOW, KV_WIDTH), lambda b, t: (b, 0, 0)),
                   pl.BlockSpec((None, WINDOW, KV_WIDTH), lambda b, t: (b, 0, 0)),
                   pl.BlockSpec((None, POOL_PAD, POOL_WIDTH), lambda b, t: (b, 0, 0))],
        out_shape=[jax.ShapeDtypeStruct((B, S, D_MODEL), F32),
                   jax.ShapeDtypeStruct((B, WINDOW, KV_WIDTH), F32),
                   jax.ShapeDtypeStruct((B, WINDOW, KV_WIDTH), F32),
                   jax.ShapeDtypeStruct((B, POOL_PAD, POOL_WIDTH), F32)],
        scratch_shapes=[pltpu.VMEM((N_HEADS * WINDOW, N_KEYS), F32),
                        pltpu.VMEM((2, WINDOW, N_KEYS), F32),
                        pltpu.VMEM((N_KV_HEADS, WINDOW + tm, LANES), BF16),
                        pltpu.VMEM((N_KV_HEADS, WINDOW + tm, LANES), BF16),
                        pltpu.VMEM((N_KV_HEADS, WINDOW + tm, 2 * LANES), BF16),
                        pltpu.VMEM((N_KV_HEADS, WINDOW + tm, 2 * LANES), BF16),
                        pltpu.VMEM((POOL_PAD + tm, POOL_WIDTH), F32),
                        pltpu.VMEM((tm, ATTN_WIDTH), BF16),
                        pltpu.VMEM((tm, ATTN_WIDTH), BF16),
                        pltpu.VMEM((GATE_CHUNKS, tm, GATE_CHUNK), F32),
                        pltpu.VMEM((tm, D_MODEL), BF16)],
        compiler_params=pltpu.CompilerParams(dimension_semantics=("arbitrary", "arbitrary"),
                                             vmem_limit_bytes=VMEM_LIMIT_BYTES),
        name="mix_prompt",
    )(x_prompt, g_pre, w_in_b[:, :OFF_GA], b_in2[:, :OFF_GA], w_gate_b, b_gate, rel_bias_table,
      sinks2, _bucket_matrix(WINDOW), _lane_patterns(tm), *weights)
    y_p = _mlp(h_p.reshape(B * S, D_MODEL), g_pre_mlp, w_mlp_up_b, w_mlp_down_b, g_post_mlp)

    tm, ns = SAMPLE_TILE, SEQ_PER_TILE
    n_tok = DB * T
    state16 = jnp.pad(state_pool[l], ((0, 0), (POOL_PAD - POOL_STATE, 0), (0, 0)))
    ck = cache_k_win[l].reshape(DB, WINDOW, KV_WIDTH)
    cv = cache_v_win[l].reshape(DB, WINDOW, KV_WIDTH)
    h_s, k_s, v_s, up_new = pl.pallas_call(
        _mix_sample_kernel,
        grid=(n_tok // tm,),
        in_specs=[pl.BlockSpec((tm, D_MODEL), lambda i: (i, 0)),
                  pl.BlockSpec((ns, WINDOW, KV_WIDTH), lambda i: (i, 0, 0)),
                  pl.BlockSpec((ns, WINDOW, KV_WIDTH), lambda i: (i, 0, 0)),
                  pl.BlockSpec((ns, POOL_PAD, POOL_WIDTH), lambda i: (i, 0, 0)),
                  _full((1, D_MODEL)), _full((D_MODEL, IN_WIDTH)), _full((1, IN_WIDTH)),
                  _smem(), _smem(), _full((T, N_KEYS))] + weight_specs,
        out_specs=[pl.BlockSpec((tm, D_MODEL), lambda i: (i, 0)),
                   pl.BlockSpec((ns, WINDOW, KV_WIDTH), lambda i: (i, 0, 0)),
                   pl.BlockSpec((ns, WINDOW, KV_WIDTH), lambda i: (i, 0, 0)),
                   pl.BlockSpec((tm, POOL_WIDTH), lambda i: (i, 0))],
        out_shape=[jax.ShapeDtypeStruct((n_tok, D_MODEL), F32),
                   jax.ShapeDtypeStruct((DB, WINDOW, KV_WIDTH), F32),
                   jax.ShapeDtypeStruct((DB, WINDOW, KV_WIDTH), F32),
                   jax.ShapeDtypeStruct((n_tok, POOL_WIDTH), F32)],
        scratch_shapes=[pltpu.VMEM((SEQ_GROUP * N_HEADS * T, N_KEYS), F32),
                        pltpu.VMEM((SEQ_GROUP * N_HEADS * T, LANES), F32),
                        pltpu.VMEM((2, T, N_KEYS), F32),
                        pltpu.VMEM((SEQ_GROUP * N_HEADS * T, N_KEYS), F32),
                        pltpu.VMEM((ns, N_KEYS, LANES), BF16),
                        pltpu.VMEM((ns, N_KEYS, 2 * LANES), BF16),
                        pltpu.VMEM((ns, POOL_PAD + T, POOL_WIDTH), F32),
                        pltpu.VMEM((ns, N_HEADS * T, LANES), F32),
                        pltpu.VMEM((ns, N_HEADS * T, LANES), F32)],
        compiler_params=pltpu.CompilerParams(dimension_semantics=("arbitrary",),
                                             vmem_limit_bytes=VMEM_LIMIT_BYTES),
        name="mix_sample",
    )(x_sample.reshape(n_tok, D_MODEL), ck, cv, state16, g_pre, w_in_b, b_in2, rel_bias_table,
      sinks2, _bucket_matrix(T), *weights)
    y_s = _mlp(h_s, g_pre_mlp, w_mlp_up_b, w_mlp_down_b, g_post_mlp)

    kv_shape = (1, DB, WINDOW, N_KV_HEADS, HEAD_DIM)
    k_s = k_s.reshape(kv_shape)
    v_s = v_s.reshape(kv_shape)
    p_s = jnp.concatenate([state_pool[l][:, T:], up_new.reshape(DB, T, POOL_WIDTH)], axis=1)[None]

    return (y_p.reshape(B, S, D_MODEL), y_s.reshape(DB, T, D_MODEL),
            kwin.reshape(1, B, WINDOW, N_KV_HEADS, HEAD_DIM),
            vwin.reshape(1, B, WINDOW, N_KV_HEADS, HEAD_DIM),
            pstate[:, POOL_PAD - POOL_STATE:][None],
            k_s, v_s, p_s)
```

```python
import functools
import math

import jax
import jax.numpy as jnp
from jax import lax
from jax.experimental import pallas as pl
from jax.experimental.pallas import tpu as pltpu

D_MODEL = 1024
N_HEADS = 8
HEAD_DIM = 64
N_KV_HEADS = 2
GROUP = N_HEADS // N_KV_HEADS
ATTN_WIDTH = N_HEADS * HEAD_DIM
KV_WIDTH = N_KV_HEADS * HEAD_DIM
WINDOW = 128
POOL_WIDTH = D_MODEL // 2
POOL_WINDOWS = (2, 4, 8, 16)
POOL_GROUP_WIDTH = POOL_WIDTH // len(POOL_WINDOWS)
POOL_STATE = max(POOL_WINDOWS) - 1
POOL_PAD = POOL_STATE + 1
D_FF = 4 * D_MODEL
N_BUCKETS = 32
MAX_DISTANCE = 128
RMS_EPS = 1e-6
NEG_INF = -1e30
LOG2E = math.log2(math.e)
PAST_LEN = 16384
IN_WIDTH = ATTN_WIDTH + 2 * KV_WIDTH + POOL_WIDTH + 2 * D_MODEL
OFF_K = ATTN_WIDTH
OFF_V = OFF_K + KV_WIDTH
OFF_POOL = OFF_V + KV_WIDTH
OFF_GA = OFF_POOL + POOL_WIDTH
OFF_GP = OFF_GA + D_MODEL
N_KEYS = 2 * WINDOW
GATE_CHUNKS = 4
GATE_CHUNK = (IN_WIDTH - OFF_GA) // GATE_CHUNKS

LANES = 128
SUBLANES = 8
VMEM_LIMIT_BYTES = 56 * 1024 * 1024

TOKEN_TILE = 512
DEC_SEQ = 8
SAMPLE_TILE = 256
SEQ_PER_TILE = SAMPLE_TILE // DEC_SEQ
SEQ_GROUP = 8

BF16 = jnp.bfloat16
F32 = jnp.float32


def _rms(x, g):
    return x * lax.rsqrt(jnp.mean(x * x, axis=-1, keepdims=True) + RMS_EPS) * g


def _sigmoid(x):
    return 0.5 * jnp.tanh(0.5 * x) + 0.5


def _dot(a, b):
    return jnp.dot(a, b, preferred_element_type=F32)


def _dot_nt(a, b):
    return lax.dot_general(a, b, (((1,), (1,)), ((), ())), preferred_element_type=F32)


def _build_tables(rows, scale, bucket_ref, table_ref, sinks_ref, bias_ref, sink_ref, valid_ref):
    bucket = bucket_ref[...]
    for h in range(N_HEADS):
        acc = jnp.zeros(bucket.shape, F32)
        for b in range(N_BUCKETS):
            acc = jnp.where(bucket == b, table_ref[b, h] * scale, acc)
        bias_ref[h * rows:(h + 1) * rows, :] = acc
        if sink_ref is not None:
            sink_ref[h * rows:(h + 1) * rows, :] = jnp.full((rows, LANES), sinks_ref[0, h] * scale, F32)
    r = lax.broadcasted_iota(jnp.int32, (rows, N_KEYS), 0)
    j = lax.broadcasted_iota(jnp.int32, (rows, N_KEYS), 1)
    cur = jnp.where(j >= WINDOW, jnp.where(j - WINDOW <= r, 1.0, 0.0), 0.0)
    prev = jnp.where(j < WINDOW, jnp.where(j > r, 1.0, 0.0), 0.0)
    valid_ref[0] = cur
    valid_ref[1] = cur + prev


def _head_queries(q):
    lane = lax.broadcasted_iota(jnp.int32, (q.shape[0], LANES), 1)
    lo = lane < HEAD_DIM
    out = []
    for c in range(ATTN_WIDTH // LANES):
        qc = q[:, c * LANES:(c + 1) * LANES]
        qr = pltpu.roll(qc, HEAD_DIM, axis=1)
        g = (2 * c) // GROUP
        if g == 0:
            out.append(jnp.where(lo, qc, 0.0))
            out.append(jnp.where(lo, qr, 0.0))
        else:
            out.append(jnp.where(lo, 0.0, qr))
            out.append(jnp.where(lo, 0.0, qc))
    return out


def _merge_heads(o_heads):
    lane = lax.broadcasted_iota(jnp.int32, o_heads[0].shape, 1)
    lo = lane < HEAD_DIM
    chunks = []
    for c in range(ATTN_WIDTH // LANES):
        e, o = o_heads[2 * c], o_heads[2 * c + 1]
        g = (2 * c) // GROUP
        if g == 0:
            chunks.append(jnp.where(lo, e, pltpu.roll(o, HEAD_DIM, axis=1)))
        else:
            chunks.append(jnp.where(lo, pltpu.roll(e, HEAD_DIM, axis=1), o))
    return jnp.concatenate(chunks, axis=1)


def _pool_means(ext, first_pos):
    del first_pos
    parts = []
    for gi, w in enumerate(POOL_WINDOWS):
        acc = ext[:, gi * POOL_GROUP_WIDTH:(gi + 1) * POOL_GROUP_WIDTH]
        shift = 1
        while shift < w:
            acc = acc + pltpu.roll(acc, shift, axis=0)
            shift *= 2
        parts.append(acc)
    return jnp.concatenate(parts, axis=1)


def _pool_counts(pos):
    lane = lax.broadcasted_iota(jnp.int32, (pos.shape[0], POOL_WIDTH), 1)
    w = jnp.left_shift(2, lane // POOL_GROUP_WIDTH)
    return jnp.minimum(pos + 1, w).astype(F32)


def _mix_tail(x, xn, attn_o, pool_sum, up, cnt, w_in_ref, b_in_ref, w_grp_ref, scale_ref,
              w_attn_up_ref, w_pool_up_ref, w_out_ref, g_post_ref):
    z = (pool_sum / cnt - up).astype(BF16)
    pool_z = (_dot(z, w_grp_ref[...]) * scale_ref[...]).astype(BF16)
    ga = _dot(xn, w_in_ref[:, OFF_GA:OFF_GP]) + b_in_ref[:, OFF_GA:OFF_GP]
    m = _sigmoid(ga) * _dot(attn_o, w_attn_up_ref[...])
    gp = _dot(xn, w_in_ref[:, OFF_GP:IN_WIDTH]) + b_in_ref[:, OFF_GP:IN_WIDTH]
    m = m + _sigmoid(gp) * _dot(pool_z, w_pool_up_ref[...])
    mo = _dot(m.astype(BF16), w_out_ref[...])
    return x + _rms(mo, g_post_ref[...])


def _masked_bias_tables(bucket_ref, table_ref, biasm_ref):
    bucket = bucket_ref[...]
    r = lax.broadcasted_iota(jnp.int32, (WINDOW, N_KEYS), 0)
    j = lax.broadcasted_iota(jnp.int32, (WINDOW, N_KEYS), 1)
    cur = jnp.where(j >= WINDOW, jnp.where(j - WINDOW <= r, 1.0, 0.0), 0.0)
    prev = jnp.where(j < WINDOW, jnp.where(j > r, 1.0, 0.0), 0.0)
    for h in range(N_HEADS):
        acc = jnp.zeros(bucket.shape, F32)
        for b in range(N_BUCKETS):
            acc = jnp.where(bucket == b, table_ref[b, h] * LOG2E, acc)
        biasm_ref[0, h * WINDOW:(h + 1) * WINDOW, :] = jnp.where(cur > 0.5, acc, -jnp.inf)
        biasm_ref[1, h * WINDOW:(h + 1) * WINDOW, :] = jnp.where(cur + prev > 0.5, acc, -jnp.inf)


def _scores_block(i, base, q_ref, ke_ref, ko_ref, s_ref):
    rows = pl.ds(base, WINDOW)
    keys = pl.ds(base, N_KEYS)
    for g in range(N_KV_HEADS):
        qg = jnp.concatenate([q_ref[rows, (2 * g) * LANES:(2 * g + 1) * LANES],
                              q_ref[rows, (2 * g + 1) * LANES:(2 * g + 2) * LANES]], axis=0)
        s_ref[i, g, 0] = _dot_nt(qg, ke_ref[g, keys, :])
        s_ref[i, g, 1] = _dot_nt(qg, ko_ref[g, keys, :])


def _softmax_block(i, has_prev, s_ref, biasm_ref, sinks_ref, p_ref, es_ref):
    lo = lax.broadcasted_iota(jnp.int32, (WINDOW, LANES), 1) < HEAD_DIM
    for g in range(N_KV_HEADS):
        for a in range(2):
            half = slice(a * WINDOW, (a + 1) * WINDOW)
            es = []
            for b in range(2):
                h = GROUP * g + 2 * a + b
                s = s_ref[i, g, b, half, :] + biasm_ref[has_prev, h * WINDOW:(h + 1) * WINDOW, :]
                sink = sinks_ref[0, h] * LOG2E
                m = jnp.maximum(jnp.max(s, axis=-1, keepdims=True), sink)
                p_ref[i, g, b, half, :] = jnp.exp2(s - m).astype(BF16)
                es.append(jnp.exp2(sink - m))
            es_ref[i, 2 * g + a] = jnp.where(lo, es[0], es[1])


def _values_block(i, base, p_ref, es_ref, ve_ref, vo_ref, ao_ref):
    rows = pl.ds(base, WINDOW)
    keys = pl.ds(base, N_KEYS)
    for g in range(N_KV_HEADS):
        o_ext = _dot(p_ref[i, g, 0], ve_ref[g, keys, :]) + _dot(p_ref[i, g, 1], vo_ref[g, keys, :])
        for a in range(2):
            c = 2 * g + a
            o = o_ext[a * WINDOW:(a + 1) * WINDOW]
            ao_ref[rows, c * LANES:(c + 1) * LANES] = (
                o[:, :LANES] / (o[:, LANES:] + es_ref[i, c])).astype(BF16)


def _mix_prompt_kernel(x_ref, g_pre_ref, w_in_ref, b_in_ref, w_gate_ref, b_gate_ref, table_ref,
                       sinks_ref, bucket_ref, pat_ref,
                       w_grp_ref, scale_ref, w_attn_up_ref, w_pool_up_ref, w_out_ref, g_post_ref,
                       h_ref, kwin_ref, vwin_ref, pstate_ref,
                       biasm_ref, ke_ref, ko_ref, ve_ref, vo_ref, ext_ref, q_ref, ao_ref,
                       sg_ref, xn_ref, s_ref, p_ref, es_ref):
    tm = TOKEN_TILE
    t = pl.program_id(1)
    lane = lax.broadcasted_iota(jnp.int32, (tm, LANES), 1)
    lo = lane < HEAD_DIM

    ones_lo, ones_hi = pat_ref[1], pat_ref[2]

    @pl.when((pl.program_id(0) == 0) & (t == 0))
    def _():
        _masked_bias_tables(bucket_ref, table_ref, biasm_ref)

    @pl.when(t == 0)
    def _():
        zeros = pat_ref[0, 0:WINDOW, :]
        for g in range(N_KV_HEADS):
            ke_ref[g, 0:WINDOW, :] = zeros
            ko_ref[g, 0:WINDOW, :] = zeros
            ve_ref[g, 0:WINDOW, :] = jnp.concatenate([zeros, pat_ref[1, 0:WINDOW, :]], axis=1)
            vo_ref[g, 0:WINDOW, :] = jnp.concatenate([zeros, pat_ref[2, 0:WINDOW, :]], axis=1)
        ext_ref[0:POOL_PAD, :] = jnp.zeros((POOL_PAD, POOL_WIDTH), F32)

    x = x_ref[...]
    xn = _rms(x, g_pre_ref[...]).astype(BF16)
    xn_ref[...] = xn
    u = _dot(xn, w_in_ref[...]) + b_in_ref[...]
    k = u[:, OFF_K:OFF_V]
    v = u[:, OFF_V:OFF_POOL]
    up = u[:, OFF_POOL:OFF_GA]

    q_ref[...] = (u[:, 0:OFF_K] * (HEAD_DIM ** -0.5 * LOG2E)).astype(BF16)
    kwin_ref[...] = k[tm - WINDOW:, :]
    vwin_ref[...] = v[tm - WINDOW:, :]
    pstate_ref[...] = up[tm - POOL_PAD:, :]
    ext_ref[POOL_PAD:, :] = up
    kr = pltpu.roll(k, HEAD_DIM, axis=1)
    vr = pltpu.roll(v, HEAD_DIM, axis=1)
    ke_ref[0, WINDOW:, :] = jnp.where(lo, k, 0.0).astype(BF16)
    ko_ref[0, WINDOW:, :] = jnp.where(lo, 0.0, kr).astype(BF16)
    ke_ref[1, WINDOW:, :] = jnp.where(lo, kr, 0.0).astype(BF16)
    ko_ref[1, WINDOW:, :] = jnp.where(lo, 0.0, k).astype(BF16)
    ve_ref[0, WINDOW:, :] = jnp.concatenate([jnp.where(lo, v, 0.0).astype(BF16), ones_lo], axis=1)
    vo_ref[0, WINDOW:, :] = jnp.concatenate([jnp.where(lo, 0.0, vr).astype(BF16), ones_hi], axis=1)
    ve_ref[1, WINDOW:, :] = jnp.concatenate([jnp.where(lo, vr, 0.0).astype(BF16), ones_lo], axis=1)
    vo_ref[1, WINDOW:, :] = jnp.concatenate([jnp.where(lo, 0.0, v).astype(BF16), ones_hi], axis=1)

    pool_sum = _pool_means(ext_ref[...], None)[POOL_PAD:, :]
    pos = t * tm + lax.broadcasted_iota(jnp.int32, (tm, 1), 0)
    z = (pool_sum / _pool_counts(pos) - up).astype(BF16)
    pool_z = (_dot(z, w_grp_ref[...]) * scale_ref[...]).astype(BF16)
    pp = _dot(pool_z, w_pool_up_ref[...])

    def pair(ip, carry):
        blocks = [2 * ip + d for d in range(2)]
        bases = [pl.multiple_of(i * WINDOW, WINDOW) for i in blocks]
        for d in range(2):
            _scores_block(d, bases[d], q_ref, ke_ref, ko_ref, s_ref)
        for d in range(2):
            i = blocks[d]
            sg_ref[i] = _sigmoid(_dot(xn_ref[...], w_gate_ref[i]) + b_gate_ref[i])
        for d in range(2):
            has_prev = jnp.where((t == 0) & (blocks[d] == 0), 0, 1)
            _softmax_block(d, has_prev, s_ref, biasm_ref, sinks_ref, p_ref, es_ref)
        for d in range(2):
            _values_block(d, bases[d], p_ref, es_ref, ve_ref, vo_ref, ao_ref)
        return carry

    lax.fori_loop(0, GATE_CHUNKS // 2, pair, 0)

    a = _dot(ao_ref[...], w_attn_up_ref[...])
    half = D_MODEL // 2
    m = jnp.concatenate([sg_ref[0] * a[:, :half] + sg_ref[2] * pp[:, :half],
                         sg_ref[1] * a[:, half:] + sg_ref[3] * pp[:, half:]], axis=1)
    mo = _dot(m.astype(BF16), w_out_ref[...])
    h_ref[...] = x + _rms(mo, g_post_ref[...])

    for g in range(N_KV_HEADS):
        ke_ref[g, 0:WINDOW, :] = ke_ref[g, tm:tm + WINDOW, :]
        ko_ref[g, 0:WINDOW, :] = ko_ref[g, tm:tm + WINDOW, :]
        ve_ref[g, 0:WINDOW, :] = ve_ref[g, tm:tm + WINDOW, :]
        vo_ref[g, 0:WINDOW, :] = vo_ref[g, tm:tm + WINDOW, :]
    ext_ref[0:POOL_PAD, :] = ext_ref[tm:tm + POOL_PAD, :]


def _mix_sample_kernel(x_ref, ck_ref, cv_ref, state_ref, g_pre_ref, w_in_ref, b_in_ref, table_ref,
                       sinks_ref, bucket_ref, w_grp_ref, scale_ref, w_attn_up_ref, w_pool_up_ref,
                       w_out_ref, g_post_ref,
                       h_ref, kout_ref, vout_ref, up_ref,
                       bias_ref, sink_ref, valid_ref, vfull_ref, kx_ref, vx_ref, ext_ref, qh_ref,
                       oh_ref):
    tm = SAMPLE_TILE
    ns = SEQ_PER_TILE
    T = DEC_SEQ
    R = N_HEADS * T

    @pl.when(pl.program_id(0) == 0)
    def _():
        _build_tables(T, LOG2E, bucket_ref, table_ref, sinks_ref, bias_ref, sink_ref, valid_ref)
        for j in range(N_HEADS):
            vfull_ref[j * T:(j + 1) * T, :] = valid_ref[1]
        for j in range(1, SEQ_GROUP):
            bias_ref[j * R:(j + 1) * R, :] = bias_ref[0:R, :]
            sink_ref[j * R:(j + 1) * R, :] = sink_ref[0:R, :]
            vfull_ref[j * R:(j + 1) * R, :] = vfull_ref[0:R, :]
        vx_ref[:, :, LANES:] = jnp.ones((ns, N_KEYS, LANES), BF16)

    x = x_ref[...]
    xn = _rms(x, g_pre_ref[...]).astype(BF16)
    u = _dot(xn, w_in_ref[:, 0:OFF_GA]) + b_in_ref[:, 0:OFF_GA]
    q = u[:, 0:OFF_K] * (HEAD_DIM ** -0.5 * LOG2E)
    k = u[:, OFF_K:OFF_V]
    v = u[:, OFF_V:OFF_POOL]
    up = u[:, OFF_POOL:OFF_GA]
    k3 = k.reshape(ns, T, LANES)
    v3 = v.reshape(ns, T, LANES)
    ck = ck_ref[...]
    cv = cv_ref[...]

    kout_ref[:, 0:WINDOW - T, :] = ck[:, T:, :]
    kout_ref[:, WINDOW - T:, :] = k3
    vout_ref[:, 0:WINDOW - T, :] = cv[:, T:, :]
    vout_ref[:, WINDOW - T:, :] = v3
    up_ref[...] = up

    pad = jnp.zeros((ns, WINDOW - T, LANES), F32)
    kx_ref[:, 0:WINDOW, :] = ck.astype(BF16)
    kx_ref[:, WINDOW:, :] = jnp.concatenate([k3, pad], axis=1).astype(BF16)
    vx_ref[:, 0:WINDOW, 0:LANES] = cv.astype(BF16)
    vx_ref[:, WINDOW:, 0:LANES] = jnp.concatenate([v3, pad], axis=1).astype(BF16)
    ext_ref[:, 0:POOL_PAD, :] = state_ref[...]
    ext_ref[:, POOL_PAD:, :] = up.reshape(ns, T, POOL_WIDTH)
    for h, qh in enumerate(_head_queries(q)):
        qh_ref[:, h * T:(h + 1) * T, :] = qh.reshape(ns, T, LANES)

    def group(gi, carry):
        s0 = gi * SEQ_GROUP
        s = jnp.concatenate([_dot_nt(qh_ref[s0 + j].astype(BF16), kx_ref[s0 + j])
                             for j in range(SEQ_GROUP)], axis=0)
        s = jnp.where(vfull_ref[...] > 0.5, s + bias_ref[...], NEG_INF)
        sink = sink_ref[:, 0:1]
        m = jnp.maximum(jnp.max(s, axis=-1, keepdims=True), sink)
        p = jnp.exp2(s - m).astype(BF16)
        es = jnp.exp2(sink - m)
        for j in range(SEQ_GROUP):
            o = _dot(p[j * R:(j + 1) * R], vx_ref[s0 + j])
            oh_ref[s0 + j] = o[:, :LANES] / (o[:, LANES:] + es[j * R:(j + 1) * R])
        return carry

    lax.fori_loop(0, ns // SEQ_GROUP, group, 0)

    o_heads = [oh_ref[:, h * T:(h + 1) * T, :].reshape(tm, LANES) for h in range(N_HEADS)]
    attn_o = _merge_heads(o_heads).astype(BF16)

    ext = ext_ref[...].reshape(ns * (POOL_PAD + T), POOL_WIDTH)
    pool_sum = _pool_means(ext, None).reshape(ns, POOL_PAD + T, POOL_WIDTH)[:, POOL_PAD:, :]
    pool_sum = pool_sum.reshape(tm, POOL_WIDTH)
    row = lax.broadcasted_iota(jnp.int32, (tm, 1), 0)
    pos = PAST_LEN + (row & (T - 1))
    h_ref[...] = _mix_tail(x, xn, attn_o, pool_sum, up, _pool_counts(pos),
                           w_in_ref, b_in_ref, w_grp_ref, scale_ref,
                           w_attn_up_ref, w_pool_up_ref, w_out_ref, g_post_ref)


def _mlp_kernel(h_ref, g_pre_ref, w_up_ref, w_down_ref, g_post_ref, y_ref):
    h = h_ref[...]
    hn = _rms(h, g_pre_ref[...]).astype(BF16)
    f = jnp.zeros(h.shape, F32)
    chunk = D_FF // 4
    for c in range(D_FF // chunk):
        a = jnp.maximum(_dot(hn, w_up_ref[:, c * chunk:(c + 1) * chunk]), 0.0)
        f = f + _dot((a * a).astype(BF16), w_down_ref[c * chunk:(c + 1) * chunk, :])
    y_ref[...] = h + _rms(f, g_post_ref[...])


def _full(shape):
    return pl.BlockSpec(shape, lambda *_: (0,) * len(shape), pipeline_mode=pl.Buffered(1))


def _smem():
    return pl.BlockSpec(memory_space=pltpu.SMEM)


def _rel_bucket(dist):
    d = jnp.maximum(dist, 0)
    max_exact = N_BUCKETS // 2
    large = max_exact + (jnp.log(jnp.maximum(d, 1).astype(F32) / max_exact)
                         / math.log(MAX_DISTANCE / max_exact)
                         * (N_BUCKETS - max_exact)).astype(jnp.int32)
    large = jnp.minimum(large, N_BUCKETS - 1)
    return jnp.where(d < max_exact, d, large)


def _bucket_matrix(rows):
    r = jnp.arange(rows)[:, None]
    j = jnp.arange(N_KEYS)[None, :]
    return _rel_bucket(r + WINDOW - j).astype(jnp.int32)


def _lane_patterns(rows):
    lo = (jnp.arange(LANES) < HEAD_DIM).astype(BF16)
    pats = jnp.stack([jnp.zeros_like(lo), lo, 1 - lo])
    return jnp.broadcast_to(pats[:, None, :], (3, rows, LANES))


def _mlp(h2d, g_pre, w_up, w_down, g_post):
    n = h2d.shape[0]
    return pl.pallas_call(
        _mlp_kernel,
        grid=(n // TOKEN_TILE,),
        in_specs=[pl.BlockSpec((TOKEN_TILE, D_MODEL), lambda i: (i, 0)),
                  _full((1, D_MODEL)), _full((D_MODEL, D_FF)), _full((D_FF, D_MODEL)),
                  _full((1, D_MODEL))],
        out_specs=pl.BlockSpec((TOKEN_TILE, D_MODEL), lambda i: (i, 0)),
        out_shape=jax.ShapeDtypeStruct((n, D_MODEL), F32),
        compiler_params=pltpu.CompilerParams(dimension_semantics=("arbitrary",),
                                             vmem_limit_bytes=VMEM_LIMIT_BYTES),
        name="mlp",
    )(h2d, g_pre, w_up, w_down, g_post)


def kernel(x_prompt, x_sample, cache_k_win, cache_v_win, state_pool, norm_pre_mix, norm_post_mix,
           norm_pre_mlp, norm_post_mlp, w_in, b_in, attn_sinks, rel_bias_table, w_attn_up,
           w_pool_grp, pool_scale, w_pool_up, w_out, w_mlp_up, w_mlp_down):
    B, S, _ = x_prompt.shape
    DB, T, _ = x_sample.shape
    depth = w_in.shape[0]
    assert depth == 1 and S % TOKEN_TILE == 0 and (DB * T) % TOKEN_TILE == 0
    assert T == DEC_SEQ and (DB * T) % SAMPLE_TILE == 0
    assert GATE_CHUNKS == TOKEN_TILE // WINDOW

    l = 0
    w_in_b = w_in[l].astype(BF16)
    w_grp_b = jax.scipy.linalg.block_diag(*[w_pool_grp[l, g] for g in range(len(POOL_WINDOWS))]
                                          ).astype(BF16)
    w_attn_up_b = w_attn_up[l].astype(BF16)
    w_pool_up_b = w_pool_up[l].astype(BF16)
    w_out_b = w_out[l].astype(BF16)
    w_mlp_up_b = w_mlp_up[l].astype(BF16)
    w_mlp_down_b = w_mlp_down[l].astype(BF16)
    b_in2 = b_in[l][None, :]
    w_gate_b = w_in_b[:, OFF_GA:].reshape(D_MODEL, GATE_CHUNKS, GATE_CHUNK).transpose(1, 0, 2)
    b_gate = b_in[l][OFF_GA:].reshape(GATE_CHUNKS, 1, GATE_CHUNK)
    sinks2 = attn_sinks[l][None, :]
    scale2 = pool_scale[l][None, :]
    g_pre, g_post = norm_pre_mix[l][None, :], norm_post_mix[l][None, :]
    g_pre_mlp, g_post_mlp = norm_pre_mlp[l][None, :], norm_post_mlp[l][None, :]

    weight_specs = [_full((POOL_WIDTH, POOL_WIDTH)), _full((1, POOL_WIDTH)),
                    _full((ATTN_WIDTH, D_MODEL)), _full((POOL_WIDTH, D_MODEL)),
                    _full((D_MODEL, D_MODEL)), _full((1, D_MODEL))]
    weights = (w_grp_b, scale2, w_attn_up_b, w_pool_up_b, w_out_b, g_post)

    tm = TOKEN_TILE
    h_p, kwin, vwin, pstate = pl.pallas_call(
        _mix_prompt_kernel,
        grid=(B, S // tm),
        in_specs=[pl.BlockSpec((None, tm, D_MODEL), lambda b, t: (b, t, 0)),
                  _full((1, D_MODEL)), _full((D_MODEL, OFF_GA)), _full((1, OFF_GA)),
                  _full((GATE_CHUNKS, D_MODEL, GATE_CHUNK)), _full((GATE_CHUNKS, 1, GATE_CHUNK)),
                  _smem(), _smem(), _full((WINDOW, N_KEYS)), _full((3, tm, LANES))] + weight_specs,
        out_specs=[pl.BlockSpec((None, tm, D_MODEL), lambda b, t: (b, t, 0)),
                   pl.BlockSpec((None, WINDOW, KV_WIDTH), lambda b, t: (b, 0, 0)),
                   pl.BlockSpec((None, WINDOW, KV_WIDTH), lambda b, t: (b, 0, 0)),
                   pl.BlockSpec((None, POOL_PAD, POOL_WIDTH), lambda b, t: (b, 0, 0))],
        out_shape=[jax.ShapeDtypeStruct((B, S, D_MODEL), F32),
                   jax.ShapeDtypeStruct((B, WINDOW, KV_WIDTH), F32),
                   jax.ShapeDtypeStruct((B, WINDOW, KV_WIDTH), F32),
                   jax.ShapeDtypeStruct((B, POOL_PAD, POOL_WIDTH), F32)],
        scratch_shapes=[pltpu.VMEM((2, N_HEADS * WINDOW, N_KEYS), F32),
                        pltpu.VMEM((N_KV_HEADS, WINDOW + tm, LANES), BF16),
                        pltpu.VMEM((N_KV_HEADS, WINDOW + tm, LANES), BF16),
                        pltpu.VMEM((N_KV_HEADS, WINDOW + tm, 2 * LANES), BF16),
                        pltpu.VMEM((N_KV_HEADS, WINDOW + tm, 2 * LANES), BF16),
                        pltpu.VMEM((POOL_PAD + tm, POOL_WIDTH), F32),
                        pltpu.VMEM((tm, ATTN_WIDTH), BF16),
                        pltpu.VMEM((tm, ATTN_WIDTH), BF16),
                        pltpu.VMEM((GATE_CHUNKS, tm, GATE_CHUNK), F32),
                        pltpu.VMEM((tm, D_MODEL), BF16),
                        pltpu.VMEM((2, N_KV_HEADS, 2, 2 * WINDOW, N_KEYS), F32),
                        pltpu.VMEM((2, N_KV_HEADS, 2, 2 * WINDOW, N_KEYS), BF16),
                        pltpu.VMEM((2, ATTN_WIDTH // LANES, WINDOW, LANES), F32)],
        compiler_params=pltpu.CompilerParams(dimension_semantics=("arbitrary", "arbitrary"),
                                             vmem_limit_bytes=VMEM_LIMIT_BYTES),
        name="mix_prompt",
    )(x_prompt, g_pre, w_in_b[:, :OFF_GA], b_in2[:, :OFF_GA], w_gate_b, b_gate, rel_bias_table,
      sinks2, _bucket_matrix(WINDOW), _lane_patterns(tm), *weights)
    y_p = _mlp(h_p.reshape(B * S, D_MODEL), g_pre_mlp, w_mlp_up_b, w_mlp_down_b, g_post_mlp)

    tm, ns = SAMPLE_TILE, SEQ_PER_TILE
    n_tok = DB * T
    state16 = jnp.pad(state_pool[l], ((0, 0), (POOL_PAD - POOL_STATE, 0), (0, 0)))
    ck = cache_k_win[l].reshape(DB, WINDOW, KV_WIDTH)
    cv = cache_v_win[l].reshape(DB, WINDOW, KV_WIDTH)
    h_s, k_s, v_s, up_new = pl.pallas_call(
        _mix_sample_kernel,
        grid=(n_tok // tm,),
        in_specs=[pl.BlockSpec((tm, D_MODEL), lambda i: (i, 0)),
                  pl.BlockSpec((ns, WINDOW, KV_WIDTH), lambda i: (i, 0, 0)),
                  pl.BlockSpec((ns, WINDOW, KV_WIDTH), lambda i: (i, 0, 0)),
                  pl.BlockSpec((ns, POOL_PAD, POOL_WIDTH), lambda i: (i, 0, 0)),
                  _full((1, D_MODEL)), _full((D_MODEL, IN_WIDTH)), _full((1, IN_WIDTH)),
                  _smem(), _smem(), _full((T, N_KEYS))] + weight_specs,
        out_specs=[pl.BlockSpec((tm, D_MODEL), lambda i: (i, 0)),
                   pl.BlockSpec((ns, WINDOW, KV_WIDTH), lambda i: (i, 0, 0)),
                   pl.BlockSpec((ns, WINDOW, KV_WIDTH), lambda i: (i, 0, 0)),
                   pl.BlockSpec((tm, POOL_WIDTH), lambda i: (i, 0))],
        out_shape=[jax.ShapeDtypeStruct((n_tok, D_MODEL), F32),
                   jax.ShapeDtypeStruct((DB, WINDOW, KV_WIDTH), F32),
                   jax.ShapeDtypeStruct((DB, WINDOW, KV_WIDTH), F32),
                   jax.ShapeDtypeStruct((n_tok, POOL_WIDTH), F32)],
        scratch_shapes=[pltpu.VMEM((SEQ_GROUP * N_HEADS * T, N_KEYS), F32),
                        pltpu.VMEM((SEQ_GROUP * N_HEADS * T, LANES), F32),
                        pltpu.VMEM((2, T, N_KEYS), F32),
                        pltpu.VMEM((SEQ_GROUP * N_HEADS * T, N_KEYS), F32),
                        pltpu.VMEM((ns, N_KEYS, LANES), BF16),
                        pltpu.VMEM((ns, N_KEYS, 2 * LANES), BF16),
                        pltpu.VMEM((ns, POOL_PAD + T, POOL_WIDTH), F32),
                        pltpu.VMEM((ns, N_HEADS * T, LANES), F32),
                        pltpu.VMEM((ns, N_HEADS * T, LANES), F32)],
        compiler_params=pltpu.CompilerParams(dimension_semantics=("arbitrary",),
                                             vmem_limit_bytes=VMEM_LIMIT_BYTES),
        name="mix_sample",
    )(x_sample.reshape(n_tok, D_MODEL), ck, cv, state16, g_pre, w_in_b, b_in2, rel_bias_table,
      sinks2, _bucket_matrix(T), *weights)
    y_s = _mlp(h_s, g_pre_mlp, w_mlp_up_b, w_mlp_down_b, g_post_mlp)

    kv_shape = (1, DB, WINDOW, N_KV_HEADS, HEAD_DIM)
    k_s = k_s.reshape(kv_shape)
    v_s = v_s.reshape(kv_shape)
    p_s = jnp.concatenate([state_pool[l][:, T:], up_new.reshape(DB, T, POOL_WIDTH)], axis=1)[None]

    return (y_p.reshape(B, S, D_MODEL), y_s.reshape(DB, T, D_MODEL),
            kwin.reshape(1, B, WINDOW, N_KV_HEADS, HEAD_DIM),
            vwin.reshape(1, B, WINDOW, N_KV_HEADS, HEAD_DIM),
            pstate[:, POOL_PAD - POOL_STATE:][None],
            k_s, v_s, p_s)
```

```python
import functools
import math

import jax
import jax.numpy as jnp
import numpy as np
from jax import lax
from jax.experimental import pallas as pl
from jax.experimental.pallas import tpu as pltpu

D_MODEL = 1024
N_HEADS = 8
HEAD_DIM = 64
N_KV_HEADS = 2
GROUP = N_HEADS // N_KV_HEADS
ATTN_WIDTH = N_HEADS * HEAD_DIM
KV_WIDTH = N_KV_HEADS * HEAD_DIM
WINDOW = 128
POOL_WIDTH = D_MODEL // 2
POOL_WINDOWS = (2, 4, 8, 16)
POOL_GROUP_WIDTH = POOL_WIDTH // len(POOL_WINDOWS)
POOL_STATE = max(POOL_WINDOWS) - 1
POOL_PAD = POOL_STATE + 1
D_FF = 4 * D_MODEL
N_BUCKETS = 32
MAX_DISTANCE = 128
RMS_EPS = 1e-6
NEG_INF = -1e30
LOG2E = math.log2(math.e)
PAST_LEN = 16384
IN_WIDTH = ATTN_WIDTH + 2 * KV_WIDTH + POOL_WIDTH + 2 * D_MODEL
OFF_K = ATTN_WIDTH
OFF_V = OFF_K + KV_WIDTH
OFF_POOL = OFF_V + KV_WIDTH
OFF_GA = OFF_POOL + POOL_WIDTH
OFF_GP = OFF_GA + D_MODEL
N_KEYS = 2 * WINDOW
GATE_CHUNKS = 4
GATE_CHUNK = (IN_WIDTH - OFF_GA) // GATE_CHUNKS
BLOCKS_PER_ITER = 4

LANES = 128
SUBLANES = 8
VMEM_LIMIT_BYTES = 56 * 1024 * 1024

TOKEN_TILE = 512
MLP_TILE = 1024
DEC_SEQ = 8
SAMPLE_TILE = 256
SEQ_PER_TILE = SAMPLE_TILE // DEC_SEQ
SEQ_GROUP = 8

BF16 = jnp.bfloat16
F32 = jnp.float32


def _rms(x, g):
    return x * lax.rsqrt(jnp.mean(x * x, axis=-1, keepdims=True) + RMS_EPS) * g


def _sigmoid(x):
    return 0.5 * jnp.tanh(0.5 * x) + 0.5


def _exp2(x):
    return jnp.exp2(x)


def _dot(a, b):
    return jnp.dot(a, b, preferred_element_type=F32)


def _dot_nt(a, b):
    return lax.dot_general(a, b, (((1,), (1,)), ((), ())), preferred_element_type=F32)


def _build_tables(rows, scale, bucket_ref, table_ref, sinks_ref, bias_ref, sink_ref, valid_ref):
    bucket = bucket_ref[...]
    for h in range(N_HEADS):
        acc = jnp.zeros(bucket.shape, F32)
        for b in range(N_BUCKETS):
            acc = jnp.where(bucket == b, table_ref[b, h] * scale, acc)
        bias_ref[h * rows:(h + 1) * rows, :] = acc
        if sink_ref is not None:
            sink_ref[h * rows:(h + 1) * rows, :] = jnp.full((rows, LANES), sinks_ref[0, h] * scale, F32)
    r = lax.broadcasted_iota(jnp.int32, (rows, N_KEYS), 0)
    j = lax.broadcasted_iota(jnp.int32, (rows, N_KEYS), 1)
    cur = jnp.where(j >= WINDOW, jnp.where(j - WINDOW <= r, 1.0, 0.0), 0.0)
    prev = jnp.where(j < WINDOW, jnp.where(j > r, 1.0, 0.0), 0.0)
    valid_ref[0] = cur
    valid_ref[1] = cur + prev


def _head_queries(q):
    lane = lax.broadcasted_iota(jnp.int32, (q.shape[0], LANES), 1)
    lo = lane < HEAD_DIM
    out = []
    for c in range(ATTN_WIDTH // LANES):
        qc = q[:, c * LANES:(c + 1) * LANES]
        qr = pltpu.roll(qc, HEAD_DIM, axis=1)
        g = (2 * c) // GROUP
        if g == 0:
            out.append(jnp.where(lo, qc, 0.0))
            out.append(jnp.where(lo, qr, 0.0))
        else:
            out.append(jnp.where(lo, 0.0, qr))
            out.append(jnp.where(lo, 0.0, qc))
    return out


def _merge_heads(o_heads):
    lane = lax.broadcasted_iota(jnp.int32, o_heads[0].shape, 1)
    lo = lane < HEAD_DIM
    chunks = []
    for c in range(ATTN_WIDTH // LANES):
        e, o = o_heads[2 * c], o_heads[2 * c + 1]
        g = (2 * c) // GROUP
        if g == 0:
            chunks.append(jnp.where(lo, e, pltpu.roll(o, HEAD_DIM, axis=1)))
        else:
            chunks.append(jnp.where(lo, pltpu.roll(e, HEAD_DIM, axis=1), o))
    return jnp.concatenate(chunks, axis=1)


def _pool_means(ext, first_pos):
    del first_pos
    parts = []
    for gi, w in enumerate(POOL_WINDOWS):
        acc = ext[:, gi * POOL_GROUP_WIDTH:(gi + 1) * POOL_GROUP_WIDTH]
        shift = 1
        while shift < w:
            acc = acc + pltpu.roll(acc, shift, axis=0)
            shift *= 2
        parts.append(acc)
    return jnp.concatenate(parts, axis=1)


def _pool_counts(pos):
    lane = lax.broadcasted_iota(jnp.int32, (pos.shape[0], POOL_WIDTH), 1)
    w = jnp.left_shift(2, lane // POOL_GROUP_WIDTH)
    return jnp.minimum(pos + 1, w).astype(F32)


def _mix_tail(x, xn, attn_o, pool_sum, up, cnt, w_in_ref, b_in_ref, w_grp_ref, scale_ref,
              w_attn_up_ref, w_pool_up_ref, w_out_ref, g_post_ref):
    z = (pool_sum / cnt - up).astype(BF16)
    pool_z = (_dot(z, w_grp_ref[...]) * scale_ref[...]).astype(BF16)
    ga = _dot(xn, w_in_ref[:, OFF_GA:OFF_GP]) + b_in_ref[:, OFF_GA:OFF_GP]
    m = _sigmoid(ga) * _dot(attn_o, w_attn_up_ref[...])
    gp = _dot(xn, w_in_ref[:, OFF_GP:IN_WIDTH]) + b_in_ref[:, OFF_GP:IN_WIDTH]
    m = m + _sigmoid(gp) * _dot(pool_z, w_pool_up_ref[...])
    mo = _dot(m.astype(BF16), w_out_ref[...])
    return x + _rms(mo, g_post_ref[...])


def _masked_bias_tables(bucket_ref, table_ref, biasm_ref):
    bucket = bucket_ref[...]
    r = lax.broadcasted_iota(jnp.int32, (WINDOW, N_KEYS), 0)
    j = lax.broadcasted_iota(jnp.int32, (WINDOW, N_KEYS), 1)
    cur = jnp.where(j >= WINDOW, jnp.where(j - WINDOW <= r, 1.0, 0.0), 0.0)
    prev = jnp.where(j < WINDOW, jnp.where(j > r, 1.0, 0.0), 0.0)
    for h in range(N_HEADS):
        acc = jnp.zeros(bucket.shape, F32)
        for b in range(N_BUCKETS):
            acc = jnp.where(bucket == b, table_ref[b, h] * LOG2E, acc)
        biasm_ref[0, h * WINDOW:(h + 1) * WINDOW, :] = jnp.where(cur > 0.5, acc, -jnp.inf)
        biasm_ref[1, h * WINDOW:(h + 1) * WINDOW, :] = jnp.where(cur + prev > 0.5, acc, -jnp.inf)


def _scores_block(i, base, q_ref, ke_ref, ko_ref, s_ref):
    rows = pl.ds(base, WINDOW)
    keys = pl.ds(base, N_KEYS)
    for g in range(N_KV_HEADS):
        qg = jnp.concatenate([q_ref[rows, (2 * g) * LANES:(2 * g + 1) * LANES],
                              q_ref[rows, (2 * g + 1) * LANES:(2 * g + 2) * LANES]], axis=0)
        s_ref[i, g, 0] = _dot_nt(qg, ke_ref[g, keys, :])
        s_ref[i, g, 1] = _dot_nt(qg, ko_ref[g, keys, :])


def _softmax_block(i, has_prev, s_ref, biasm_ref, sinks_ref, p_ref, es_ref):
    lo = lax.broadcasted_iota(jnp.int32, (WINDOW, LANES), 1) < HEAD_DIM
    for g in range(N_KV_HEADS):
        for a in range(2):
            half = slice(a * WINDOW, (a + 1) * WINDOW)
            es = []
            for b in range(2):
                h = GROUP * g + 2 * a + b
                s = s_ref[i, g, b, half, :] + biasm_ref[has_prev, h * WINDOW:(h + 1) * WINDOW, :]
                sink = sinks_ref[0, h] * LOG2E
                m = jnp.maximum(jnp.max(s, axis=-1, keepdims=True), sink)
                p_ref[i, g, b, half, :] = _exp2(s - m).astype(BF16)
                es.append(_exp2(sink - m))
            es_ref[i, 2 * g + a] = jnp.where(lo, es[0], es[1])


def _values_block(i, base, p_ref, es_ref, ve_ref, vo_ref, ao_ref):
    rows = pl.ds(base, WINDOW)
    keys = pl.ds(base, N_KEYS)
    for g in range(N_KV_HEADS):
        o_ext = _dot(p_ref[i, g, 0], ve_ref[g, keys, :]) + _dot(p_ref[i, g, 1], vo_ref[g, keys, :])
        for a in range(2):
            c = 2 * g + a
            o = o_ext[a * WINDOW:(a + 1) * WINDOW]
            ao_ref[rows, c * LANES:(c + 1) * LANES] = (
                o[:, :LANES] / (o[:, LANES:] + es_ref[i, c])).astype(BF16)


def _mix_prompt_kernel(x_ref, g_pre_ref, w_in_ref, b_in_ref, w_gate_ref, b_gate_ref, table_ref,
                       sinks_ref, bucket_ref, pat_ref,
                       w_grp_ref, scale_ref, w_attn_up_ref, w_pool_up_ref, w_out_ref, g_post_ref,
                       h_ref, kwin_ref, vwin_ref, pstate_ref,
                       biasm_ref, ke_ref, ko_ref, ve_ref, vo_ref, ext_ref, q_ref, ao_ref,
                       sg_ref, xn_ref, s_ref, p_ref, es_ref):
    tm = TOKEN_TILE
    t = pl.program_id(1)
    lane = lax.broadcasted_iota(jnp.int32, (tm, LANES), 1)
    lo = lane < HEAD_DIM

    ones_lo, ones_hi = pat_ref[1], pat_ref[2]

    @pl.when((pl.program_id(0) == 0) & (t == 0))
    def _():
        _masked_bias_tables(bucket_ref, table_ref, biasm_ref)

    @pl.when(t == 0)
    def _():
        zeros = pat_ref[0, 0:WINDOW, :]
        for g in range(N_KV_HEADS):
            ke_ref[g, 0:WINDOW, :] = zeros
            ko_ref[g, 0:WINDOW, :] = zeros
            ve_ref[g, 0:WINDOW, :] = jnp.concatenate([zeros, pat_ref[1, 0:WINDOW, :]], axis=1)
            vo_ref[g, 0:WINDOW, :] = jnp.concatenate([zeros, pat_ref[2, 0:WINDOW, :]], axis=1)
        ext_ref[0:POOL_PAD, :] = jnp.zeros((POOL_PAD, POOL_WIDTH), F32)

    x = x_ref[...]
    xn = _rms(x, g_pre_ref[...]).astype(BF16)
    xn_ref[...] = xn
    u = _dot(xn, w_in_ref[...]) + b_in_ref[...]
    k = u[:, OFF_K:OFF_V]
    v = u[:, OFF_V:OFF_POOL]
    up = u[:, OFF_POOL:OFF_GA]

    q_ref[...] = (u[:, 0:OFF_K] * (HEAD_DIM ** -0.5 * LOG2E)).astype(BF16)
    kwin_ref[...] = k[tm - WINDOW:, :]
    vwin_ref[...] = v[tm - WINDOW:, :]
    pstate_ref[...] = up[tm - POOL_PAD:, :]
    ext_ref[POOL_PAD:, :] = up
    kr = pltpu.roll(k, HEAD_DIM, axis=1)
    vr = pltpu.roll(v, HEAD_DIM, axis=1)
    ke_ref[0, WINDOW:, :] = jnp.where(lo, k, 0.0).astype(BF16)
    ko_ref[0, WINDOW:, :] = jnp.where(lo, 0.0, kr).astype(BF16)
    ke_ref[1, WINDOW:, :] = jnp.where(lo, kr, 0.0).astype(BF16)
    ko_ref[1, WINDOW:, :] = jnp.where(lo, 0.0, k).astype(BF16)
    ve_ref[0, WINDOW:, :] = jnp.concatenate([jnp.where(lo, v, 0.0).astype(BF16), ones_lo], axis=1)
    vo_ref[0, WINDOW:, :] = jnp.concatenate([jnp.where(lo, 0.0, vr).astype(BF16), ones_hi], axis=1)
    ve_ref[1, WINDOW:, :] = jnp.concatenate([jnp.where(lo, vr, 0.0).astype(BF16), ones_lo], axis=1)
    vo_ref[1, WINDOW:, :] = jnp.concatenate([jnp.where(lo, 0.0, v).astype(BF16), ones_hi], axis=1)

    nb = BLOCKS_PER_ITER
    assert nb == GATE_CHUNKS

    def finish(_, carry):
        bases = [d * WINDOW for d in range(nb)]
        for d in range(nb):
            _scores_block(d, bases[d], q_ref, ke_ref, ko_ref, s_ref)
        for d in range(nb):
            sg_ref[d] = _sigmoid(_dot(xn_ref[...], w_gate_ref[d]) + b_gate_ref[d])
        for d in range(nb):
            has_prev = jnp.where(t == 0, 0, 1) if d == 0 else 1
            _softmax_block(d, has_prev, s_ref, biasm_ref, sinks_ref, p_ref, es_ref)
        ext = ext_ref[...]
        pos = t * tm + lax.broadcasted_iota(jnp.int32, (tm, 1), 0)
        z = (_pool_means(ext, None)[POOL_PAD:, :] / _pool_counts(pos) - ext[POOL_PAD:, :]).astype(BF16)
        pool_z = (_dot(z, w_grp_ref[...]) * scale_ref[...]).astype(BF16)
        pp = _dot(pool_z, w_pool_up_ref[...])
        for d in range(nb):
            _values_block(d, bases[d], p_ref, es_ref, ve_ref, vo_ref, ao_ref)
        a = _dot(ao_ref[...], w_attn_up_ref[...])
        half = D_MODEL // 2
        m = jnp.concatenate([sg_ref[0] * a[:, :half] + sg_ref[2] * pp[:, :half],
                             sg_ref[1] * a[:, half:] + sg_ref[3] * pp[:, half:]], axis=1)
        mo = _dot(m.astype(BF16), w_out_ref[...])
        h_ref[...] = x_ref[...] + _rms(mo, g_post_ref[...])
        return carry

    lax.fori_loop(0, 1 + jnp.minimum(t, 0), finish, 0)

    for g in range(N_KV_HEADS):
        ke_ref[g, 0:WINDOW, :] = ke_ref[g, tm:tm + WINDOW, :]
        ko_ref[g, 0:WINDOW, :] = ko_ref[g, tm:tm + WINDOW, :]
        ve_ref[g, 0:WINDOW, :] = ve_ref[g, tm:tm + WINDOW, :]
        vo_ref[g, 0:WINDOW, :] = vo_ref[g, tm:tm + WINDOW, :]
    ext_ref[0:POOL_PAD, :] = ext_ref[tm:tm + POOL_PAD, :]


def _mix_sample_kernel(x_ref, ck_ref, cv_ref, state_ref, g_pre_ref, w_in_ref, b_in_ref, table_ref,
                       sinks_ref, bucket_ref, w_grp_ref, scale_ref, w_attn_up_ref, w_pool_up_ref,
                       w_out_ref, g_post_ref,
                       h_ref, kout_ref, vout_ref, up_ref,
                       bias_ref, sink_ref, valid_ref, vfull_ref, kx_ref, vx_ref, ext_ref, qh_ref,
                       oh_ref):
    tm = SAMPLE_TILE
    ns = SEQ_PER_TILE
    T = DEC_SEQ
    R = N_HEADS * T

    @pl.when(pl.program_id(0) == 0)
    def _():
        _build_tables(T, LOG2E, bucket_ref, table_ref, sinks_ref, bias_ref, sink_ref, valid_ref)
        for j in range(N_HEADS):
            vfull_ref[j * T:(j + 1) * T, :] = valid_ref[1]
        for j in range(1, SEQ_GROUP):
            bias_ref[j * R:(j + 1) * R, :] = bias_ref[0:R, :]
            sink_ref[j * R:(j + 1) * R, :] = sink_ref[0:R, :]
            vfull_ref[j * R:(j + 1) * R, :] = vfull_ref[0:R, :]
        vx_ref[:, :, LANES:] = jnp.ones((ns, N_KEYS, LANES), BF16)

    x = x_ref[...]
    xn = _rms(x, g_pre_ref[...]).astype(BF16)
    u = _dot(xn, w_in_ref[:, 0:OFF_GA]) + b_in_ref[:, 0:OFF_GA]
    q = u[:, 0:OFF_K] * (HEAD_DIM ** -0.5 * LOG2E)
    k = u[:, OFF_K:OFF_V]
    v = u[:, OFF_V:OFF_POOL]
    up = u[:, OFF_POOL:OFF_GA]
    k3 = k.reshape(ns, T, LANES)
    v3 = v.reshape(ns, T, LANES)
    ck = ck_ref[...]
    cv = cv_ref[...]

    kout_ref[:, 0:WINDOW - T, :] = ck[:, T:, :]
    kout_ref[:, WINDOW - T:, :] = k3
    vout_ref[:, 0:WINDOW - T, :] = cv[:, T:, :]
    vout_ref[:, WINDOW - T:, :] = v3
    up_ref[...] = up

    pad = jnp.zeros((ns, WINDOW - T, LANES), F32)
    kx_ref[:, 0:WINDOW, :] = ck.astype(BF16)
    kx_ref[:, WINDOW:, :] = jnp.concatenate([k3, pad], axis=1).astype(BF16)
    vx_ref[:, 0:WINDOW, 0:LANES] = cv.astype(BF16)
    vx_ref[:, WINDOW:, 0:LANES] = jnp.concatenate([v3, pad], axis=1).astype(BF16)
    ext_ref[:, 0:POOL_PAD, :] = state_ref[...]
    ext_ref[:, POOL_PAD:, :] = up.reshape(ns, T, POOL_WIDTH)
    for h, qh in enumerate(_head_queries(q)):
        qh_ref[:, h * T:(h + 1) * T, :] = qh.reshape(ns, T, LANES)

    def group(gi, carry):
        s0 = gi * SEQ_GROUP
        s = jnp.concatenate([_dot_nt(qh_ref[s0 + j].astype(BF16), kx_ref[s0 + j])
                             for j in range(SEQ_GROUP)], axis=0)
        s = jnp.where(vfull_ref[...] > 0.5, s + bias_ref[...], NEG_INF)
        sink = sink_ref[:, 0:1]
        m = jnp.maximum(jnp.max(s, axis=-1, keepdims=True), sink)
        p = _exp2(s - m).astype(BF16)
        es = _exp2(sink - m)
        for j in range(SEQ_GROUP):
            o = _dot(p[j * R:(j + 1) * R], vx_ref[s0 + j])
            oh_ref[s0 + j] = o[:, :LANES] / (o[:, LANES:] + es[j * R:(j + 1) * R])
        return carry

    lax.fori_loop(0, ns // SEQ_GROUP, group, 0)

    o_heads = [oh_ref[:, h * T:(h + 1) * T, :].reshape(tm, LANES) for h in range(N_HEADS)]
    attn_o = _merge_heads(o_heads).astype(BF16)

    ext = ext_ref[...].reshape(ns * (POOL_PAD + T), POOL_WIDTH)
    pool_sum = _pool_means(ext, None).reshape(ns, POOL_PAD + T, POOL_WIDTH)[:, POOL_PAD:, :]
    pool_sum = pool_sum.reshape(tm, POOL_WIDTH)
    row = lax.broadcasted_iota(jnp.int32, (tm, 1), 0)
    pos = PAST_LEN + (row & (T - 1))
    h_ref[...] = _mix_tail(x, xn, attn_o, pool_sum, up, _pool_counts(pos),
                           w_in_ref, b_in_ref, w_grp_ref, scale_ref,
                           w_attn_up_ref, w_pool_up_ref, w_out_ref, g_post_ref)


def _mlp_kernel(h_ref, g_pre_ref, w_up_ref, w_down_ref, g_post_ref, y_ref):
    h = h_ref[...]
    hn = _rms(h, g_pre_ref[...]).astype(BF16)
    f = jnp.zeros(h.shape, F32)
    chunk = D_FF // 4
    for c in range(D_FF // chunk):
        a = jnp.maximum(_dot(hn, w_up_ref[:, c * chunk:(c + 1) * chunk]), 0.0)
        f = f + _dot((a * a).astype(BF16), w_down_ref[c * chunk:(c + 1) * chunk, :])
    y_ref[...] = h + _rms(f, g_post_ref[...])


def _full(shape):
    return pl.BlockSpec(shape, lambda *_: (0,) * len(shape), pipeline_mode=pl.Buffered(1))


def _smem():
    return pl.BlockSpec(memory_space=pltpu.SMEM)


def _rel_bucket(dist):
    d = np.maximum(dist, 0)
    max_exact = N_BUCKETS // 2
    large = max_exact + (np.log(np.maximum(d, 1).astype(np.float32) / np.float32(max_exact))
                         / np.float32(math.log(MAX_DISTANCE / max_exact))
                         * np.float32(N_BUCKETS - max_exact)).astype(np.int32)
    large = np.minimum(large, N_BUCKETS - 1)
    return np.where(d < max_exact, d, large)


def _bucket_matrix(rows):
    r = np.arange(rows)[:, None]
    j = np.arange(N_KEYS)[None, :]
    return _rel_bucket(r + WINDOW - j).astype(np.int32)


def _lane_patterns(rows):
    lo = (jnp.arange(LANES) < HEAD_DIM).astype(BF16)
    pats = jnp.stack([jnp.zeros_like(lo), lo, 1 - lo])
    return jnp.broadcast_to(pats[:, None, :], (3, rows, LANES))


def _mlp(h2d, g_pre, w_up, w_down, g_post):
    n = h2d.shape[0]
    tile = MLP_TILE
    return pl.pallas_call(
        _mlp_kernel,
        grid=(n // tile,),
        in_specs=[pl.BlockSpec((tile, D_MODEL), lambda i: (i, 0)),
                  _full((1, D_MODEL)), _full((D_MODEL, D_FF)), _full((D_FF, D_MODEL)),
                  _full((1, D_MODEL))],
        out_specs=pl.BlockSpec((tile, D_MODEL), lambda i: (i, 0)),
        out_shape=jax.ShapeDtypeStruct((n, D_MODEL), F32),
        compiler_params=pltpu.CompilerParams(dimension_semantics=("arbitrary",),
                                             vmem_limit_bytes=VMEM_LIMIT_BYTES),
        name="mlp",
    )(h2d, g_pre, w_up, w_down, g_post)


def kernel(x_prompt, x_sample, cache_k_win, cache_v_win, state_pool, norm_pre_mix, norm_post_mix,
           norm_pre_mlp, norm_post_mlp, w_in, b_in, attn_sinks, rel_bias_table, w_attn_up,
           w_pool_grp, pool_scale, w_pool_up, w_out, w_mlp_up, w_mlp_down):
    B, S, _ = x_prompt.shape
    DB, T, _ = x_sample.shape
    depth = w_in.shape[0]
    assert depth == 1 and S % TOKEN_TILE == 0 and (DB * T) % TOKEN_TILE == 0
    assert T == DEC_SEQ and (DB * T) % SAMPLE_TILE == 0
    assert GATE_CHUNKS == TOKEN_TILE // WINDOW

    l = 0
    w_in_b = w_in[l].astype(BF16)
    w_grp_b = jax.scipy.linalg.block_diag(*[w_pool_grp[l, g] for g in range(len(POOL_WINDOWS))]
                                          ).astype(BF16)
    w_attn_up_b = w_attn_up[l].astype(BF16)
    w_pool_up_b = w_pool_up[l].astype(BF16)
    w_out_b = w_out[l].astype(BF16)
    w_mlp_up_b = w_mlp_up[l].astype(BF16)
    w_mlp_down_b = w_mlp_down[l].astype(BF16)
    b_in2 = b_in[l][None, :]
    w_gate_b = w_in_b[:, OFF_GA:].reshape(D_MODEL, GATE_CHUNKS, GATE_CHUNK).transpose(1, 0, 2)
    b_gate = b_in[l][OFF_GA:].reshape(GATE_CHUNKS, 1, GATE_CHUNK)
    sinks2 = attn_sinks[l][None, :]
    scale2 = pool_scale[l][None, :]
    g_pre, g_post = norm_pre_mix[l][None, :], norm_post_mix[l][None, :]
    g_pre_mlp, g_post_mlp = norm_pre_mlp[l][None, :], norm_post_mlp[l][None, :]

    weight_specs = [_full((POOL_WIDTH, POOL_WIDTH)), _full((1, POOL_WIDTH)),
                    _full((ATTN_WIDTH, D_MODEL)), _full((POOL_WIDTH, D_MODEL)),
                    _full((D_MODEL, D_MODEL)), _full((1, D_MODEL))]
    weights = (w_grp_b, scale2, w_attn_up_b, w_pool_up_b, w_out_b, g_post)

    tm = TOKEN_TILE
    h_p, kwin, vwin, pstate = pl.pallas_call(
        _mix_prompt_kernel,
        grid=(B, S // tm),
        in_specs=[pl.BlockSpec((None, tm, D_MODEL), lambda b, t: (b, t, 0)),
                  _full((1, D_MODEL)), _full((D_MODEL, OFF_GA)), _full((1, OFF_GA)),
                  _full((GATE_CHUNKS, D_MODEL, GATE_CHUNK)), _full((GATE_CHUNKS, 1, GATE_CHUNK)),
                  _smem(), _smem(), _full((WINDOW, N_KEYS)), _full((3, tm, LANES))] + weight_specs,
        out_specs=[pl.BlockSpec((None, tm, D_MODEL), lambda b, t: (b, t, 0)),
                   pl.BlockSpec((None, WINDOW, KV_WIDTH), lambda b, t: (b, 0, 0)),
                   pl.BlockSpec((None, WINDOW, KV_WIDTH), lambda b, t: (b, 0, 0)),
                   pl.BlockSpec((None, POOL_PAD, POOL_WIDTH), lambda b, t: (b, 0, 0))],
        out_shape=[jax.ShapeDtypeStruct((B, S, D_MODEL), F32),
                   jax.ShapeDtypeStruct((B, WINDOW, KV_WIDTH), F32),
                   jax.ShapeDtypeStruct((B, WINDOW, KV_WIDTH), F32),
                   jax.ShapeDtypeStruct((B, POOL_PAD, POOL_WIDTH), F32)],
        scratch_shapes=[pltpu.VMEM((2, N_HEADS * WINDOW, N_KEYS), F32),
                        pltpu.VMEM((N_KV_HEADS, WINDOW + tm, LANES), BF16),
                        pltpu.VMEM((N_KV_HEADS, WINDOW + tm, LANES), BF16),
                        pltpu.VMEM((N_KV_HEADS, WINDOW + tm, 2 * LANES), BF16),
                        pltpu.VMEM((N_KV_HEADS, WINDOW + tm, 2 * LANES), BF16),
                        pltpu.VMEM((POOL_PAD + tm, POOL_WIDTH), F32),
                        pltpu.VMEM((tm, ATTN_WIDTH), BF16),
                        pltpu.VMEM((tm, ATTN_WIDTH), BF16),
                        pltpu.VMEM((GATE_CHUNKS, tm, GATE_CHUNK), F32),
                        pltpu.VMEM((tm, D_MODEL), BF16),
                        pltpu.VMEM((BLOCKS_PER_ITER, N_KV_HEADS, 2, 2 * WINDOW, N_KEYS), F32),
                        pltpu.VMEM((BLOCKS_PER_ITER, N_KV_HEADS, 2, 2 * WINDOW, N_KEYS), BF16),
                        pltpu.VMEM((BLOCKS_PER_ITER, ATTN_WIDTH // LANES, WINDOW, LANES), F32)],
        compiler_params=pltpu.CompilerParams(dimension_semantics=("arbitrary", "arbitrary"),
                                             vmem_limit_bytes=VMEM_LIMIT_BYTES),
        name="mix_prompt",
    )(x_prompt, g_pre, w_in_b[:, :OFF_GA], b_in2[:, :OFF_GA], w_gate_b, b_gate, rel_bias_table,
      sinks2, _bucket_matrix(WINDOW), _lane_patterns(tm), *weights)
    y_p = _mlp(h_p.reshape(B * S, D_MODEL), g_pre_mlp, w_mlp_up_b, w_mlp_down_b, g_post_mlp)

    tm, ns = SAMPLE_TILE, SEQ_PER_TILE
    n_tok = DB * T
    state16 = jnp.pad(state_pool[l], ((0, 0), (POOL_PAD - POOL_STATE, 0), (0, 0)))
    ck = cache_k_win[l].reshape(DB, WINDOW, KV_WIDTH)
    cv = cache_v_win[l].reshape(DB, WINDOW, KV_WIDTH)
    h_s, k_s, v_s, up_new = pl.pallas_call(
        _mix_sample_kernel,
        grid=(n_tok // tm,),
        in_specs=[pl.BlockSpec((tm, D_MODEL), lambda i: (i, 0)),
                  pl.BlockSpec((ns, WINDOW, KV_WIDTH), lambda i: (i, 0, 0)),
                  pl.BlockSpec((ns, WINDOW, KV_WIDTH), lambda i: (i, 0, 0)),
                  pl.BlockSpec((ns, POOL_PAD, POOL_WIDTH), lambda i: (i, 0, 0)),
                  _full((1, D_MODEL)), _full((D_MODEL, IN_WIDTH)), _full((1, IN_WIDTH)),
                  _smem(), _smem(), _full((T, N_KEYS))] + weight_specs,
        out_specs=[pl.BlockSpec((tm, D_MODEL), lambda i: (i, 0)),
                   pl.BlockSpec((ns, WINDOW, KV_WIDTH), lambda i: (i, 0, 0)),
                   pl.BlockSpec((ns, WINDOW, KV_WIDTH), lambda i: (i, 0, 0)),
                   pl.BlockSpec((tm, POOL_WIDTH), lambda i: (i, 0))],
        out_shape=[jax.ShapeDtypeStruct((n_tok, D_MODEL), F32),
                   jax.ShapeDtypeStruct((DB, WINDOW, KV_WIDTH), F32),
                   jax.ShapeDtypeStruct((DB, WINDOW, KV_WIDTH), F32),
                   jax.ShapeDtypeStruct((n_tok, POOL_WIDTH), F32)],
        scratch_shapes=[pltpu.VMEM((SEQ_GROUP * N_HEADS * T, N_KEYS), F32),
                        pltpu.VMEM((SEQ_GROUP * N_HEADS * T, LANES), F32),
                        pltpu.VMEM((2, T, N_KEYS), F32),
                        pltpu.VMEM((SEQ_GROUP * N_HEADS * T, N_KEYS), F32),
                        pltpu.VMEM((ns, N_KEYS, LANES), BF16),
                        pltpu.VMEM((ns, N_KEYS, 2 * LANES), BF16),
                        pltpu.VMEM((ns, POOL_PAD + T, POOL_WIDTH), F32),
                        pltpu.VMEM((ns, N_HEADS * T, LANES), F32),
                        pltpu.VMEM((ns, N_HEADS * T, LANES), F32)],
        compiler_params=pltpu.CompilerParams(dimension_semantics=("arbitrary",),
                                             vmem_limit_bytes=VMEM_LIMIT_BYTES),
        name="mix_sample",
    )(x_sample.reshape(n_tok, D_MODEL), ck, cv, state16, g_pre, w_in_b, b_in2, rel_bias_table,
      sinks2, _bucket_matrix(T), *weights)
    y_s = _mlp(h_s, g_pre_mlp, w_mlp_up_b, w_mlp_down_b, g_post_mlp)

    kv_shape = (1, DB, WINDOW, N_KV_HEADS, HEAD_DIM)
    k_s = k_s.reshape(kv_shape)
    v_s = v_s.reshape(kv_shape)
    p_s = jnp.concatenate([state_pool[l][:, T:], up_new.reshape(DB, T, POOL_WIDTH)], axis=1)[None]

    return (y_p.reshape(B, S, D_MODEL), y_s.reshape(DB, T, D_MODEL),
            kwin.reshape(1, B, WINDOW, N_KV_HEADS, HEAD_DIM),
            vwin.reshape(1, B, WINDOW, N_KV_HEADS, HEAD_DIM),
            pstate[:, POOL_PAD - POOL_STATE:][None],
            k_s, v_s, p_s)
```

```python
import functools
import math

import jax
import jax.numpy as jnp
import numpy as np
from jax import lax
from jax.experimental import pallas as pl
from jax.experimental.pallas import tpu as pltpu

D_MODEL = 1024
N_HEADS = 8
HEAD_DIM = 64
N_KV_HEADS = 2
GROUP = N_HEADS // N_KV_HEADS
ATTN_WIDTH = N_HEADS * HEAD_DIM
KV_WIDTH = N_KV_HEADS * HEAD_DIM
WINDOW = 128
POOL_WIDTH = D_MODEL // 2
POOL_WINDOWS = (2, 4, 8, 16)
POOL_GROUP_WIDTH = POOL_WIDTH // len(POOL_WINDOWS)
POOL_STATE = max(POOL_WINDOWS) - 1
POOL_PAD = POOL_STATE + 1
D_FF = 4 * D_MODEL
N_BUCKETS = 32
MAX_DISTANCE = 128
RMS_EPS = 1e-6
NEG_INF = -1e30
LOG2E = math.log2(math.e)
PAST_LEN = 16384
IN_WIDTH = ATTN_WIDTH + 2 * KV_WIDTH + POOL_WIDTH + 2 * D_MODEL
OFF_K = ATTN_WIDTH
OFF_V = OFF_K + KV_WIDTH
OFF_POOL = OFF_V + KV_WIDTH
OFF_GA = OFF_POOL + POOL_WIDTH
OFF_GP = OFF_GA + D_MODEL
N_KEYS = 2 * WINDOW
GATE_CHUNKS = 4
GATE_CHUNK = (IN_WIDTH - OFF_GA) // GATE_CHUNKS
BLOCKS_PER_ITER = 4

LANES = 128
SUBLANES = 8
VMEM_LIMIT_BYTES = 56 * 1024 * 1024

TOKEN_TILE = 512
MLP_TILE = 1024
DEC_SEQ = 8
SAMPLE_TILE = 256
SEQ_PER_TILE = SAMPLE_TILE // DEC_SEQ
SEQ_GROUP = 8

BF16 = jnp.bfloat16
F32 = jnp.float32


def _rms(x, g):
    return x * lax.rsqrt(jnp.mean(x * x, axis=-1, keepdims=True) + RMS_EPS) * g


def _sigmoid(x):
    return 0.5 * jnp.tanh(0.5 * x) + 0.5


def _exp2(x):
    return jnp.exp2(x)


def _dot(a, b):
    return jnp.dot(a, b, preferred_element_type=F32)


def _dot_nt(a, b):
    return lax.dot_general(a, b, (((1,), (1,)), ((), ())), preferred_element_type=F32)


def _build_tables(rows, scale, bucket_ref, table_ref, sinks_ref, bias_ref, sink_ref, valid_ref):
    bucket = bucket_ref[...]
    for h in range(N_HEADS):
        acc = jnp.zeros(bucket.shape, F32)
        for b in range(N_BUCKETS):
            acc = jnp.where(bucket == b, table_ref[b, h] * scale, acc)
        bias_ref[h * rows:(h + 1) * rows, :] = acc
        if sink_ref is not None:
            sink_ref[h * rows:(h + 1) * rows, :] = jnp.full((rows, LANES), sinks_ref[0, h] * scale, F32)
    r = lax.broadcasted_iota(jnp.int32, (rows, N_KEYS), 0)
    j = lax.broadcasted_iota(jnp.int32, (rows, N_KEYS), 1)
    cur = jnp.where(j >= WINDOW, jnp.where(j - WINDOW <= r, 1.0, 0.0), 0.0)
    prev = jnp.where(j < WINDOW, jnp.where(j > r, 1.0, 0.0), 0.0)
    valid_ref[0] = cur
    valid_ref[1] = cur + prev


def _head_queries(q):
    lane = lax.broadcasted_iota(jnp.int32, (q.shape[0], LANES), 1)
    lo = lane < HEAD_DIM
    out = []
    for c in range(ATTN_WIDTH // LANES):
        qc = q[:, c * LANES:(c + 1) * LANES]
        qr = pltpu.roll(qc, HEAD_DIM, axis=1)
        g = (2 * c) // GROUP
        if g == 0:
            out.append(jnp.where(lo, qc, 0.0))
            out.append(jnp.where(lo, qr, 0.0))
        else:
            out.append(jnp.where(lo, 0.0, qr))
            out.append(jnp.where(lo, 0.0, qc))
    return out


def _merge_heads(o_heads):
    lane = lax.broadcasted_iota(jnp.int32, o_heads[0].shape, 1)
    lo = lane < HEAD_DIM
    chunks = []
    for c in range(ATTN_WIDTH // LANES):
        e, o = o_heads[2 * c], o_heads[2 * c + 1]
        g = (2 * c) // GROUP
        if g == 0:
            chunks.append(jnp.where(lo, e, pltpu.roll(o, HEAD_DIM, axis=1)))
        else:
            chunks.append(jnp.where(lo, pltpu.roll(e, HEAD_DIM, axis=1), o))
    return jnp.concatenate(chunks, axis=1)


def _pool_means(ext, first_pos):
    del first_pos
    parts = []
    for gi, w in enumerate(POOL_WINDOWS):
        acc = ext[:, gi * POOL_GROUP_WIDTH:(gi + 1) * POOL_GROUP_WIDTH]
        shift = 1
        while shift < w:
            acc = acc + pltpu.roll(acc, shift, axis=0)
            shift *= 2
        parts.append(acc)
    return jnp.concatenate(parts, axis=1)


def _pool_counts(pos):
    lane = lax.broadcasted_iota(jnp.int32, (pos.shape[0], POOL_WIDTH), 1)
    w = jnp.left_shift(2, lane // POOL_GROUP_WIDTH)
    return jnp.minimum(pos + 1, w).astype(F32)


def _mix_tail(x, xn, attn_o, pool_sum, up, cnt, w_in_ref, b_in_ref, w_grp_ref, scale_ref,
              w_attn_up_ref, w_pool_up_ref, w_out_ref, g_post_ref):
    z = (pool_sum / cnt - up).astype(BF16)
    pool_z = (_dot(z, w_grp_ref[...]) * scale_ref[...]).astype(BF16)
    ga = _dot(xn, w_in_ref[:, OFF_GA:OFF_GP]) + b_in_ref[:, OFF_GA:OFF_GP]
    m = _sigmoid(ga) * _dot(attn_o, w_attn_up_ref[...])
    gp = _dot(xn, w_in_ref[:, OFF_GP:IN_WIDTH]) + b_in_ref[:, OFF_GP:IN_WIDTH]
    m = m + _sigmoid(gp) * _dot(pool_z, w_pool_up_ref[...])
    mo = _dot(m.astype(BF16), w_out_ref[...])
    return x + _rms(mo, g_post_ref[...])


def _masked_bias_tables(bucket_ref, table_ref, biasm_ref):
    bucket = bucket_ref[...]
    r = lax.broadcasted_iota(jnp.int32, (WINDOW, N_KEYS), 0)
    j = lax.broadcasted_iota(jnp.int32, (WINDOW, N_KEYS), 1)
    cur = jnp.where(j >= WINDOW, jnp.where(j - WINDOW <= r, 1.0, 0.0), 0.0)
    prev = jnp.where(j < WINDOW, jnp.where(j > r, 1.0, 0.0), 0.0)
    for h in range(N_HEADS):
        acc = jnp.zeros(bucket.shape, F32)
        for b in range(N_BUCKETS):
            acc = jnp.where(bucket == b, table_ref[b, h] * LOG2E, acc)
        biasm_ref[0, h * WINDOW:(h + 1) * WINDOW, :] = jnp.where(cur > 0.5, acc, -jnp.inf)
        biasm_ref[1, h * WINDOW:(h + 1) * WINDOW, :] = jnp.where(cur + prev > 0.5, acc, -jnp.inf)


def _scores_block(i, base, q_ref, ke_ref, ko_ref, s_ref):
    rows = pl.ds(base, WINDOW)
    keys = pl.ds(base, N_KEYS)
    for g in range(N_KV_HEADS):
        qg = jnp.concatenate([q_ref[rows, (2 * g) * LANES:(2 * g + 1) * LANES],
                              q_ref[rows, (2 * g + 1) * LANES:(2 * g + 2) * LANES]], axis=0)
        s_ref[i, g, 0] = _dot_nt(qg, ke_ref[g, keys, :])
        s_ref[i, g, 1] = _dot_nt(qg, ko_ref[g, keys, :])


def _softmax_block(i, has_prev, s_ref, biasm_ref, sinks_ref, p_ref, es_ref):
    lo = lax.broadcasted_iota(jnp.int32, (WINDOW, LANES), 1) < HEAD_DIM
    for g in range(N_KV_HEADS):
        for a in range(2):
            half = slice(a * WINDOW, (a + 1) * WINDOW)
            es = []
            for b in range(2):
                h = GROUP * g + 2 * a + b
                s = s_ref[i, g, b, half, :] + biasm_ref[has_prev, h * WINDOW:(h + 1) * WINDOW, :]
                sink = sinks_ref[0, h] * LOG2E
                m = jnp.maximum(jnp.max(s, axis=-1, keepdims=True), sink)
                p_ref[i, g, b, half, :] = _exp2(s - m).astype(BF16)
                es.append(_exp2(sink - m))
            es_ref[i, 2 * g + a] = jnp.where(lo, es[0], es[1])


def _values_block(i, base, p_ref, es_ref, ve_ref, vo_ref, ao_ref):
    rows = pl.ds(base, WINDOW)
    keys = pl.ds(base, N_KEYS)
    for g in range(N_KV_HEADS):
        o_ext = _dot(p_ref[i, g, 0], ve_ref[g, keys, :]) + _dot(p_ref[i, g, 1], vo_ref[g, keys, :])
        for a in range(2):
            c = 2 * g + a
            o = o_ext[a * WINDOW:(a + 1) * WINDOW]
            ao_ref[rows, c * LANES:(c + 1) * LANES] = (
                o[:, :LANES] / (o[:, LANES:] + es_ref[i, c])).astype(BF16)


def _mix_prompt_kernel(x_ref, g_pre_ref, w_in_ref, b_in_ref, table_ref,
                       sinks_ref, bucket_ref, pat_ref,
                       w_grp_ref, scale_ref, w_attn_up_ref, w_pool_up_ref, w_out_ref, g_post_ref,
                       h_ref, kwin_ref, vwin_ref, pstate_ref,
                       biasm_ref, ke_ref, ko_ref, ve_ref, vo_ref, ext_ref, q_ref, ao_ref,
                       sg_ref, xn_ref, s_ref, p_ref, es_ref):
    tm = TOKEN_TILE
    t = pl.program_id(1)
    lane = lax.broadcasted_iota(jnp.int32, (tm, LANES), 1)
    lo = lane < HEAD_DIM

    ones_lo, ones_hi = pat_ref[1], pat_ref[2]

    @pl.when((pl.program_id(0) == 0) & (t == 0))
    def _():
        _masked_bias_tables(bucket_ref, table_ref, biasm_ref)

    @pl.when(t == 0)
    def _():
        zeros = pat_ref[0, 0:WINDOW, :]
        for g in range(N_KV_HEADS):
            ke_ref[g, 0:WINDOW, :] = zeros
            ko_ref[g, 0:WINDOW, :] = zeros
            ve_ref[g, 0:WINDOW, :] = jnp.concatenate([zeros, pat_ref[1, 0:WINDOW, :]], axis=1)
            vo_ref[g, 0:WINDOW, :] = jnp.concatenate([zeros, pat_ref[2, 0:WINDOW, :]], axis=1)
        ext_ref[0:POOL_PAD, :] = jnp.zeros((POOL_PAD, POOL_WIDTH), F32)

    x = x_ref[...]
    xn = _rms(x, g_pre_ref[...]).astype(BF16)
    xn_ref[...] = xn
    u = _dot(xn, w_in_ref[:, 0:OFF_GA]) + b_in_ref[:, 0:OFF_GA]
    k = u[:, OFF_K:OFF_V]
    v = u[:, OFF_V:OFF_POOL]
    up = u[:, OFF_POOL:OFF_GA]

    q_ref[...] = (u[:, 0:OFF_K] * (HEAD_DIM ** -0.5 * LOG2E)).astype(BF16)
    kwin_ref[...] = k[tm - WINDOW:, :]
    vwin_ref[...] = v[tm - WINDOW:, :]
    pstate_ref[...] = up[tm - POOL_PAD:, :]
    ext_ref[POOL_PAD:, :] = up
    kr = pltpu.roll(k, HEAD_DIM, axis=1)
    vr = pltpu.roll(v, HEAD_DIM, axis=1)
    ke_ref[0, WINDOW:, :] = jnp.where(lo, k, 0.0).astype(BF16)
    ko_ref[0, WINDOW:, :] = jnp.where(lo, 0.0, kr).astype(BF16)
    ke_ref[1, WINDOW:, :] = jnp.where(lo, kr, 0.0).astype(BF16)
    ko_ref[1, WINDOW:, :] = jnp.where(lo, 0.0, k).astype(BF16)
    ve_ref[0, WINDOW:, :] = jnp.concatenate([jnp.where(lo, v, 0.0).astype(BF16), ones_lo], axis=1)
    vo_ref[0, WINDOW:, :] = jnp.concatenate([jnp.where(lo, 0.0, vr).astype(BF16), ones_hi], axis=1)
    ve_ref[1, WINDOW:, :] = jnp.concatenate([jnp.where(lo, vr, 0.0).astype(BF16), ones_lo], axis=1)
    vo_ref[1, WINDOW:, :] = jnp.concatenate([jnp.where(lo, 0.0, v).astype(BF16), ones_hi], axis=1)

    nb = BLOCKS_PER_ITER
    assert nb == GATE_CHUNKS

    def finish(_, carry):
        bases = [d * WINDOW for d in range(nb)]
        for d in range(nb):
            _scores_block(d, bases[d], q_ref, ke_ref, ko_ref, s_ref)
        for d in range(nb):
            cols = slice(OFF_GA + d * GATE_CHUNK, OFF_GA + (d + 1) * GATE_CHUNK)
            sg_ref[d] = _sigmoid(_dot(xn_ref[...], w_in_ref[:, cols]) + b_in_ref[:, cols])
        for d in range(nb):
            has_prev = jnp.where(t == 0, 0, 1) if d == 0 else 1
            _softmax_block(d, has_prev, s_ref, biasm_ref, sinks_ref, p_ref, es_ref)
        ext = ext_ref[...]
        pos = t * tm + lax.broadcasted_iota(jnp.int32, (tm, 1), 0)
        z = (_pool_means(ext, None)[POOL_PAD:, :] / _pool_counts(pos) - ext[POOL_PAD:, :]).astype(BF16)
        pool_z = (_dot(z, w_grp_ref[...]) * scale_ref[...]).astype(BF16)
        pp = _dot(pool_z, w_pool_up_ref[...])
        for d in range(nb):
            _values_block(d, bases[d], p_ref, es_ref, ve_ref, vo_ref, ao_ref)
        a = _dot(ao_ref[...], w_attn_up_ref[...])
        half = D_MODEL // 2
        m = jnp.concatenate([sg_ref[0] * a[:, :half] + sg_ref[2] * pp[:, :half],
                             sg_ref[1] * a[:, half:] + sg_ref[3] * pp[:, half:]], axis=1)
        mo = _dot(m.astype(BF16), w_out_ref[...])
        h_ref[...] = x_ref[...] + _rms(mo, g_post_ref[...])
        return carry

    lax.fori_loop(0, 1 + jnp.minimum(t, 0), finish, 0)

    for g in range(N_KV_HEADS):
        ke_ref[g, 0:WINDOW, :] = ke_ref[g, tm:tm + WINDOW, :]
        ko_ref[g, 0:WINDOW, :] = ko_ref[g, tm:tm + WINDOW, :]
        ve_ref[g, 0:WINDOW, :] = ve_ref[g, tm:tm + WINDOW, :]
        vo_ref[g, 0:WINDOW, :] = vo_ref[g, tm:tm + WINDOW, :]
    ext_ref[0:POOL_PAD, :] = ext_ref[tm:tm + POOL_PAD, :]


def _mix_sample_kernel(x_ref, ck_ref, cv_ref, state_ref, g_pre_ref, w_in_ref, b_in_ref, table_ref,
                       sinks_ref, bucket_ref, w_grp_ref, scale_ref, w_attn_up_ref, w_pool_up_ref,
                       w_out_ref, g_post_ref,
                       h_ref, kout_ref, vout_ref, up_ref,
                       bias_ref, sink_ref, valid_ref, vfull_ref, kx_ref, vx_ref, ext_ref, qh_ref,
                       oh_ref):
    tm = SAMPLE_TILE
    ns = SEQ_PER_TILE
    T = DEC_SEQ
    R = N_HEADS * T

    @pl.when(pl.program_id(0) == 0)
    def _():
        _build_tables(T, LOG2E, bucket_ref, table_ref, sinks_ref, bias_ref, sink_ref, valid_ref)
        for j in range(N_HEADS):
            vfull_ref[j * T:(j + 1) * T, :] = valid_ref[1]
        for j in range(1, SEQ_GROUP):
            bias_ref[j * R:(j + 1) * R, :] = bias_ref[0:R, :]
            sink_ref[j * R:(j + 1) * R, :] = sink_ref[0:R, :]
            vfull_ref[j * R:(j + 1) * R, :] = vfull_ref[0:R, :]
        vx_ref[:, :, LANES:] = jnp.ones((ns, N_KEYS, LANES), BF16)

    x = x_ref[...]
    xn = _rms(x, g_pre_ref[...]).astype(BF16)
    u = _dot(xn, w_in_ref[:, 0:OFF_GA]) + b_in_ref[:, 0:OFF_GA]
    q = u[:, 0:OFF_K] * (HEAD_DIM ** -0.5 * LOG2E)
    k = u[:, OFF_K:OFF_V]
    v = u[:, OFF_V:OFF_POOL]
    up = u[:, OFF_POOL:OFF_GA]
    k3 = k.reshape(ns, T, LANES)
    v3 = v.reshape(ns, T, LANES)
    ck = ck_ref[...]
    cv = cv_ref[...]

    kout_ref[:, 0:WINDOW - T, :] = ck[:, T:, :]
    kout_ref[:, WINDOW - T:, :] = k3
    vout_ref[:, 0:WINDOW - T, :] = cv[:, T:, :]
    vout_ref[:, WINDOW - T:, :] = v3
    up_ref[...] = up

    pad = jnp.zeros((ns, WINDOW - T, LANES), F32)
    kx_ref[:, 0:WINDOW, :] = ck.astype(BF16)
    kx_ref[:, WINDOW:, :] = jnp.concatenate([k3, pad], axis=1).astype(BF16)
    vx_ref[:, 0:WINDOW, 0:LANES] = cv.astype(BF16)
    vx_ref[:, WINDOW:, 0:LANES] = jnp.concatenate([v3, pad], axis=1).astype(BF16)
    ext_ref[:, 0:POOL_PAD, :] = state_ref[...]
    ext_ref[:, POOL_PAD:, :] = up.reshape(ns, T, POOL_WIDTH)
    for h, qh in enumerate(_head_queries(q)):
        qh_ref[:, h * T:(h + 1) * T, :] = qh.reshape(ns, T, LANES)

    def group(gi, carry):
        s0 = gi * SEQ_GROUP
        s = jnp.concatenate([_dot_nt(qh_ref[s0 + j].astype(BF16), kx_ref[s0 + j])
                             for j in range(SEQ_GROUP)], axis=0)
        s = jnp.where(vfull_ref[...] > 0.5, s + bias_ref[...], NEG_INF)
        sink = sink_ref[:, 0:1]
        m = jnp.maximum(jnp.max(s, axis=-1, keepdims=True), sink)
        p = _exp2(s - m).astype(BF16)
        es = _exp2(sink - m)
        for j in range(SEQ_GROUP):
            o = _dot(p[j * R:(j + 1) * R], vx_ref[s0 + j])
            oh_ref[s0 + j] = o[:, :LANES] / (o[:, LANES:] + es[j * R:(j + 1) * R])
        return carry

    lax.fori_loop(0, ns // SEQ_GROUP, group, 0)

    o_heads = [oh_ref[:, h * T:(h + 1) * T, :].reshape(tm, LANES) for h in range(N_HEADS)]
    attn_o = _merge_heads(o_heads).astype(BF16)

    ext = ext_ref[...].reshape(ns * (POOL_PAD + T), POOL_WIDTH)
    pool_sum = _pool_means(ext, None).reshape(ns, POOL_PAD + T, POOL_WIDTH)[:, POOL_PAD:, :]
    pool_sum = pool_sum.reshape(tm, POOL_WIDTH)
    row = lax.broadcasted_iota(jnp.int32, (tm, 1), 0)
    pos = PAST_LEN + (row & (T - 1))
    h_ref[...] = _mix_tail(x, xn, attn_o, pool_sum, up, _pool_counts(pos),
                           w_in_ref, b_in_ref, w_grp_ref, scale_ref,
                           w_attn_up_ref, w_pool_up_ref, w_out_ref, g_post_ref)


def _mlp_kernel(h_ref, g_pre_ref, w_up_ref, w_down_ref, g_post_ref, y_ref):
    h = h_ref[...]
    hn = _rms(h, g_pre_ref[...]).astype(BF16)
    f = jnp.zeros(h.shape, F32)
    chunk = D_FF // 4
    for c in range(D_FF // chunk):
        a = jnp.maximum(_dot(hn, w_up_ref[:, c * chunk:(c + 1) * chunk]), 0.0)
        f = f + _dot((a * a).astype(BF16), w_down_ref[c * chunk:(c + 1) * chunk, :])
    y_ref[...] = h + _rms(f, g_post_ref[...])


def _full(shape):
    return pl.BlockSpec(shape, lambda *_: (0,) * len(shape), pipeline_mode=pl.Buffered(1))


def _smem():
    return pl.BlockSpec(memory_space=pltpu.SMEM)


def _rel_bucket(dist):
    d = np.maximum(dist, 0)
    max_exact = N_BUCKETS // 2
    large = max_exact + (np.log(np.maximum(d, 1).astype(np.float32) / np.float32(max_exact))
                         / np.float32(math.log(MAX_DISTANCE / max_exact))
                         * np.float32(N_BUCKETS - max_exact)).astype(np.int32)
    large = np.minimum(large, N_BUCKETS - 1)
    return np.where(d < max_exact, d, large)


def _bucket_matrix(rows):
    r = np.arange(rows)[:, None]
    j = np.arange(N_KEYS)[None, :]
    return _rel_bucket(r + WINDOW - j).astype(np.int32)


def _lane_patterns(rows):
    lo = (jnp.arange(LANES) < HEAD_DIM).astype(BF16)
    pats = jnp.stack([jnp.zeros_like(lo), lo, 1 - lo])
    return jnp.broadcast_to(pats[:, None, :], (3, rows, LANES))


def _mlp(h2d, g_pre, w_up, w_down, g_post):
    n = h2d.shape[0]
    tile = MLP_TILE
    return pl.pallas_call(
        _mlp_kernel,
        grid=(n // tile,),
        in_specs=[pl.BlockSpec((tile, D_MODEL), lambda i: (i, 0)),
                  _full((1, D_MODEL)), _full((D_MODEL, D_FF)), _full((D_FF, D_MODEL)),
                  _full((1, D_MODEL))],
        out_specs=pl.BlockSpec((tile, D_MODEL), lambda i: (i, 0)),
        out_shape=jax.ShapeDtypeStruct((n, D_MODEL), F32),
        compiler_params=pltpu.CompilerParams(dimension_semantics=("arbitrary",),
                                             vmem_limit_bytes=VMEM_LIMIT_BYTES),
        name="mlp",
    )(h2d, g_pre, w_up, w_down, g_post)


def kernel(x_prompt, x_sample, cache_k_win, cache_v_win, state_pool, norm_pre_mix, norm_post_mix,
           norm_pre_mlp, norm_post_mlp, w_in, b_in, attn_sinks, rel_bias_table, w_attn_up,
           w_pool_grp, pool_scale, w_pool_up, w_out, w_mlp_up, w_mlp_down):
    B, S, _ = x_prompt.shape
    DB, T, _ = x_sample.shape
    depth = w_in.shape[0]
    assert depth == 1 and S % TOKEN_TILE == 0 and (DB * T) % TOKEN_TILE == 0
    assert T == DEC_SEQ and (DB * T) % SAMPLE_TILE == 0
    assert GATE_CHUNKS == TOKEN_TILE // WINDOW

    l = 0
    w_in_b = w_in[l].astype(BF16)
    w_grp_b = jax.scipy.linalg.block_diag(*[w_pool_grp[l, g] for g in range(len(POOL_WINDOWS))]
                                          ).astype(BF16)
    w_attn_up_b = w_attn_up[l].astype(BF16)
    w_pool_up_b = w_pool_up[l].astype(BF16)
    w_out_b = w_out[l].astype(BF16)
    w_mlp_up_b = w_mlp_up[l].astype(BF16)
    w_mlp_down_b = w_mlp_down[l].astype(BF16)
    b_in2 = b_in[l][None, :]
    sinks2 = attn_sinks[l][None, :]
    scale2 = pool_scale[l][None, :]
    g_pre, g_post = norm_pre_mix[l][None, :], norm_post_mix[l][None, :]
    g_pre_mlp, g_post_mlp = norm_pre_mlp[l][None, :], norm_post_mlp[l][None, :]

    weight_specs = [_full((POOL_WIDTH, POOL_WIDTH)), _full((1, POOL_WIDTH)),
                    _full((ATTN_WIDTH, D_MODEL)), _full((POOL_WIDTH, D_MODEL)),
                    _full((D_MODEL, D_MODEL)), _full((1, D_MODEL))]
    weights = (w_grp_b, scale2, w_attn_up_b, w_pool_up_b, w_out_b, g_post)

    tm = TOKEN_TILE
    h_p, kwin, vwin, pstate = pl.pallas_call(
        _mix_prompt_kernel,
        grid=(B, S // tm),
        in_specs=[pl.BlockSpec((None, tm, D_MODEL), lambda b, t: (b, t, 0)),
                  _full((1, D_MODEL)), _full((D_MODEL, IN_WIDTH)), _full((1, IN_WIDTH)),
                  _smem(), _smem(), _full((WINDOW, N_KEYS)), _full((3, tm, LANES))] + weight_specs,
        out_specs=[pl.BlockSpec((None, tm, D_MODEL), lambda b, t: (b, t, 0)),
                   pl.BlockSpec((None, WINDOW, KV_WIDTH), lambda b, t: (b, 0, 0)),
                   pl.BlockSpec((None, WINDOW, KV_WIDTH), lambda b, t: (b, 0, 0)),
                   pl.BlockSpec((None, POOL_PAD, POOL_WIDTH), lambda b, t: (b, 0, 0))],
        out_shape=[jax.ShapeDtypeStruct((B, S, D_MODEL), F32),
                   jax.ShapeDtypeStruct((B, WINDOW, KV_WIDTH), F32),
                   jax.ShapeDtypeStruct((B, WINDOW, KV_WIDTH), F32),
                   jax.ShapeDtypeStruct((B, POOL_PAD, POOL_WIDTH), F32)],
        scratch_shapes=[pltpu.VMEM((2, N_HEADS * WINDOW, N_KEYS), F32),
                        pltpu.VMEM((N_KV_HEADS, WINDOW + tm, LANES), BF16),
                        pltpu.VMEM((N_KV_HEADS, WINDOW + tm, LANES), BF16),
                        pltpu.VMEM((N_KV_HEADS, WINDOW + tm, 2 * LANES), BF16),
                        pltpu.VMEM((N_KV_HEADS, WINDOW + tm, 2 * LANES), BF16),
                        pltpu.VMEM((POOL_PAD + tm, POOL_WIDTH), F32),
                        pltpu.VMEM((tm, ATTN_WIDTH), BF16),
                        pltpu.VMEM((tm, ATTN_WIDTH), BF16),
                        pltpu.VMEM((GATE_CHUNKS, tm, GATE_CHUNK), F32),
                        pltpu.VMEM((tm, D_MODEL), BF16),
                        pltpu.VMEM((BLOCKS_PER_ITER, N_KV_HEADS, 2, 2 * WINDOW, N_KEYS), F32),
                        pltpu.VMEM((BLOCKS_PER_ITER, N_KV_HEADS, 2, 2 * WINDOW, N_KEYS), BF16),
                        pltpu.VMEM((BLOCKS_PER_ITER, ATTN_WIDTH // LANES, WINDOW, LANES), F32)],
        compiler_params=pltpu.CompilerParams(dimension_semantics=("arbitrary", "arbitrary"),
                                             vmem_limit_bytes=VMEM_LIMIT_BYTES),
        name="mix_prompt",
    )(x_prompt, g_pre, w_in_b, b_in2, rel_bias_table,
      sinks2, _bucket_matrix(WINDOW), _lane_patterns(tm), *weights)
    y_p = _mlp(h_p.reshape(B * S, D_MODEL), g_pre_mlp, w_mlp_up_b, w_mlp_down_b, g_post_mlp)

    tm, ns = SAMPLE_TILE, SEQ_PER_TILE
    n_tok = DB * T
    state16 = jnp.pad(state_pool[l], ((0, 0), (POOL_PAD - POOL_STATE, 0), (0, 0)))
    ck = cache_k_win[l].reshape(DB, WINDOW, KV_WIDTH)
    cv = cache_v_win[l].reshape(DB, WINDOW, KV_WIDTH)
    h_s, k_s, v_s, up_new = pl.pallas_call(
        _mix_sample_kernel,
        grid=(n_tok // tm,),
        in_specs=[pl.BlockSpec((tm, D_MODEL), lambda i: (i, 0)),
                  pl.BlockSpec((ns, WINDOW, KV_WIDTH), lambda i: (i, 0, 0)),
                  pl.BlockSpec((ns, WINDOW, KV_WIDTH), lambda i: (i, 0, 0)),
                  pl.BlockSpec((ns, POOL_PAD, POOL_WIDTH), lambda i: (i, 0, 0)),
                  _full((1, D_MODEL)), _full((D_MODEL, IN_WIDTH)), _full((1, IN_WIDTH)),
                  _smem(), _smem(), _full((T, N_KEYS))] + weight_specs,
        out_specs=[pl.BlockSpec((tm, D_MODEL), lambda i: (i, 0)),
                   pl.BlockSpec((ns, WINDOW, KV_WIDTH), lambda i: (i, 0, 0)),
                   pl.BlockSpec((ns, WINDOW, KV_WIDTH), lambda i: (i, 0, 0)),
                   pl.BlockSpec((tm, POOL_WIDTH), lambda i: (i, 0))],
        out_shape=[jax.ShapeDtypeStruct((n_tok, D_MODEL), F32),
                   jax.ShapeDtypeStruct((DB, WINDOW, KV_WIDTH), F32),
                   jax.ShapeDtypeStruct((DB, WINDOW, KV_WIDTH), F32),
                   jax.ShapeDtypeStruct((n_tok, POOL_WIDTH), F32)],
        scratch_shapes=[pltpu.VMEM((SEQ_GROUP * N_HEADS * T, N_KEYS), F32),
                        pltpu.VMEM((SEQ_GROUP * N_HEADS * T, LANES), F32),
                        pltpu.VMEM((2, T, N_KEYS), F32),
                        pltpu.VMEM((SEQ_GROUP * N_HEADS * T, N_KEYS), F32),
                        pltpu.VMEM((ns, N_KEYS, LANES), BF16),
                        pltpu.VMEM((ns, N_KEYS, 2 * LANES), BF16),
                        pltpu.VMEM((ns, POOL_PAD + T, POOL_WIDTH), F32),
                        pltpu.VMEM((ns, N_HEADS * T, LANES), F32),
                        pltpu.VMEM((ns, N_HEADS * T, LANES), F32)],
        compiler_params=pltpu.CompilerParams(dimension_semantics=("arbitrary",),
                                             vmem_limit_bytes=VMEM_LIMIT_BYTES),
        name="mix_sample",
    )(x_sample.reshape(n_tok, D_MODEL), ck, cv, state16, g_pre, w_in_b, b_in2, rel_bias_table,
      sinks2, _bucket_matrix(T), *weights)
    y_s = _mlp(h_s, g_pre_mlp, w_mlp_up_b, w_mlp_down_b, g_post_mlp)

    kv_shape = (1, DB, WINDOW, N_KV_HEADS, HEAD_DIM)
    k_s = k_s.reshape(kv_shape)
    v_s = v_s.reshape(kv_shape)
    p_s = jnp.concatenate([state_pool[l][:, T:], up_new.reshape(DB, T, POOL_WIDTH)], axis=1)[None]

    return (y_p.reshape(B, S, D_MODEL), y_s.reshape(DB, T, D_MODEL),
            kwin.reshape(1, B, WINDOW, N_KV_HEADS, HEAD_DIM),
            vwin.reshape(1, B, WINDOW, N_KV_HEADS, HEAD_DIM),
            pstate[:, POOL_PAD - POOL_STATE:][None],
            k_s, v_s, p_s)
```

```python
import functools
import math

import jax
import jax.numpy as jnp
import numpy as np
from jax import lax
from jax.experimental import pallas as pl
from jax.experimental.pallas import tpu as pltpu

D_MODEL = 1024
N_HEADS = 8
HEAD_DIM = 64
N_KV_HEADS = 2
GROUP = N_HEADS // N_KV_HEADS
ATTN_WIDTH = N_HEADS * HEAD_DIM
KV_WIDTH = N_KV_HEADS * HEAD_DIM
WINDOW = 128
POOL_WIDTH = D_MODEL // 2
POOL_WINDOWS = (2, 4, 8, 16)
POOL_GROUP_WIDTH = POOL_WIDTH // len(POOL_WINDOWS)
POOL_STATE = max(POOL_WINDOWS) - 1
POOL_PAD = POOL_STATE + 1
D_FF = 4 * D_MODEL
N_BUCKETS = 32
MAX_DISTANCE = 128
RMS_EPS = 1e-6
NEG_INF = -1e30
LOG2E = math.log2(math.e)
PAST_LEN = 16384
IN_WIDTH = ATTN_WIDTH + 2 * KV_WIDTH + POOL_WIDTH + 2 * D_MODEL
OFF_K = ATTN_WIDTH
OFF_V = OFF_K + KV_WIDTH
OFF_POOL = OFF_V + KV_WIDTH
OFF_GA = OFF_POOL + POOL_WIDTH
OFF_GP = OFF_GA + D_MODEL
N_KEYS = 2 * WINDOW
GATE_CHUNKS = 4
GATE_CHUNK = (IN_WIDTH - OFF_GA) // GATE_CHUNKS
BLOCKS_PER_ITER = 4

LANES = 128
SUBLANES = 8
VMEM_LIMIT_BYTES = 56 * 1024 * 1024

TOKEN_TILE = 512
MLP_TILE = 1024
DEC_SEQ = 8
SAMPLE_TILE = 256
SEQ_PER_TILE = SAMPLE_TILE // DEC_SEQ
SEQ_GROUP = 8

BF16 = jnp.bfloat16
F32 = jnp.float32


def _rms(x, g):
    return x * lax.rsqrt(jnp.mean(x * x, axis=-1, keepdims=True) + RMS_EPS) * g


def _sigmoid(x):
    return 0.5 * jnp.tanh(0.5 * x) + 0.5


def _exp2(x):
    return jnp.exp2(x)


def _dot(a, b):
    return jnp.dot(a, b, preferred_element_type=F32)


def _dot_nt(a, b):
    return lax.dot_general(a, b, (((1,), (1,)), ((), ())), preferred_element_type=F32)


def _build_tables(rows, scale, bucket_ref, table_ref, sinks_ref, bias_ref, sink_ref, valid_ref):
    bucket = bucket_ref[...]
    for h in range(N_HEADS):
        acc = jnp.zeros(bucket.shape, F32)
        for b in range(N_BUCKETS):
            acc = jnp.where(bucket == b, table_ref[b, h] * scale, acc)
        bias_ref[h * rows:(h + 1) * rows, :] = acc
        if sink_ref is not None:
            sink_ref[h * rows:(h + 1) * rows, :] = jnp.full((rows, LANES), sinks_ref[0, h] * scale, F32)
    r = lax.broadcasted_iota(jnp.int32, (rows, N_KEYS), 0)
    j = lax.broadcasted_iota(jnp.int32, (rows, N_KEYS), 1)
    cur = jnp.where(j >= WINDOW, jnp.where(j - WINDOW <= r, 1.0, 0.0), 0.0)
    prev = jnp.where(j < WINDOW, jnp.where(j > r, 1.0, 0.0), 0.0)
    valid_ref[0] = cur
    valid_ref[1] = cur + prev


def _head_queries(q):
    lane = lax.broadcasted_iota(jnp.int32, (q.shape[0], LANES), 1)
    lo = lane < HEAD_DIM
    out = []
    for c in range(ATTN_WIDTH // LANES):
        qc = q[:, c * LANES:(c + 1) * LANES]
        qr = pltpu.roll(qc, HEAD_DIM, axis=1)
        g = (2 * c) // GROUP
        if g == 0:
            out.append(jnp.where(lo, qc, 0.0))
            out.append(jnp.where(lo, qr, 0.0))
        else:
            out.append(jnp.where(lo, 0.0, qr))
            out.append(jnp.where(lo, 0.0, qc))
    return out


def _merge_heads(o_heads):
    lane = lax.broadcasted_iota(jnp.int32, o_heads[0].shape, 1)
    lo = lane < HEAD_DIM
    chunks = []
    for c in range(ATTN_WIDTH // LANES):
        e, o = o_heads[2 * c], o_heads[2 * c + 1]
        g = (2 * c) // GROUP
        if g == 0:
            chunks.append(jnp.where(lo, e, pltpu.roll(o, HEAD_DIM, axis=1)))
        else:
            chunks.append(jnp.where(lo, pltpu.roll(e, HEAD_DIM, axis=1), o))
    return jnp.concatenate(chunks, axis=1)


def _pool_means(ext, first_pos):
    del first_pos
    parts = []
    for gi, w in enumerate(POOL_WINDOWS):
        acc = ext[:, gi * POOL_GROUP_WIDTH:(gi + 1) * POOL_GROUP_WIDTH]
        shift = 1
        while shift < w:
            acc = acc + pltpu.roll(acc, shift, axis=0)
            shift *= 2
        parts.append(acc)
    return jnp.concatenate(parts, axis=1)


def _pool_counts(pos):
    lane = lax.broadcasted_iota(jnp.int32, (pos.shape[0], POOL_WIDTH), 1)
    w = jnp.left_shift(2, lane // POOL_GROUP_WIDTH)
    return jnp.minimum(pos + 1, w).astype(F32)


def _mix_tail(x, xn, attn_o, pool_sum, up, cnt, w_in_ref, b_in_ref, w_grp_ref, scale_ref,
              w_attn_up_ref, w_pool_up_ref, w_out_ref, g_post_ref):
    z = (pool_sum / cnt - up).astype(BF16)
    pool_z = (_dot(z, w_grp_ref[...]) * scale_ref[...]).astype(BF16)
    ga = _dot(xn, w_in_ref[:, OFF_GA:OFF_GP]) + b_in_ref[:, OFF_GA:OFF_GP]
    m = _sigmoid(ga) * _dot(attn_o, w_attn_up_ref[...])
    gp = _dot(xn, w_in_ref[:, OFF_GP:IN_WIDTH]) + b_in_ref[:, OFF_GP:IN_WIDTH]
    m = m + _sigmoid(gp) * _dot(pool_z, w_pool_up_ref[...])
    mo = _dot(m.astype(BF16), w_out_ref[...])
    return x + _rms(mo, g_post_ref[...])


def _masked_bias_tables(bucket_ref, table_ref, biasm_ref):
    bucket = bucket_ref[...]
    r = lax.broadcasted_iota(jnp.int32, (WINDOW, N_KEYS), 0)
    j = lax.broadcasted_iota(jnp.int32, (WINDOW, N_KEYS), 1)
    cur = jnp.where(j >= WINDOW, jnp.where(j - WINDOW <= r, 1.0, 0.0), 0.0)
    prev = jnp.where(j < WINDOW, jnp.where(j > r, 1.0, 0.0), 0.0)
    for h in range(N_HEADS):
        acc = jnp.zeros(bucket.shape, F32)
        for b in range(N_BUCKETS):
            acc = jnp.where(bucket == b, table_ref[b, h] * LOG2E, acc)
        biasm_ref[0, h * WINDOW:(h + 1) * WINDOW, :] = jnp.where(cur > 0.5, acc, -jnp.inf)
        biasm_ref[1, h * WINDOW:(h + 1) * WINDOW, :] = jnp.where(cur + prev > 0.5, acc, -jnp.inf)


def _scores_block(i, base, q_ref, ke_ref, ko_ref, s_ref):
    rows = pl.ds(base, WINDOW)
    keys = pl.ds(base, N_KEYS)
    for g in range(N_KV_HEADS):
        qg = jnp.concatenate([q_ref[rows, (2 * g) * LANES:(2 * g + 1) * LANES],
                              q_ref[rows, (2 * g + 1) * LANES:(2 * g + 2) * LANES]], axis=0)
        s_ref[i, g, 0] = _dot_nt(qg, ke_ref[g, keys, :])
        s_ref[i, g, 1] = _dot_nt(qg, ko_ref[g, keys, :])


def _softmax_block(i, has_prev, s_ref, biasm_ref, sinks_ref, p_ref, es_ref):
    lo = lax.broadcasted_iota(jnp.int32, (WINDOW, LANES), 1) < HEAD_DIM
    for g in range(N_KV_HEADS):
        for a in range(2):
            half = slice(a * WINDOW, (a + 1) * WINDOW)
            es = []
            for b in range(2):
                h = GROUP * g + 2 * a + b
                s = s_ref[i, g, b, half, :] + biasm_ref[has_prev, h * WINDOW:(h + 1) * WINDOW, :]
                sink = sinks_ref[0, h] * LOG2E
                m = jnp.maximum(jnp.max(s, axis=-1, keepdims=True), sink)
                p_ref[i, g, b, half, :] = _exp2(s - m).astype(BF16)
                es.append(_exp2(sink - m))
            es_ref[i, 2 * g + a] = jnp.where(lo, es[0], es[1])


def _values_block(i, base, p_ref, es_ref, ve_ref, vo_ref, ao_ref):
    rows = pl.ds(base, WINDOW)
    keys = pl.ds(base, N_KEYS)
    for g in range(N_KV_HEADS):
        o_ext = _dot(p_ref[i, g, 0], ve_ref[g, keys, :]) + _dot(p_ref[i, g, 1], vo_ref[g, keys, :])
        for a in range(2):
            c = 2 * g + a
            o = o_ext[a * WINDOW:(a + 1) * WINDOW]
            ao_ref[rows, c * LANES:(c + 1) * LANES] = (
                o[:, :LANES] / (o[:, LANES:] + es_ref[i, c])).astype(BF16)


def _mix_prompt_kernel(x_ref, g_pre_ref, w_in_ref, b_in_ref, table_ref,
                       sinks_ref, bucket_ref, pat_ref,
                       w_grp_ref, scale_ref, w_attn_up_ref, w_pool_up_ref, w_out_ref, g_post_ref,
                       h_ref, kwin_ref, vwin_ref, pstate_ref,
                       biasm_ref, ke_ref, ko_ref, ve_ref, vo_ref, ext_ref, q_ref, ao_ref,
                       sg_ref, xn_ref, s_ref, p_ref, es_ref):
    tm = TOKEN_TILE
    t = pl.program_id(1)
    lane = lax.broadcasted_iota(jnp.int32, (tm, LANES), 1)
    lo = lane < HEAD_DIM

    ones_lo, ones_hi = pat_ref[1], pat_ref[2]

    @pl.when((pl.program_id(0) == 0) & (t == 0))
    def _():
        _masked_bias_tables(bucket_ref, table_ref, biasm_ref)

    @pl.when(t == 0)
    def _():
        zeros = pat_ref[0, 0:WINDOW, :]
        for g in range(N_KV_HEADS):
            ke_ref[g, 0:WINDOW, :] = zeros
            ko_ref[g, 0:WINDOW, :] = zeros
            ve_ref[g, 0:WINDOW, :] = jnp.concatenate([zeros, pat_ref[1, 0:WINDOW, :]], axis=1)
            vo_ref[g, 0:WINDOW, :] = jnp.concatenate([zeros, pat_ref[2, 0:WINDOW, :]], axis=1)
        ext_ref[0:POOL_PAD, :] = jnp.zeros((POOL_PAD, POOL_WIDTH), F32)

    us = []
    for r in range(2):
        rows = slice(r * (tm // 2), (r + 1) * (tm // 2))
        xn = _rms(x_ref[rows, :], g_pre_ref[...]).astype(BF16)
        xn_ref[rows, :] = xn
        us.append(_dot(xn, w_in_ref[:, 0:OFF_GA]) + b_in_ref[:, 0:OFF_GA])
    u = jnp.concatenate(us, axis=0)
    k = u[:, OFF_K:OFF_V]
    v = u[:, OFF_V:OFF_POOL]
    up = u[:, OFF_POOL:OFF_GA]

    q_ref[...] = (u[:, 0:OFF_K] * (HEAD_DIM ** -0.5 * LOG2E)).astype(BF16)
    kwin_ref[...] = k[tm - WINDOW:, :]
    vwin_ref[...] = v[tm - WINDOW:, :]
    pstate_ref[...] = up[tm - POOL_PAD:, :]
    ext_ref[POOL_PAD:, :] = up
    kr = pltpu.roll(k, HEAD_DIM, axis=1)
    vr = pltpu.roll(v, HEAD_DIM, axis=1)
    ke_ref[0, WINDOW:, :] = jnp.where(lo, k, 0.0).astype(BF16)
    ko_ref[0, WINDOW:, :] = jnp.where(lo, 0.0, kr).astype(BF16)
    ke_ref[1, WINDOW:, :] = jnp.where(lo, kr, 0.0).astype(BF16)
    ko_ref[1, WINDOW:, :] = jnp.where(lo, 0.0, k).astype(BF16)
    ve_ref[0, WINDOW:, :] = jnp.concatenate([jnp.where(lo, v, 0.0).astype(BF16), ones_lo], axis=1)
    vo_ref[0, WINDOW:, :] = jnp.concatenate([jnp.where(lo, 0.0, vr).astype(BF16), ones_hi], axis=1)
    ve_ref[1, WINDOW:, :] = jnp.concatenate([jnp.where(lo, vr, 0.0).astype(BF16), ones_lo], axis=1)
    vo_ref[1, WINDOW:, :] = jnp.concatenate([jnp.where(lo, 0.0, v).astype(BF16), ones_hi], axis=1)

    nb = BLOCKS_PER_ITER
    assert nb == GATE_CHUNKS

    def finish(_, carry):
        bases = [d * WINDOW for d in range(nb)]
        for d in range(nb):
            _scores_block(d, bases[d], q_ref, ke_ref, ko_ref, s_ref)
        for d in range(nb):
            cols = slice(OFF_GA + d * GATE_CHUNK, OFF_GA + (d + 1) * GATE_CHUNK)
            sg_ref[d] = _sigmoid(_dot(xn_ref[...], w_in_ref[:, cols]) + b_in_ref[:, cols])
        for d in range(nb):
            has_prev = jnp.where(t == 0, 0, 1) if d == 0 else 1
            _softmax_block(d, has_prev, s_ref, biasm_ref, sinks_ref, p_ref, es_ref)
        ext = ext_ref[...]
        pos = t * tm + lax.broadcasted_iota(jnp.int32, (tm, 1), 0)
        z = (_pool_means(ext, None)[POOL_PAD:, :] / _pool_counts(pos) - ext[POOL_PAD:, :]).astype(BF16)
        pool_z = (_dot(z, w_grp_ref[...]) * scale_ref[...]).astype(BF16)
        pp = _dot(pool_z, w_pool_up_ref[...])
        for d in range(nb):
            _values_block(d, bases[d], p_ref, es_ref, ve_ref, vo_ref, ao_ref)
        a = _dot(ao_ref[...], w_attn_up_ref[...])
        half = D_MODEL // 2
        m = jnp.concatenate([sg_ref[0] * a[:, :half] + sg_ref[2] * pp[:, :half],
                             sg_ref[1] * a[:, half:] + sg_ref[3] * pp[:, half:]], axis=1)
        mo = _dot(m.astype(BF16), w_out_ref[...])
        h_ref[...] = x_ref[...] + _rms(mo, g_post_ref[...])
        return carry

    lax.fori_loop(0, 1 + jnp.minimum(t, 0), finish, 0)

    for g in range(N_KV_HEADS):
        ke_ref[g, 0:WINDOW, :] = ke_ref[g, tm:tm + WINDOW, :]
        ko_ref[g, 0:WINDOW, :] = ko_ref[g, tm:tm + WINDOW, :]
        ve_ref[g, 0:WINDOW, :] = ve_ref[g, tm:tm + WINDOW, :]
        vo_ref[g, 0:WINDOW, :] = vo_ref[g, tm:tm + WINDOW, :]
    ext_ref[0:POOL_PAD, :] = ext_ref[tm:tm + POOL_PAD, :]


def _mix_sample_kernel(x_ref, ck_ref, cv_ref, state_ref, g_pre_ref, w_in_ref, b_in_ref, table_ref,
                       sinks_ref, bucket_ref, w_grp_ref, scale_ref, w_attn_up_ref, w_pool_up_ref,
                       w_out_ref, g_post_ref,
                       h_ref, kout_ref, vout_ref, up_ref,
                       bias_ref, sink_ref, valid_ref, vfull_ref, kx_ref, vx_ref, ext_ref, qh_ref,
                       oh_ref):
    tm = SAMPLE_TILE
    ns = SEQ_PER_TILE
    T = DEC_SEQ
    R = N_HEADS * T

    @pl.when(pl.program_id(0) == 0)
    def _():
        _build_tables(T, LOG2E, bucket_ref, table_ref, sinks_ref, bias_ref, sink_ref, valid_ref)
        for j in range(N_HEADS):
            vfull_ref[j * T:(j + 1) * T, :] = valid_ref[1]
        for j in range(1, SEQ_GROUP):
            bias_ref[j * R:(j + 1) * R, :] = bias_ref[0:R, :]
            sink_ref[j * R:(j + 1) * R, :] = sink_ref[0:R, :]
            vfull_ref[j * R:(j + 1) * R, :] = vfull_ref[0:R, :]
        vx_ref[:, :, LANES:] = jnp.ones((ns, N_KEYS, LANES), BF16)

    x = x_ref[...]
    xn = _rms(x, g_pre_ref[...]).astype(BF16)
    u = _dot(xn, w_in_ref[:, 0:OFF_GA]) + b_in_ref[:, 0:OFF_GA]
    q = u[:, 0:OFF_K] * (HEAD_DIM ** -0.5 * LOG2E)
    k = u[:, OFF_K:OFF_V]
    v = u[:, OFF_V:OFF_POOL]
    up = u[:, OFF_POOL:OFF_GA]
    k3 = k.reshape(ns, T, LANES)
    v3 = v.reshape(ns, T, LANES)
    ck = ck_ref[...]
    cv = cv_ref[...]

    kout_ref[:, 0:WINDOW - T, :] = ck[:, T:, :]
    kout_ref[:, WINDOW - T:, :] = k3
    vout_ref[:, 0:WINDOW - T, :] = cv[:, T:, :]
    vout_ref[:, WINDOW - T:, :] = v3
    up_ref[...] = up

    pad = jnp.zeros((ns, WINDOW - T, LANES), F32)
    kx_ref[:, 0:WINDOW, :] = ck.astype(BF16)
    kx_ref[:, WINDOW:, :] = jnp.concatenate([k3, pad], axis=1).astype(BF16)
    vx_ref[:, 0:WINDOW, 0:LANES] = cv.astype(BF16)
    vx_ref[:, WINDOW:, 0:LANES] = jnp.concatenate([v3, pad], axis=1).astype(BF16)
    ext_ref[:, 0:POOL_PAD, :] = state_ref[...]
    ext_ref[:, POOL_PAD:, :] = up.reshape(ns, T, POOL_WIDTH)
    for h, qh in enumerate(_head_queries(q)):
        qh_ref[:, h * T:(h + 1) * T, :] = qh.reshape(ns, T, LANES)

    def group(gi, carry):
        s0 = gi * SEQ_GROUP
        s = jnp.concatenate([_dot_nt(qh_ref[s0 + j].astype(BF16), kx_ref[s0 + j])
                             for j in range(SEQ_GROUP)], axis=0)
        s = jnp.where(vfull_ref[...] > 0.5, s + bias_ref[...], NEG_INF)
        sink = sink_ref[:, 0:1]
        m = jnp.maximum(jnp.max(s, axis=-1, keepdims=True), sink)
        p = _exp2(s - m).astype(BF16)
        es = _exp2(sink - m)
        for j in range(SEQ_GROUP):
            o = _dot(p[j * R:(j + 1) * R], vx_ref[s0 + j])
            oh_ref[s0 + j] = o[:, :LANES] / (o[:, LANES:] + es[j * R:(j + 1) * R])
        return carry

    lax.fori_loop(0, ns // SEQ_GROUP, group, 0)

    o_heads = [oh_ref[:, h * T:(h + 1) * T, :].reshape(tm, LANES) for h in range(N_HEADS)]
    attn_o = _merge_heads(o_heads).astype(BF16)

    ext = ext_ref[...].reshape(ns * (POOL_PAD + T), POOL_WIDTH)
    pool_sum = _pool_means(ext, None).reshape(ns, POOL_PAD + T, POOL_WIDTH)[:, POOL_PAD:, :]
    pool_sum = pool_sum.reshape(tm, POOL_WIDTH)
    row = lax.broadcasted_iota(jnp.int32, (tm, 1), 0)
    pos = PAST_LEN + (row & (T - 1))
    h_ref[...] = _mix_tail(x, xn, attn_o, pool_sum, up, _pool_counts(pos),
                           w_in_ref, b_in_ref, w_grp_ref, scale_ref,
                           w_attn_up_ref, w_pool_up_ref, w_out_ref, g_post_ref)


def _mlp_kernel(h_ref, g_pre_ref, w_up_ref, w_down_ref, g_post_ref, y_ref):
    rows = h_ref.shape[0] // 2
    chunk = D_FF // 4
    n_chunks = D_FF // chunk
    hn = [_rms(h_ref[r * rows:(r + 1) * rows, :], g_pre_ref[...]).astype(BF16) for r in range(2)]
    f = [jnp.zeros((rows, D_MODEL), F32) for _ in range(2)]
    for c in range(n_chunks):
        for r in range(2):
            a = jnp.maximum(_dot(hn[r], w_up_ref[:, c * chunk:(c + 1) * chunk]), 0.0)
            f[r] = f[r] + _dot((a * a).astype(BF16), w_down_ref[c * chunk:(c + 1) * chunk, :])
    for r in range(2):
        y_ref[r * rows:(r + 1) * rows, :] = (h_ref[r * rows:(r + 1) * rows, :]
                                             + _rms(f[r], g_post_ref[...]))


def _full(shape):
    return pl.BlockSpec(shape, lambda *_: (0,) * len(shape), pipeline_mode=pl.Buffered(1))


def _smem():
    return pl.BlockSpec(memory_space=pltpu.SMEM)


def _rel_bucket(dist):
    d = np.maximum(dist, 0)
    max_exact = N_BUCKETS // 2
    large = max_exact + (np.log(np.maximum(d, 1).astype(np.float32) / np.float32(max_exact))
                         / np.float32(math.log(MAX_DISTANCE / max_exact))
                         * np.float32(N_BUCKETS - max_exact)).astype(np.int32)
    large = np.minimum(large, N_BUCKETS - 1)
    return np.where(d < max_exact, d, large)


def _bucket_matrix(rows):
    r = np.arange(rows)[:, None]
    j = np.arange(N_KEYS)[None, :]
    return _rel_bucket(r + WINDOW - j).astype(np.int32)


def _lane_patterns(rows):
    lo = (jnp.arange(LANES) < HEAD_DIM).astype(BF16)
    pats = jnp.stack([jnp.zeros_like(lo), lo, 1 - lo])
    return jnp.broadcast_to(pats[:, None, :], (3, rows, LANES))


def _mlp(h2d, g_pre, w_up, w_down, g_post):
    n = h2d.shape[0]
    tile = MLP_TILE
    return pl.pallas_call(
        _mlp_kernel,
        grid=(n // tile,),
        in_specs=[pl.BlockSpec((tile, D_MODEL), lambda i: (i, 0)),
                  _full((1, D_MODEL)), _full((D_MODEL, D_FF)), _full((D_FF, D_MODEL)),
                  _full((1, D_MODEL))],
        out_specs=pl.BlockSpec((tile, D_MODEL), lambda i: (i, 0)),
        out_shape=jax.ShapeDtypeStruct((n, D_MODEL), F32),
        compiler_params=pltpu.CompilerParams(dimension_semantics=("arbitrary",),
                                             vmem_limit_bytes=VMEM_LIMIT_BYTES),
        name="mlp",
    )(h2d, g_pre, w_up, w_down, g_post)


def kernel(x_prompt, x_sample, cache_k_win, cache_v_win, state_pool, norm_pre_mix, norm_post_mix,
           norm_pre_mlp, norm_post_mlp, w_in, b_in, attn_sinks, rel_bias_table, w_attn_up,
           w_pool_grp, pool_scale, w_pool_up, w_out, w_mlp_up, w_mlp_down):
    B, S, _ = x_prompt.shape
    DB, T, _ = x_sample.shape
    depth = w_in.shape[0]
    assert depth == 1 and S % TOKEN_TILE == 0 and (DB * T) % TOKEN_TILE == 0
    assert T == DEC_SEQ and (DB * T) % SAMPLE_TILE == 0
    assert GATE_CHUNKS == TOKEN_TILE // WINDOW

    l = 0
    w_in_b = w_in[l].astype(BF16)
    w_grp_b = jax.scipy.linalg.block_diag(*[w_pool_grp[l, g] for g in range(len(POOL_WINDOWS))]
                                          ).astype(BF16)
    w_attn_up_b = w_attn_up[l].astype(BF16)
    w_pool_up_b = w_pool_up[l].astype(BF16)
    w_out_b = w_out[l].astype(BF16)
    w_mlp_up_b = w_mlp_up[l].astype(BF16)
    w_mlp_down_b = w_mlp_down[l].astype(BF16)
    b_in2 = b_in[l][None, :]
    sinks2 = attn_sinks[l][None, :]
    scale2 = pool_scale[l][None, :]
    g_pre, g_post = norm_pre_mix[l][None, :], norm_post_mix[l][None, :]
    g_pre_mlp, g_post_mlp = norm_pre_mlp[l][None, :], norm_post_mlp[l][None, :]

    weight_specs = [_full((POOL_WIDTH, POOL_WIDTH)), _full((1, POOL_WIDTH)),
                    _full((ATTN_WIDTH, D_MODEL)), _full((POOL_WIDTH, D_MODEL)),
                    _full((D_MODEL, D_MODEL)), _full((1, D_MODEL))]
    weights = (w_grp_b, scale2, w_attn_up_b, w_pool_up_b, w_out_b, g_post)

    tm = TOKEN_TILE
    h_p, kwin, vwin, pstate = pl.pallas_call(
        _mix_prompt_kernel,
        grid=(B, S // tm),
        in_specs=[pl.BlockSpec((None, tm, D_MODEL), lambda b, t: (b, t, 0)),
                  _full((1, D_MODEL)), _full((D_MODEL, IN_WIDTH)), _full((1, IN_WIDTH)),
                  _smem(), _smem(), _full((WINDOW, N_KEYS)), _full((3, tm, LANES))] + weight_specs,
        out_specs=[pl.BlockSpec((None, tm, D_MODEL), lambda b, t: (b, t, 0)),
                   pl.BlockSpec((None, WINDOW, KV_WIDTH), lambda b, t: (b, 0, 0)),
                   pl.BlockSpec((None, WINDOW, KV_WIDTH), lambda b, t: (b, 0, 0)),
                   pl.BlockSpec((None, POOL_PAD, POOL_WIDTH), lambda b, t: (b, 0, 0))],
        out_shape=[jax.ShapeDtypeStruct((B, S, D_MODEL), F32),
                   jax.ShapeDtypeStruct((B, WINDOW, KV_WIDTH), F32),
                   jax.ShapeDtypeStruct((B, WINDOW, KV_WIDTH), F32),
                   jax.ShapeDtypeStruct((B, POOL_PAD, POOL_WIDTH), F32)],
        scratch_shapes=[pltpu.VMEM((2, N_HEADS * WINDOW, N_KEYS), F32),
                        pltpu.VMEM((N_KV_HEADS, WINDOW + tm, LANES), BF16),
                        pltpu.VMEM((N_KV_HEADS, WINDOW + tm, LANES), BF16),
                        pltpu.VMEM((N_KV_HEADS, WINDOW + tm, 2 * LANES), BF16),
                        pltpu.VMEM((N_KV_HEADS, WINDOW + tm, 2 * LANES), BF16),
                        pltpu.VMEM((POOL_PAD + tm, POOL_WIDTH), F32),
                        pltpu.VMEM((tm, ATTN_WIDTH), BF16),
                        pltpu.VMEM((tm, ATTN_WIDTH), BF16),
                        pltpu.VMEM((GATE_CHUNKS, tm, GATE_CHUNK), F32),
                        pltpu.VMEM((tm, D_MODEL), BF16),
                        pltpu.VMEM((BLOCKS_PER_ITER, N_KV_HEADS, 2, 2 * WINDOW, N_KEYS), F32),
                        pltpu.VMEM((BLOCKS_PER_ITER, N_KV_HEADS, 2, 2 * WINDOW, N_KEYS), BF16),
                        pltpu.VMEM((BLOCKS_PER_ITER, ATTN_WIDTH // LANES, WINDOW, LANES), F32)],
        compiler_params=pltpu.CompilerParams(dimension_semantics=("arbitrary", "arbitrary"),
                                             vmem_limit_bytes=VMEM_LIMIT_BYTES),
        name="mix_prompt",
    )(x_prompt, g_pre, w_in_b, b_in2, rel_bias_table,
      sinks2, _bucket_matrix(WINDOW), _lane_patterns(tm), *weights)
    y_p = _mlp(h_p.reshape(B * S, D_MODEL), g_pre_mlp, w_mlp_up_b, w_mlp_down_b, g_post_mlp)

    tm, ns = SAMPLE_TILE, SEQ_PER_TILE
    n_tok = DB * T
    state16 = jnp.pad(state_pool[l], ((0, 0), (POOL_PAD - POOL_STATE, 0), (0, 0)))
    ck = cache_k_win[l].reshape(DB, WINDOW, KV_WIDTH)
    cv = cache_v_win[l].reshape(DB, WINDOW, KV_WIDTH)
    h_s, k_s, v_s, up_new = pl.pallas_call(
        _mix_sample_kernel,
        grid=(n_tok // tm,),
        in_specs=[pl.BlockSpec((tm, D_MODEL), lambda i: (i, 0)),
                  pl.BlockSpec((ns, WINDOW, KV_WIDTH), lambda i: (i, 0, 0)),
                  pl.BlockSpec((ns, WINDOW, KV_WIDTH), lambda i: (i, 0, 0)),
                  pl.BlockSpec((ns, POOL_PAD, POOL_WIDTH), lambda i: (i, 0, 0)),
                  _full((1, D_MODEL)), _full((D_MODEL, IN_WIDTH)), _full((1, IN_WIDTH)),
                  _smem(), _smem(), _full((T, N_KEYS))] + weight_specs,
        out_specs=[pl.BlockSpec((tm, D_MODEL), lambda i: (i, 0)),
                   pl.BlockSpec((ns, WINDOW, KV_WIDTH), lambda i: (i, 0, 0)),
                   pl.BlockSpec((ns, WINDOW, KV_WIDTH), lambda i: (i, 0, 0)),
                   pl.BlockSpec((tm, POOL_WIDTH), lambda i: (i, 0))],
        out_shape=[jax.ShapeDtypeStruct((n_tok, D_MODEL), F32),
                   jax.ShapeDtypeStruct((DB, WINDOW, KV_WIDTH), F32),
                   jax.ShapeDtypeStruct((DB, WINDOW, KV_WIDTH), F32),
                   jax.ShapeDtypeStruct((n_tok, POOL_WIDTH), F32)],
        scratch_shapes=[pltpu.VMEM((SEQ_GROUP * N_HEADS * T, N_KEYS), F32),
                        pltpu.VMEM((SEQ_GROUP * N_HEADS * T, LANES), F32),
                        pltpu.VMEM((2, T, N_KEYS), F32),
                        pltpu.VMEM((SEQ_GROUP * N_HEADS * T, N_KEYS), F32),
                        pltpu.VMEM((ns, N_KEYS, LANES), BF16),
                        pltpu.VMEM((ns, N_KEYS, 2 * LANES), BF16),
                        pltpu.VMEM((ns, POOL_PAD + T, POOL_WIDTH), F32),
                        pltpu.VMEM((ns, N_HEADS * T, LANES), F32),
                        pltpu.VMEM((ns, N_HEADS * T, LANES), F32)],
        compiler_params=pltpu.CompilerParams(dimension_semantics=("arbitrary",),
                                             vmem_limit_bytes=VMEM_LIMIT_BYTES),
        name="mix_sample",
    )(x_sample.reshape(n_tok, D_MODEL), ck, cv, state16, g_pre, w_in_b, b_in2, rel_bias_table,
      sinks2, _bucket_matrix(T), *weights)
    y_s = _mlp(h_s, g_pre_mlp, w_mlp_up_b, w_mlp_down_b, g_post_mlp)

    kv_shape = (1, DB, WINDOW, N_KV_HEADS, HEAD_DIM)
    k_s = k_s.reshape(kv_shape)
    v_s = v_s.reshape(kv_shape)
    p_s = jnp.concatenate([state_pool[l][:, T:], up_new.reshape(DB, T, POOL_WIDTH)], axis=1)[None]

    return (y_p.reshape(B, S, D_MODEL), y_s.reshape(DB, T, D_MODEL),
            kwin.reshape(1, B, WINDOW, N_KV_HEADS, HEAD_DIM),
            vwin.reshape(1, B, WINDOW, N_KV_HEADS, HEAD_DIM),
            pstate[:, POOL_PAD - POOL_STATE:][None],
            k_s, v_s, p_s)
```

```python
import math

import jax
import jax.numpy as jnp
import numpy as np
from jax import lax
from jax.experimental import pallas as pl
from jax.experimental.pallas import tpu as pltpu

D_MODEL = 1024
N_HEADS = 8
HEAD_DIM = 64
N_KV_HEADS = 2
GROUP = N_HEADS // N_KV_HEADS
ATTN_WIDTH = N_HEADS * HEAD_DIM
KV_WIDTH = N_KV_HEADS * HEAD_DIM
WINDOW = 128
POOL_WIDTH = D_MODEL // 2
POOL_WINDOWS = (2, 4, 8, 16)
POOL_GROUP_WIDTH = POOL_WIDTH // len(POOL_WINDOWS)
POOL_STATE = max(POOL_WINDOWS) - 1
POOL_PAD = POOL_STATE + 1
D_FF = 4 * D_MODEL
N_BUCKETS = 32
MAX_DISTANCE = 128
RMS_EPS = 1e-6
LOG2E = math.log2(math.e)
PAST_LEN = 16384
IN_WIDTH = ATTN_WIDTH + 2 * KV_WIDTH + POOL_WIDTH + 2 * D_MODEL
OFF_K = ATTN_WIDTH
OFF_V = OFF_K + KV_WIDTH
OFF_POOL = OFF_V + KV_WIDTH
OFF_GA = OFF_POOL + POOL_WIDTH
OFF_GP = OFF_GA + D_MODEL
N_KEYS = 2 * WINDOW
GATE_CHUNKS = 4
GATE_CHUNK = (IN_WIDTH - OFF_GA) // GATE_CHUNKS
BLOCKS_PER_ITER = 4

LANES = 128
VMEM_LIMIT_BYTES = 56 * 1024 * 1024

TOKEN_TILE = 512
MLP_TILE = 1024
DEC_SEQ = 8
SAMPLE_TILE = 256
SEQ_PER_TILE = SAMPLE_TILE // DEC_SEQ
SEQ_GROUP = 16

BF16 = jnp.bfloat16
F32 = jnp.float32


def _rms(x, g):
    return x * lax.rsqrt(jnp.mean(x * x, axis=-1, keepdims=True) + RMS_EPS) * g


def _sigmoid(x):
    return 0.5 * jnp.tanh(0.5 * x) + 0.5


def _exp2(x):
    return jnp.exp2(x)


def _dot(a, b):
    return jnp.dot(a, b, preferred_element_type=F32)


def _dot_nt(a, b):
    return lax.dot_general(a, b, (((1,), (1,)), ((), ())), preferred_element_type=F32)


def _head_queries(q):
    lane = lax.broadcasted_iota(jnp.int32, (q.shape[0], LANES), 1)
    lo = lane < HEAD_DIM
    out = []
    for c in range(ATTN_WIDTH // LANES):
        qc = q[:, c * LANES:(c + 1) * LANES]
        qr = pltpu.roll(qc, HEAD_DIM, axis=1)
        g = (2 * c) // GROUP
        if g == 0:
            out.append(jnp.where(lo, qc, 0.0))
            out.append(jnp.where(lo, qr, 0.0))
        else:
            out.append(jnp.where(lo, 0.0, qr))
            out.append(jnp.where(lo, 0.0, qc))
    return out


def _merge_heads(o_heads):
    lane = lax.broadcasted_iota(jnp.int32, o_heads[0].shape, 1)
    lo = lane < HEAD_DIM
    chunks = []
    for c in range(ATTN_WIDTH // LANES):
        e, o = o_heads[2 * c], o_heads[2 * c + 1]
        g = (2 * c) // GROUP
        if g == 0:
            chunks.append(jnp.where(lo, e, pltpu.roll(o, HEAD_DIM, axis=1)))
        else:
            chunks.append(jnp.where(lo, pltpu.roll(e, HEAD_DIM, axis=1), o))
    return jnp.concatenate(chunks, axis=1)


def _pool_sums(ext):
    parts = []
    for gi, w in enumerate(POOL_WINDOWS):
        acc = ext[:, gi * POOL_GROUP_WIDTH:(gi + 1) * POOL_GROUP_WIDTH]
        shift = 1
        while shift < w:
            acc = acc + pltpu.roll(acc, shift, axis=0)
            shift *= 2
        parts.append(acc)
    return jnp.concatenate(parts, axis=1)


def _pool_counts(pos):
    lane = lax.broadcasted_iota(jnp.int32, (pos.shape[0], POOL_WIDTH), 1)
    w = jnp.left_shift(2, lane // POOL_GROUP_WIDTH)
    return jnp.minimum(pos + 1, w).astype(F32)


def _mix_tail(x, xn, attn_o, pool_sum, up, cnt, w_in_ref, b_in_ref, w_grp_ref, scale_ref,
              w_attn_up_ref, w_pool_up_ref, w_out_ref, g_post_ref):
    z = (pool_sum / cnt - up).astype(BF16)
    pool_z = (_dot(z, w_grp_ref[...]) * scale_ref[...]).astype(BF16)
    ga = _dot(xn, w_in_ref[:, OFF_GA:OFF_GP]) + b_in_ref[:, OFF_GA:OFF_GP]
    m = _sigmoid(ga) * _dot(attn_o, w_attn_up_ref[...])
    gp = _dot(xn, w_in_ref[:, OFF_GP:IN_WIDTH]) + b_in_ref[:, OFF_GP:IN_WIDTH]
    m = m + _sigmoid(gp) * _dot(pool_z, w_pool_up_ref[...])
    mo = _dot(m.astype(BF16), w_out_ref[...])
    return x + _rms(mo, g_post_ref[...])


def _masked_bias_tables(rows, bucket_ref, table_ref, biasm_ref):
    bucket = bucket_ref[...]
    r = lax.broadcasted_iota(jnp.int32, (rows, N_KEYS), 0)
    j = lax.broadcasted_iota(jnp.int32, (rows, N_KEYS), 1)
    cur = jnp.where(j >= WINDOW, jnp.where(j - WINDOW <= r, 1.0, 0.0), 0.0)
    prev = jnp.where(j < WINDOW, jnp.where(j > r, 1.0, 0.0), 0.0)
    for h in range(N_HEADS):
        acc = jnp.zeros(bucket.shape, F32)
        for b in range(N_BUCKETS):
            acc = jnp.where(bucket == b, table_ref[b, h] * LOG2E, acc)
        biasm_ref[0, h * rows:(h + 1) * rows, :] = jnp.where(cur > 0.5, acc, -jnp.inf)
        biasm_ref[1, h * rows:(h + 1) * rows, :] = jnp.where(cur + prev > 0.5, acc, -jnp.inf)


def _scores_block(i, base, q_ref, ke_ref, ko_ref, s_ref):
    rows = pl.ds(base, WINDOW)
    keys = pl.ds(base, N_KEYS)
    for g in range(N_KV_HEADS):
        qg = jnp.concatenate([q_ref[rows, (2 * g) * LANES:(2 * g + 1) * LANES],
                              q_ref[rows, (2 * g + 1) * LANES:(2 * g + 2) * LANES]], axis=0)
        s_ref[i, g, 0] = _dot_nt(qg, ke_ref[g, keys, :])
        s_ref[i, g, 1] = _dot_nt(qg, ko_ref[g, keys, :])


def _softmax_block(i, has_prev, s_ref, biasm_ref, sinks_ref, p_ref, es_ref):
    lo = lax.broadcasted_iota(jnp.int32, (WINDOW, LANES), 1) < HEAD_DIM
    for g in range(N_KV_HEADS):
        for a in range(2):
            half = slice(a * WINDOW, (a + 1) * WINDOW)
            es = []
            for b in range(2):
                h = GROUP * g + 2 * a + b
                s = s_ref[i, g, b, half, :] + biasm_ref[has_prev, h * WINDOW:(h + 1) * WINDOW, :]
                sink = sinks_ref[0, h] * LOG2E
                m = jnp.maximum(jnp.max(s, axis=-1, keepdims=True), sink)
                p_ref[i, g, b, half, :] = _exp2(s - m).astype(BF16)
                es.append(_exp2(sink - m))
            es_ref[i, 2 * g + a] = jnp.where(lo, es[0], es[1])


def _values_block(i, base, p_ref, es_ref, ve_ref, vo_ref, ao_ref):
    rows = pl.ds(base, WINDOW)
    keys = pl.ds(base, N_KEYS)
    for g in range(N_KV_HEADS):
        o_ext = _dot(p_ref[i, g, 0], ve_ref[g, keys, :]) + _dot(p_ref[i, g, 1], vo_ref[g, keys, :])
        for a in range(2):
            c = 2 * g + a
            o = o_ext[a * WINDOW:(a + 1) * WINDOW]
            ao_ref[rows, c * LANES:(c + 1) * LANES] = (
                o[:, :LANES] / (o[:, LANES:] + es_ref[i, c])).astype(BF16)


def _mix_prompt_kernel(x_ref, g_pre_ref, w_in_ref, b_in_ref, table_ref,
                       sinks_ref, bucket_ref, pat_ref,
                       w_grp_ref, scale_ref, w_attn_up_ref, w_pool_up_ref, w_out_ref, g_post_ref,
                       h_ref, kwin_ref, vwin_ref, pstate_ref,
                       biasm_ref, ke_ref, ko_ref, ve_ref, vo_ref, ext_ref, q_ref, ao_ref,
                       sg_ref, xn_ref, s_ref, p_ref, es_ref):
    tm = TOKEN_TILE
    t = pl.program_id(1)
    lane = lax.broadcasted_iota(jnp.int32, (tm, LANES), 1)
    lo = lane < HEAD_DIM

    ones_lo, ones_hi = pat_ref[1], pat_ref[2]

    @pl.when((pl.program_id(0) == 0) & (t == 0))
    def _():
        _masked_bias_tables(WINDOW, bucket_ref, table_ref, biasm_ref)

    @pl.when(t == 0)
    def _():
        zeros = pat_ref[0, 0:WINDOW, :]
        for g in range(N_KV_HEADS):
            ke_ref[g, 0:WINDOW, :] = zeros
            ko_ref[g, 0:WINDOW, :] = zeros
            ve_ref[g, 0:WINDOW, :] = jnp.concatenate([zeros, pat_ref[1, 0:WINDOW, :]], axis=1)
            vo_ref[g, 0:WINDOW, :] = jnp.concatenate([zeros, pat_ref[2, 0:WINDOW, :]], axis=1)
        ext_ref[0:POOL_PAD, :] = jnp.zeros((POOL_PAD, POOL_WIDTH), F32)

    us = []
    for r in range(2):
        rows = slice(r * (tm // 2), (r + 1) * (tm // 2))
        xn = _rms(x_ref[rows, :], g_pre_ref[...]).astype(BF16)
        xn_ref[rows, :] = xn
        us.append(_dot(xn, w_in_ref[:, 0:OFF_GA]) + b_in_ref[:, 0:OFF_GA])
    u = jnp.concatenate(us, axis=0)
    k = u[:, OFF_K:OFF_V]
    v = u[:, OFF_V:OFF_POOL]
    up = u[:, OFF_POOL:OFF_GA]

    q_ref[...] = (u[:, 0:OFF_K] * (HEAD_DIM ** -0.5 * LOG2E)).astype(BF16)
    kwin_ref[...] = k[tm - WINDOW:, :]
    vwin_ref[...] = v[tm - WINDOW:, :]
    pstate_ref[...] = up[tm - POOL_PAD:, :]
    ext_ref[POOL_PAD:, :] = up
    kr = pltpu.roll(k, HEAD_DIM, axis=1)
    vr = pltpu.roll(v, HEAD_DIM, axis=1)
    ke_ref[0, WINDOW:, :] = jnp.where(lo, k, 0.0).astype(BF16)
    ko_ref[0, WINDOW:, :] = jnp.where(lo, 0.0, kr).astype(BF16)
    ke_ref[1, WINDOW:, :] = jnp.where(lo, kr, 0.0).astype(BF16)
    ko_ref[1, WINDOW:, :] = jnp.where(lo, 0.0, k).astype(BF16)
    ve_ref[0, WINDOW:, :] = jnp.concatenate([jnp.where(lo, v, 0.0).astype(BF16), ones_lo], axis=1)
    vo_ref[0, WINDOW:, :] = jnp.concatenate([jnp.where(lo, 0.0, vr).astype(BF16), ones_hi], axis=1)
    ve_ref[1, WINDOW:, :] = jnp.concatenate([jnp.where(lo, vr, 0.0).astype(BF16), ones_lo], axis=1)
    vo_ref[1, WINDOW:, :] = jnp.concatenate([jnp.where(lo, 0.0, v).astype(BF16), ones_hi], axis=1)

    nb = BLOCKS_PER_ITER
    assert nb == GATE_CHUNKS

    def finish(_, carry):
        bases = [d * WINDOW for d in range(nb)]
        for d in range(nb):
            _scores_block(d, bases[d], q_ref, ke_ref, ko_ref, s_ref)
        for d in range(nb):
            cols = slice(OFF_GA + d * GATE_CHUNK, OFF_GA + (d + 1) * GATE_CHUNK)
            sg_ref[d] = _sigmoid(_dot(xn_ref[...], w_in_ref[:, cols]) + b_in_ref[:, cols])
        for d in range(nb):
            has_prev = jnp.where(t == 0, 0, 1) if d == 0 else 1
            _softmax_block(d, has_prev, s_ref, biasm_ref, sinks_ref, p_ref, es_ref)
        ext = ext_ref[...]
        pos = t * tm + lax.broadcasted_iota(jnp.int32, (tm, 1), 0)
        z = (_pool_sums(ext)[POOL_PAD:, :] / _pool_counts(pos) - ext[POOL_PAD:, :]).astype(BF16)
        pool_z = (_dot(z, w_grp_ref[...]) * scale_ref[...]).astype(BF16)
        pp = _dot(pool_z, w_pool_up_ref[...])
        for d in range(nb):
            _values_block(d, bases[d], p_ref, es_ref, ve_ref, vo_ref, ao_ref)
        a = _dot(ao_ref[...], w_attn_up_ref[...])
        half = D_MODEL // 2
        m = jnp.concatenate([sg_ref[0] * a[:, :half] + sg_ref[2] * pp[:, :half],
                             sg_ref[1] * a[:, half:] + sg_ref[3] * pp[:, half:]], axis=1)
        mo = _dot(m.astype(BF16), w_out_ref[...])
        h_ref[...] = x_ref[...] + _rms(mo, g_post_ref[...])
        return carry

    lax.fori_loop(0, 1 + jnp.minimum(t, 0), finish, 0)

    for g in range(N_KV_HEADS):
        ke_ref[g, 0:WINDOW, :] = ke_ref[g, tm:tm + WINDOW, :]
        ko_ref[g, 0:WINDOW, :] = ko_ref[g, tm:tm + WINDOW, :]
        ve_ref[g, 0:WINDOW, :] = ve_ref[g, tm:tm + WINDOW, :]
        vo_ref[g, 0:WINDOW, :] = vo_ref[g, tm:tm + WINDOW, :]
    ext_ref[0:POOL_PAD, :] = ext_ref[tm:tm + POOL_PAD, :]


def _mix_sample_kernel(x_ref, ck_ref, cv_ref, state_ref, g_pre_ref, w_in_ref, b_in_ref, table_ref,
                       sinks_ref, bucket_ref, w_grp_ref, scale_ref, w_attn_up_ref, w_pool_up_ref,
                       w_out_ref, g_post_ref,
                       h_ref, kout_ref, vout_ref, up_ref,
                       biasm_ref, bias_ref, sink_ref, kx_ref, vx_ref, ext_ref, qh_ref, oh_ref):
    tm = SAMPLE_TILE
    ns = SEQ_PER_TILE
    T = DEC_SEQ
    R = N_HEADS * T

    @pl.when(pl.program_id(0) == 0)
    def _():
        _masked_bias_tables(T, bucket_ref, table_ref, biasm_ref)
        for h in range(N_HEADS):
            sink_ref[h * T:(h + 1) * T, :] = jnp.full((T, LANES), sinks_ref[0, h] * LOG2E, F32)
        for j in range(SEQ_GROUP):
            bias_ref[j * R:(j + 1) * R, :] = biasm_ref[1]
            if j > 0:
                sink_ref[j * R:(j + 1) * R, :] = sink_ref[0:R, :]
        vx_ref[:, :, LANES:] = jnp.ones((ns, N_KEYS, LANES), BF16)

    x = x_ref[...]
    xn = _rms(x, g_pre_ref[...]).astype(BF16)
    u = _dot(xn, w_in_ref[:, 0:OFF_GA]) + b_in_ref[:, 0:OFF_GA]
    q = u[:, 0:OFF_K] * (HEAD_DIM ** -0.5 * LOG2E)
    k = u[:, OFF_K:OFF_V]
    v = u[:, OFF_V:OFF_POOL]
    up = u[:, OFF_POOL:OFF_GA]
    k3 = k.reshape(ns, T, LANES)
    v3 = v.reshape(ns, T, LANES)
    ck = ck_ref[...]
    cv = cv_ref[...]

    kout_ref[:, 0:WINDOW - T, :] = ck[:, T:, :]
    kout_ref[:, WINDOW - T:, :] = k3
    vout_ref[:, 0:WINDOW - T, :] = cv[:, T:, :]
    vout_ref[:, WINDOW - T:, :] = v3
    up_ref[...] = up

    pad = jnp.zeros((ns, WINDOW - T, LANES), F32)
    kx_ref[:, 0:WINDOW, :] = ck.astype(BF16)
    kx_ref[:, WINDOW:, :] = jnp.concatenate([k3, pad], axis=1).astype(BF16)
    vx_ref[:, 0:WINDOW, 0:LANES] = cv.astype(BF16)
    vx_ref[:, WINDOW:, 0:LANES] = jnp.concatenate([v3, pad], axis=1).astype(BF16)
    ext_ref[:, 0:POOL_PAD, :] = state_ref[...]
    ext_ref[:, POOL_PAD:, :] = up.reshape(ns, T, POOL_WIDTH)
    for h, qh in enumerate(_head_queries(q)):
        qh_ref[:, h * T:(h + 1) * T, :] = qh.reshape(ns, T, LANES)

    def group(gi, carry):
        s0 = gi * SEQ_GROUP
        s = jnp.concatenate([_dot_nt(qh_ref[s0 + j].astype(BF16), kx_ref[s0 + j])
                             for j in range(SEQ_GROUP)], axis=0)
        s = s + bias_ref[...]
        sink = sink_ref[:, 0:1]
        m = jnp.maximum(jnp.max(s, axis=-1, keepdims=True), sink)
        p = _exp2(s - m).astype(BF16)
        es = _exp2(sink - m)
        for j in range(SEQ_GROUP):
            o = _dot(p[j * R:(j + 1) * R], vx_ref[s0 + j])
            oh_ref[s0 + j] = o[:, :LANES] / (o[:, LANES:] + es[j * R:(j + 1) * R])
        return carry

    lax.fori_loop(0, ns // SEQ_GROUP, group, 0)

    o_heads = [oh_ref[:, h * T:(h + 1) * T, :].reshape(tm, LANES) for h in range(N_HEADS)]
    attn_o = _merge_heads(o_heads).astype(BF16)

    ext = ext_ref[...].reshape(ns * (POOL_PAD + T), POOL_WIDTH)
    pool_sum = _pool_sums(ext).reshape(ns, POOL_PAD + T, POOL_WIDTH)[:, POOL_PAD:, :]
    pool_sum = pool_sum.reshape(tm, POOL_WIDTH)
    row = lax.broadcasted_iota(jnp.int32, (tm, 1), 0)
    pos = PAST_LEN + (row & (T - 1))
    h_ref[...] = _mix_tail(x, xn, attn_o, pool_sum, up, _pool_counts(pos),
                           w_in_ref, b_in_ref, w_grp_ref, scale_ref,
                           w_attn_up_ref, w_pool_up_ref, w_out_ref, g_post_ref)


def _mlp_kernel(h_ref, g_pre_ref, w_up_ref, w_down_ref, g_post_ref, y_ref):
    rows = h_ref.shape[0] // 2
    chunk = D_FF // 4
    n_chunks = D_FF // chunk
    hn = [_rms(h_ref[r * rows:(r + 1) * rows, :], g_pre_ref[...]).astype(BF16) for r in range(2)]
    f = [jnp.zeros((rows, D_MODEL), F32) for _ in range(2)]
    for c in range(n_chunks):
        for r in range(2):
            a = jnp.maximum(_dot(hn[r], w_up_ref[:, c * chunk:(c + 1) * chunk]), 0.0)
            f[r] = f[r] + _dot((a * a).astype(BF16), w_down_ref[c * chunk:(c + 1) * chunk, :])
    for r in range(2):
        y_ref[r * rows:(r + 1) * rows, :] = (h_ref[r * rows:(r + 1) * rows, :]
                                             + _rms(f[r], g_post_ref[...]))


def _full(shape):
    return pl.BlockSpec(shape, lambda *_: (0,) * len(shape), pipeline_mode=pl.Buffered(1))


def _smem():
    return pl.BlockSpec(memory_space=pltpu.SMEM)


def _rel_bucket(dist):
    d = np.maximum(dist, 0)
    max_exact = N_BUCKETS // 2
    large = max_exact + (np.log(np.maximum(d, 1).astype(np.float32) / np.float32(max_exact))
                         / np.float32(math.log(MAX_DISTANCE / max_exact))
                         * np.float32(N_BUCKETS - max_exact)).astype(np.int32)
    large = np.minimum(large, N_BUCKETS - 1)
    return np.where(d < max_exact, d, large)


def _bucket_matrix(rows):
    r = np.arange(rows)[:, None]
    j = np.arange(N_KEYS)[None, :]
    return _rel_bucket(r + WINDOW - j).astype(np.int32)


def _lane_patterns(rows):
    lo = (jnp.arange(LANES) < HEAD_DIM).astype(BF16)
    pats = jnp.stack([jnp.zeros_like(lo), lo, 1 - lo])
    return jnp.broadcast_to(pats[:, None, :], (3, rows, LANES))


def _mlp(h2d, g_pre, w_up, w_down, g_post):
    n = h2d.shape[0]
    tile = MLP_TILE
    return pl.pallas_call(
        _mlp_kernel,
        grid=(n // tile,),
        in_specs=[pl.BlockSpec((tile, D_MODEL), lambda i: (i, 0)),
                  _full((1, D_MODEL)), _full((D_MODEL, D_FF)), _full((D_FF, D_MODEL)),
                  _full((1, D_MODEL))],
        out_specs=pl.BlockSpec((tile, D_MODEL), lambda i: (i, 0)),
        out_shape=jax.ShapeDtypeStruct((n, D_MODEL), F32),
        compiler_params=pltpu.CompilerParams(dimension_semantics=("arbitrary",),
                                             vmem_limit_bytes=VMEM_LIMIT_BYTES),
        name="mlp",
    )(h2d, g_pre, w_up, w_down, g_post)


def kernel(x_prompt, x_sample, cache_k_win, cache_v_win, state_pool, norm_pre_mix, norm_post_mix,
           norm_pre_mlp, norm_post_mlp, w_in, b_in, attn_sinks, rel_bias_table, w_attn_up,
           w_pool_grp, pool_scale, w_pool_up, w_out, w_mlp_up, w_mlp_down):
    B, S, _ = x_prompt.shape
    DB, T, _ = x_sample.shape
    depth = w_in.shape[0]
    assert depth == 1 and S % TOKEN_TILE == 0 and (DB * T) % TOKEN_TILE == 0
    assert T == DEC_SEQ and (DB * T) % SAMPLE_TILE == 0
    assert GATE_CHUNKS == TOKEN_TILE // WINDOW

    l = 0
    w_in_b = w_in[l].astype(BF16)
    w_grp_b = jax.scipy.linalg.block_diag(*[w_pool_grp[l, g] for g in range(len(POOL_WINDOWS))]
                                          ).astype(BF16)
    w_attn_up_b = w_attn_up[l].astype(BF16)
    w_pool_up_b = w_pool_up[l].astype(BF16)
    w_out_b = w_out[l].astype(BF16)
    w_mlp_up_b = w_mlp_up[l].astype(BF16)
    w_mlp_down_b = w_mlp_down[l].astype(BF16)
    b_in2 = b_in[l][None, :]
    sinks2 = attn_sinks[l][None, :]
    scale2 = pool_scale[l][None, :]
    g_pre, g_post = norm_pre_mix[l][None, :], norm_post_mix[l][None, :]
    g_pre_mlp, g_post_mlp = norm_pre_mlp[l][None, :], norm_post_mlp[l][None, :]

    weight_specs = [_full((POOL_WIDTH, POOL_WIDTH)), _full((1, POOL_WIDTH)),
                    _full((ATTN_WIDTH, D_MODEL)), _full((POOL_WIDTH, D_MODEL)),
                    _full((D_MODEL, D_MODEL)), _full((1, D_MODEL))]
    weights = (w_grp_b, scale2, w_attn_up_b, w_pool_up_b, w_out_b, g_post)

    tm = TOKEN_TILE
    h_p, kwin, vwin, pstate = pl.pallas_call(
        _mix_prompt_kernel,
        grid=(B, S // tm),
        in_specs=[pl.BlockSpec((None, tm, D_MODEL), lambda b, t: (b, t, 0)),
                  _full((1, D_MODEL)), _full((D_MODEL, IN_WIDTH)), _full((1, IN_WIDTH)),
                  _smem(), _smem(), _full((WINDOW, N_KEYS)), _full((3, tm, LANES))] + weight_specs,
        out_specs=[pl.BlockSpec((None, tm, D_MODEL), lambda b, t: (b, t, 0)),
                   pl.BlockSpec((None, WINDOW, KV_WIDTH), lambda b, t: (b, 0, 0)),
                   pl.BlockSpec((None, WINDOW, KV_WIDTH), lambda b, t: (b, 0, 0)),
                   pl.BlockSpec((None, POOL_PAD, POOL_WIDTH), lambda b, t: (b, 0, 0))],
        out_shape=[jax.ShapeDtypeStruct((B, S, D_MODEL), F32),
                   jax.ShapeDtypeStruct((B, WINDOW, KV_WIDTH), F32),
                   jax.ShapeDtypeStruct((B, WINDOW, KV_WIDTH), F32),
                   jax.ShapeDtypeStruct((B, POOL_PAD, POOL_WIDTH), F32)],
        scratch_shapes=[pltpu.VMEM((2, N_HEADS * WINDOW, N_KEYS), F32),
                        pltpu.VMEM((N_KV_HEADS, WINDOW + tm, LANES), BF16),
                        pltpu.VMEM((N_KV_HEADS, WINDOW + tm, LANES), BF16),
                        pltpu.VMEM((N_KV_HEADS, WINDOW + tm, 2 * LANES), BF16),
                        pltpu.VMEM((N_KV_HEADS, WINDOW + tm, 2 * LANES), BF16),
                        pltpu.VMEM((POOL_PAD + tm, POOL_WIDTH), F32),
                        pltpu.VMEM((tm, ATTN_WIDTH), BF16),
                        pltpu.VMEM((tm, ATTN_WIDTH), BF16),
                        pltpu.VMEM((GATE_CHUNKS, tm, GATE_CHUNK), F32),
                        pltpu.VMEM((tm, D_MODEL), BF16),
                        pltpu.VMEM((BLOCKS_PER_ITER, N_KV_HEADS, 2, 2 * WINDOW, N_KEYS), F32),
                        pltpu.VMEM((BLOCKS_PER_ITER, N_KV_HEADS, 2, 2 * WINDOW, N_KEYS), BF16),
                        pltpu.VMEM((BLOCKS_PER_ITER, ATTN_WIDTH // LANES, WINDOW, LANES), F32)],
        compiler_params=pltpu.CompilerParams(dimension_semantics=("arbitrary", "arbitrary"),
                                             vmem_limit_bytes=VMEM_LIMIT_BYTES),
        name="mix_prompt",
    )(x_prompt, g_pre, w_in_b, b_in2, rel_bias_table,
      sinks2, _bucket_matrix(WINDOW), _lane_patterns(tm), *weights)
    y_p = _mlp(h_p.reshape(B * S, D_MODEL), g_pre_mlp, w_mlp_up_b, w_mlp_down_b, g_post_mlp)

    tm, ns = SAMPLE_TILE, SEQ_PER_TILE
    n_tok = DB * T
    state16 = jnp.pad(state_pool[l], ((0, 0), (POOL_PAD - POOL_STATE, 0), (0, 0)))
    ck = cache_k_win[l].reshape(DB, WINDOW, KV_WIDTH)
    cv = cache_v_win[l].reshape(DB, WINDOW, KV_WIDTH)
    h_s, k_s, v_s, up_new = pl.pallas_call(
        _mix_sample_kernel,
        grid=(n_tok // tm,),
        in_specs=[pl.BlockSpec((tm, D_MODEL), lambda i: (i, 0)),
                  pl.BlockSpec((ns, WINDOW, KV_WIDTH), lambda i: (i, 0, 0)),
                  pl.BlockSpec((ns, WINDOW, KV_WIDTH), lambda i: (i, 0, 0)),
                  pl.BlockSpec((ns, POOL_PAD, POOL_WIDTH), lambda i: (i, 0, 0)),
                  _full((1, D_MODEL)), _full((D_MODEL, IN_WIDTH)), _full((1, IN_WIDTH)),
                  _smem(), _smem(), _full((T, N_KEYS))] + weight_specs,
        out_specs=[pl.BlockSpec((tm, D_MODEL), lambda i: (i, 0)),
                   pl.BlockSpec((ns, WINDOW, KV_WIDTH), lambda i: (i, 0, 0)),
                   pl.BlockSpec((ns, WINDOW, KV_WIDTH), lambda i: (i, 0, 0)),
                   pl.BlockSpec((tm, POOL_WIDTH), lambda i: (i, 0))],
        out_shape=[jax.ShapeDtypeStruct((n_tok, D_MODEL), F32),
                   jax.ShapeDtypeStruct((DB, WINDOW, KV_WIDTH), F32),
                   jax.ShapeDtypeStruct((DB, WINDOW, KV_WIDTH), F32),
                   jax.ShapeDtypeStruct((n_tok, POOL_WIDTH), F32)],
        scratch_shapes=[pltpu.VMEM((2, N_HEADS * T, N_KEYS), F32),
                        pltpu.VMEM((SEQ_GROUP * N_HEADS * T, N_KEYS), F32),
                        pltpu.VMEM((SEQ_GROUP * N_HEADS * T, LANES), F32),
                        pltpu.VMEM((ns, N_KEYS, LANES), BF16),
                        pltpu.VMEM((ns, N_KEYS, 2 * LANES), BF16),
                        pltpu.VMEM((ns, POOL_PAD + T, POOL_WIDTH), F32),
                        pltpu.VMEM((ns, N_HEADS * T, LANES), F32),
                        pltpu.VMEM((ns, N_HEADS * T, LANES), F32)],
        compiler_params=pltpu.CompilerParams(dimension_semantics=("arbitrary",),
                                             vmem_limit_bytes=VMEM_LIMIT_BYTES),
        name="mix_sample",
    )(x_sample.reshape(n_tok, D_MODEL), ck, cv, state16, g_pre, w_in_b, b_in2, rel_bias_table,
      sinks2, _bucket_matrix(T), *weights)
    y_s = _mlp(h_s, g_pre_mlp, w_mlp_up_b, w_mlp_down_b, g_post_mlp)

    kv_shape = (1, DB, WINDOW, N_KV_HEADS, HEAD_DIM)
    k_s = k_s.reshape(kv_shape)
    v_s = v_s.reshape(kv_shape)
    p_s = jnp.concatenate([state_pool[l][:, T:], up_new.reshape(DB, T, POOL_WIDTH)], axis=1)[None]

    return (y_p.reshape(B, S, D_MODEL), y_s.reshape(DB, T, D_MODEL),
            kwin.reshape(1, B, WINDOW, N_KV_HEADS, HEAD_DIM),
            vwin.reshape(1, B, WINDOW, N_KV_HEADS, HEAD_DIM),
            pstate[:, POOL_PAD - POOL_STATE:][None],
            k_s, v_s, p_s)
```

```python
import math

import jax
import jax.numpy as jnp
import numpy as np
from jax import lax
from jax.experimental import pallas as pl
from jax.experimental.pallas import tpu as pltpu

D_MODEL = 1024
N_HEADS = 8
HEAD_DIM = 64
N_KV_HEADS = 2
GROUP = N_HEADS // N_KV_HEADS
ATTN_WIDTH = N_HEADS * HEAD_DIM
KV_WIDTH = N_KV_HEADS * HEAD_DIM
WINDOW = 128
POOL_WIDTH = D_MODEL // 2
POOL_WINDOWS = (2, 4, 8, 16)
assert POOL_WINDOWS == tuple(2 << g for g in range(len(POOL_WINDOWS)))
POOL_GROUP_WIDTH = POOL_WIDTH // len(POOL_WINDOWS)
POOL_STATE = max(POOL_WINDOWS) - 1
POOL_PAD = POOL_STATE + 1
D_FF = 4 * D_MODEL
N_BUCKETS = 32
MAX_DISTANCE = 128
RMS_EPS = 1e-6
LOG2E = math.log2(math.e)
PAST_LEN = 16384
IN_WIDTH = ATTN_WIDTH + 2 * KV_WIDTH + POOL_WIDTH + 2 * D_MODEL
OFF_K = ATTN_WIDTH
OFF_V = OFF_K + KV_WIDTH
OFF_POOL = OFF_V + KV_WIDTH
OFF_GA = OFF_POOL + POOL_WIDTH
OFF_GP = OFF_GA + D_MODEL
N_KEYS = 2 * WINDOW
GATE_CHUNKS = 4
GATE_CHUNK = (IN_WIDTH - OFF_GA) // GATE_CHUNKS
BLOCKS_PER_ITER = 4

LANES = 128
VMEM_LIMIT_BYTES = 56 * 1024 * 1024

TOKEN_TILE = 512
MLP_TILE = 1024
DEC_SEQ = 8
SAMPLE_TILE = 256
SEQ_PER_TILE = SAMPLE_TILE // DEC_SEQ
SEQ_GROUP = 16

BF16 = jnp.bfloat16
F32 = jnp.float32


def _rms(x, g):
    return x * lax.rsqrt(jnp.mean(x * x, axis=-1, keepdims=True) + RMS_EPS) * g


def _sigmoid(x):
    return 0.5 * jnp.tanh(0.5 * x) + 0.5


def _dot(a, b):
    return jnp.dot(a, b, preferred_element_type=F32)


def _dot_nt(a, b):
    return lax.dot_general(a, b, (((1,), (1,)), ((), ())), preferred_element_type=F32)


def _head_queries(q):
    lane = lax.broadcasted_iota(jnp.int32, (q.shape[0], LANES), 1)
    lo = lane < HEAD_DIM
    out = []
    for c in range(ATTN_WIDTH // LANES):
        qc = q[:, c * LANES:(c + 1) * LANES]
        qr = pltpu.roll(qc, HEAD_DIM, axis=1)
        g = (2 * c) // GROUP
        if g == 0:
            out.append(jnp.where(lo, qc, 0.0))
            out.append(jnp.where(lo, qr, 0.0))
        else:
            out.append(jnp.where(lo, 0.0, qr))
            out.append(jnp.where(lo, 0.0, qc))
    return out


def _merge_heads(o_heads):
    lane = lax.broadcasted_iota(jnp.int32, o_heads[0].shape, 1)
    lo = lane < HEAD_DIM
    chunks = []
    for c in range(ATTN_WIDTH // LANES):
        e, o = o_heads[2 * c], o_heads[2 * c + 1]
        g = (2 * c) // GROUP
        if g == 0:
            chunks.append(jnp.where(lo, e, pltpu.roll(o, HEAD_DIM, axis=1)))
        else:
            chunks.append(jnp.where(lo, pltpu.roll(e, HEAD_DIM, axis=1), o))
    return jnp.concatenate(chunks, axis=1)


def _pool_sums(ext):
    parts = []
    for gi, w in enumerate(POOL_WINDOWS):
        acc = ext[:, gi * POOL_GROUP_WIDTH:(gi + 1) * POOL_GROUP_WIDTH]
        shift = 1
        while shift < w:
            acc = acc + pltpu.roll(acc, shift, axis=0)
            shift *= 2
        parts.append(acc)
    return jnp.concatenate(parts, axis=1)


def _pool_counts(pos):
    lane = lax.broadcasted_iota(jnp.int32, (pos.shape[0], POOL_WIDTH), 1)
    w = jnp.left_shift(2, lane // POOL_GROUP_WIDTH)
    return jnp.minimum(pos + 1, w).astype(F32)


def _mix_tail(x, xn, attn_o, pool_sum, up, cnt, w_in_ref, b_in_ref, w_grp_ref, scale_ref,
              w_attn_up_ref, w_pool_up_ref, w_out_ref, g_post_ref):
    z = (pool_sum / cnt - up).astype(BF16)
    pool_z = (_dot(z, w_grp_ref[...]) * scale_ref[...]).astype(BF16)
    ga = _dot(xn, w_in_ref[:, OFF_GA:OFF_GP]) + b_in_ref[:, OFF_GA:OFF_GP]
    m = _sigmoid(ga) * _dot(attn_o, w_attn_up_ref[...])
    gp = _dot(xn, w_in_ref[:, OFF_GP:IN_WIDTH]) + b_in_ref[:, OFF_GP:IN_WIDTH]
    m = m + _sigmoid(gp) * _dot(pool_z, w_pool_up_ref[...])
    mo = _dot(m.astype(BF16), w_out_ref[...])
    return x + _rms(mo, g_post_ref[...])


def _masked_bias_tables(rows, bucket_ref, table_ref, biasm_ref):
    bucket = bucket_ref[...]
    r = lax.broadcasted_iota(jnp.int32, (rows, N_KEYS), 0)
    j = lax.broadcasted_iota(jnp.int32, (rows, N_KEYS), 1)
    cur = jnp.where(j >= WINDOW, jnp.where(j - WINDOW <= r, 1.0, 0.0), 0.0)
    prev = jnp.where(j < WINDOW, jnp.where(j > r, 1.0, 0.0), 0.0)
    for h in range(N_HEADS):
        acc = jnp.zeros(bucket.shape, F32)
        for b in range(N_BUCKETS):
            acc = jnp.where(bucket == b, table_ref[b, h] * LOG2E, acc)
        biasm_ref[0, h * rows:(h + 1) * rows, :] = jnp.where(cur > 0.5, acc, -jnp.inf)
        biasm_ref[1, h * rows:(h + 1) * rows, :] = jnp.where(cur + prev > 0.5, acc, -jnp.inf)


def _scores_block(i, base, q_ref, ke_ref, ko_ref, s_ref):
    rows = pl.ds(base, WINDOW)
    keys = pl.ds(base, N_KEYS)
    for g in range(N_KV_HEADS):
        qg = jnp.concatenate([q_ref[rows, (2 * g) * LANES:(2 * g + 1) * LANES],
                              q_ref[rows, (2 * g + 1) * LANES:(2 * g + 2) * LANES]], axis=0)
        s_ref[i, g, 0] = _dot_nt(qg, ke_ref[g, keys, :])
        s_ref[i, g, 1] = _dot_nt(qg, ko_ref[g, keys, :])


def _softmax_block(i, has_prev, s_ref, biasm_ref, sinks_ref, p_ref, es_ref):
    lo = lax.broadcasted_iota(jnp.int32, (WINDOW, LANES), 1) < HEAD_DIM
    for g in range(N_KV_HEADS):
        for a in range(2):
            half = slice(a * WINDOW, (a + 1) * WINDOW)
            es = []
            for b in range(2):
                h = GROUP * g + 2 * a + b
                s = s_ref[i, g, b, half, :] + biasm_ref[has_prev, h * WINDOW:(h + 1) * WINDOW, :]
                sink = sinks_ref[0, h] * LOG2E
                m = jnp.maximum(jnp.max(s, axis=-1, keepdims=True), sink)
                p_ref[i, g, b, half, :] = jnp.exp2(s - m).astype(BF16)
                es.append(jnp.exp2(sink - m))
            es_ref[i, 2 * g + a] = jnp.where(lo, es[0], es[1])


def _values_block(i, base, p_ref, es_ref, ve_ref, vo_ref, ao_ref):
    rows = pl.ds(base, WINDOW)
    keys = pl.ds(base, N_KEYS)
    for g in range(N_KV_HEADS):
        o_ext = _dot(p_ref[i, g, 0], ve_ref[g, keys, :]) + _dot(p_ref[i, g, 1], vo_ref[g, keys, :])
        for a in range(2):
            c = 2 * g + a
            o = o_ext[a * WINDOW:(a + 1) * WINDOW]
            ao_ref[rows, c * LANES:(c + 1) * LANES] = (
                o[:, :LANES] / (o[:, LANES:] + es_ref[i, c])).astype(BF16)


def _mix_prompt_kernel(x_ref, g_pre_ref, w_in_ref, b_in_ref, table_ref,
                       sinks_ref, bucket_ref, pat_ref,
                       w_grp_ref, scale_ref, w_attn_up_ref, w_pool_up_ref, w_out_ref, g_post_ref,
                       h_ref, kwin_ref, vwin_ref, pstate_ref,
                       biasm_ref, ke_ref, ko_ref, ve_ref, vo_ref, ext_ref, q_ref, ao_ref,
                       sg_ref, xn_ref, s_ref, p_ref, es_ref):
    tm = TOKEN_TILE
    t = pl.program_id(1)

    @pl.when((pl.program_id(0) == 0) & (t == 0))
    def _():
        _masked_bias_tables(WINDOW, bucket_ref, table_ref, biasm_ref)

    @pl.when(t == 0)
    def _():
        zeros = pat_ref[0, 0:WINDOW, :]
        for g in range(N_KV_HEADS):
            ke_ref[g, 0:WINDOW, :] = zeros
            ko_ref[g, 0:WINDOW, :] = zeros
            ve_ref[g, 0:WINDOW, :] = jnp.concatenate([zeros, pat_ref[1, 0:WINDOW, :]], axis=1)
            vo_ref[g, 0:WINDOW, :] = jnp.concatenate([zeros, pat_ref[2, 0:WINDOW, :]], axis=1)
        ext_ref[0:POOL_PAD, :] = jnp.zeros((POOL_PAD, POOL_WIDTH), F32)

    hr = tm // 2
    lo = lax.broadcasted_iota(jnp.int32, (hr, LANES), 1) < HEAD_DIM
    ones_lo, ones_hi = pat_ref[1, 0:hr, :], pat_ref[2, 0:hr, :]
    for r in range(2):
        rows = slice(r * hr, (r + 1) * hr)
        kv_rows = slice(WINDOW + r * hr, WINDOW + (r + 1) * hr)
        xn = _rms(x_ref[rows, :], g_pre_ref[...]).astype(BF16)
        xn_ref[rows, :] = xn
        u = _dot(xn, w_in_ref[:, 0:OFF_GA]) + b_in_ref[:, 0:OFF_GA]
        k = u[:, OFF_K:OFF_V]
        v = u[:, OFF_V:OFF_POOL]
        up = u[:, OFF_POOL:OFF_GA]
        q_ref[rows, :] = (u[:, 0:OFF_K] * (HEAD_DIM ** -0.5 * LOG2E)).astype(BF16)
        ext_ref[POOL_PAD + r * hr:POOL_PAD + (r + 1) * hr, :] = up
        if r == 1:
            kwin_ref[...] = k[hr - WINDOW:, :]
            vwin_ref[...] = v[hr - WINDOW:, :]
            pstate_ref[...] = up[hr - POOL_PAD:, :]
        kr = pltpu.roll(k, HEAD_DIM, axis=1)
        vr = pltpu.roll(v, HEAD_DIM, axis=1)
        ke_ref[0, kv_rows, :] = jnp.where(lo, k, 0.0).astype(BF16)
        ko_ref[0, kv_rows, :] = jnp.where(lo, 0.0, kr).astype(BF16)
        ke_ref[1, kv_rows, :] = jnp.where(lo, kr, 0.0).astype(BF16)
        ko_ref[1, kv_rows, :] = jnp.where(lo, 0.0, k).astype(BF16)
        ve_ref[0, kv_rows, :] = jnp.concatenate([jnp.where(lo, v, 0.0).astype(BF16), ones_lo], axis=1)
        vo_ref[0, kv_rows, :] = jnp.concatenate([jnp.where(lo, 0.0, vr).astype(BF16), ones_hi], axis=1)
        ve_ref[1, kv_rows, :] = jnp.concatenate([jnp.where(lo, vr, 0.0).astype(BF16), ones_lo], axis=1)
        vo_ref[1, kv_rows, :] = jnp.concatenate([jnp.where(lo, 0.0, v).astype(BF16), ones_hi], axis=1)

    nb = BLOCKS_PER_ITER
    assert nb == GATE_CHUNKS

    def finish(_, carry):
        bases = [d * WINDOW for d in range(nb)]

        def gate_chunk(d):
            cols = slice(OFF_GA + d * GATE_CHUNK, OFF_GA + (d + 1) * GATE_CHUNK)
            sg_ref[d] = _sigmoid(_dot(xn_ref[...], w_in_ref[:, cols]) + b_in_ref[:, cols])

        for d in range(nb):
            _scores_block(d, bases[d], q_ref, ke_ref, ko_ref, s_ref)
        for d in range(GATE_CHUNKS):
            gate_chunk(d)
        for d in range(nb):
            has_prev = jnp.where(t == 0, 0, 1) if d == 0 else 1
            _softmax_block(d, has_prev, s_ref, biasm_ref, sinks_ref, p_ref, es_ref)
        ext = ext_ref[...]
        pos = t * tm + lax.broadcasted_iota(jnp.int32, (tm, 1), 0)
        z = (_pool_sums(ext)[POOL_PAD:, :] / _pool_counts(pos) - ext[POOL_PAD:, :]).astype(BF16)
        pool_z = (_dot(z, w_grp_ref[...]) * scale_ref[...]).astype(BF16)
        pp = _dot(pool_z, w_pool_up_ref[...])
        for d in range(nb):
            _values_block(d, bases[d], p_ref, es_ref, ve_ref, vo_ref, ao_ref)
        a = _dot(ao_ref[...], w_attn_up_ref[...])
        half = D_MODEL // 2
        m = jnp.concatenate([sg_ref[0] * a[:, :half] + sg_ref[2] * pp[:, :half],
                             sg_ref[1] * a[:, half:] + sg_ref[3] * pp[:, half:]], axis=1)
        mo = _dot(m.astype(BF16), w_out_ref[...])
        h_ref[...] = x_ref[...] + _rms(mo, g_post_ref[...])
        return carry

    lax.fori_loop(0, 1 + jnp.minimum(t, 0), finish, 0)

    for g in range(N_KV_HEADS):
        ke_ref[g, 0:WINDOW, :] = ke_ref[g, tm:tm + WINDOW, :]
        ko_ref[g, 0:WINDOW, :] = ko_ref[g, tm:tm + WINDOW, :]
        ve_ref[g, 0:WINDOW, :] = ve_ref[g, tm:tm + WINDOW, :]
        vo_ref[g, 0:WINDOW, :] = vo_ref[g, tm:tm + WINDOW, :]
    ext_ref[0:POOL_PAD, :] = ext_ref[tm:tm + POOL_PAD, :]


def _mix_sample_kernel(x_ref, ck_ref, cv_ref, state_ref, g_pre_ref, w_in_ref, b_in_ref, table_ref,
                       sinks_ref, bucket_ref, w_grp_ref, scale_ref, w_attn_up_ref, w_pool_up_ref,
                       w_out_ref, g_post_ref,
                       h_ref, kout_ref, vout_ref, up_ref,
                       biasm_ref, bias_ref, sink_ref, kx_ref, vx_ref, ext_ref, qh_ref, oh_ref):
    tm = SAMPLE_TILE
    ns = SEQ_PER_TILE
    T = DEC_SEQ
    R = N_HEADS * T

    @pl.when(pl.program_id(0) == 0)
    def _():
        _masked_bias_tables(T, bucket_ref, table_ref, biasm_ref)
        for h in range(N_HEADS):
            sink_ref[h * T:(h + 1) * T, :] = jnp.full((T, LANES), sinks_ref[0, h] * LOG2E, F32)
        for j in range(SEQ_GROUP):
            bias_ref[j * R:(j + 1) * R, :] = biasm_ref[1]
            if j > 0:
                sink_ref[j * R:(j + 1) * R, :] = sink_ref[0:R, :]
        vx_ref[:, :, LANES:] = jnp.ones((ns, N_KEYS, LANES), BF16)

    x = x_ref[...]
    xn = _rms(x, g_pre_ref[...]).astype(BF16)
    u = _dot(xn, w_in_ref[:, 0:OFF_GA]) + b_in_ref[:, 0:OFF_GA]
    q = u[:, 0:OFF_K] * (HEAD_DIM ** -0.5 * LOG2E)
    k = u[:, OFF_K:OFF_V]
    v = u[:, OFF_V:OFF_POOL]
    up = u[:, OFF_POOL:OFF_GA]
    k3 = k.reshape(ns, T, LANES)
    v3 = v.reshape(ns, T, LANES)
    ck = ck_ref[...]
    cv = cv_ref[...]

    kout_ref[:, 0:WINDOW - T, :] = ck[:, T:, :]
    kout_ref[:, WINDOW - T:, :] = k3
    vout_ref[:, 0:WINDOW - T, :] = cv[:, T:, :]
    vout_ref[:, WINDOW - T:, :] = v3
    up_ref[...] = up

    pad = jnp.zeros((ns, WINDOW - T, LANES), F32)
    kx_ref[:, 0:WINDOW, :] = ck.astype(BF16)
    kx_ref[:, WINDOW:, :] = jnp.concatenate([k3, pad], axis=1).astype(BF16)
    vx_ref[:, 0:WINDOW, 0:LANES] = cv.astype(BF16)
    vx_ref[:, WINDOW:, 0:LANES] = jnp.concatenate([v3, pad], axis=1).astype(BF16)
    ext_ref[:, 0:POOL_PAD, :] = state_ref[...]
    ext_ref[:, POOL_PAD:, :] = up.reshape(ns, T, POOL_WIDTH)
    for h, qh in enumerate(_head_queries(q)):
        qh_ref[:, h * T:(h + 1) * T, :] = qh.reshape(ns, T, LANES)

    def group(gi, carry):
        s0 = gi * SEQ_GROUP
        s = jnp.concatenate([_dot_nt(qh_ref[s0 + j].astype(BF16), kx_ref[s0 + j])
                             for j in range(SEQ_GROUP)], axis=0)
        s = s + bias_ref[...]
        sink = sink_ref[:, 0:1]
        m = jnp.maximum(jnp.max(s, axis=-1, keepdims=True), sink)
        p = jnp.exp2(s - m).astype(BF16)
        es = jnp.exp2(sink - m)
        for j in range(SEQ_GROUP):
            o = _dot(p[j * R:(j + 1) * R], vx_ref[s0 + j])
            oh_ref[s0 + j] = o[:, :LANES] / (o[:, LANES:] + es[j * R:(j + 1) * R])
        return carry

    lax.fori_loop(0, ns // SEQ_GROUP, group, 0)

    o_heads = [oh_ref[:, h * T:(h + 1) * T, :].reshape(tm, LANES) for h in range(N_HEADS)]
    attn_o = _merge_heads(o_heads).astype(BF16)

    ext = ext_ref[...].reshape(ns * (POOL_PAD + T), POOL_WIDTH)
    pool_sum = _pool_sums(ext).reshape(ns, POOL_PAD + T, POOL_WIDTH)[:, POOL_PAD:, :]
    pool_sum = pool_sum.reshape(tm, POOL_WIDTH)
    row = lax.broadcasted_iota(jnp.int32, (tm, 1), 0)
    pos = PAST_LEN + (row & (T - 1))
    h_ref[...] = _mix_tail(x, xn, attn_o, pool_sum, up, _pool_counts(pos),
                           w_in_ref, b_in_ref, w_grp_ref, scale_ref,
                           w_attn_up_ref, w_pool_up_ref, w_out_ref, g_post_ref)


def _mlp_kernel(h_ref, g_pre_ref, w_up_ref, w_down_ref, g_post_ref, y_ref):
    rows = h_ref.shape[0] // 2
    chunk = D_FF // 4
    n_chunks = D_FF // chunk
    hn = [_rms(h_ref[r * rows:(r + 1) * rows, :], g_pre_ref[...]).astype(BF16) for r in range(2)]
    f = [jnp.zeros((rows, D_MODEL), F32) for _ in range(2)]
    for c in range(n_chunks):
        for r in range(2):
            a = jnp.maximum(_dot(hn[r], w_up_ref[:, c * chunk:(c + 1) * chunk]), 0.0)
            f[r] = f[r] + _dot((a * a).astype(BF16), w_down_ref[c * chunk:(c + 1) * chunk, :])
    for r in range(2):
        y_ref[r * rows:(r + 1) * rows, :] = (h_ref[r * rows:(r + 1) * rows, :]
                                             + _rms(f[r], g_post_ref[...]))


def _full(shape):
    return pl.BlockSpec(shape, lambda *_: (0,) * len(shape), pipeline_mode=pl.Buffered(1))


def _smem():
    return pl.BlockSpec(memory_space=pltpu.SMEM)


def _rel_bucket(dist):
    d = np.maximum(dist, 0)
    max_exact = N_BUCKETS // 2
    large = max_exact + (np.log(np.maximum(d, 1).astype(np.float32) / np.float32(max_exact))
                         / np.float32(math.log(MAX_DISTANCE / max_exact))
                         * np.float32(N_BUCKETS - max_exact)).astype(np.int32)
    large = np.minimum(large, N_BUCKETS - 1)
    return np.where(d < max_exact, d, large)


def _bucket_matrix(rows):
    r = np.arange(rows)[:, None]
    j = np.arange(N_KEYS)[None, :]
    return _rel_bucket(r + WINDOW - j).astype(np.int32)


def _lane_patterns(rows):
    lo = (jnp.arange(LANES) < HEAD_DIM).astype(BF16)
    pats = jnp.stack([jnp.zeros_like(lo), lo, 1 - lo])
    return jnp.broadcast_to(pats[:, None, :], (3, rows, LANES))


def _mlp(h2d, g_pre, w_up, w_down, g_post):
    n = h2d.shape[0]
    tile = MLP_TILE
    return pl.pallas_call(
        _mlp_kernel,
        grid=(n // tile,),
        in_specs=[pl.BlockSpec((tile, D_MODEL), lambda i: (i, 0)),
                  _full((1, D_MODEL)), _full((D_MODEL, D_FF)), _full((D_FF, D_MODEL)),
                  _full((1, D_MODEL))],
        out_specs=pl.BlockSpec((tile, D_MODEL), lambda i: (i, 0)),
        out_shape=jax.ShapeDtypeStruct((n, D_MODEL), F32),
        compiler_params=pltpu.CompilerParams(dimension_semantics=("arbitrary",),
                                             vmem_limit_bytes=VMEM_LIMIT_BYTES),
        name="mlp",
    )(h2d, g_pre, w_up, w_down, g_post)


def kernel(x_prompt, x_sample, cache_k_win, cache_v_win, state_pool, norm_pre_mix, norm_post_mix,
           norm_pre_mlp, norm_post_mlp, w_in, b_in, attn_sinks, rel_bias_table, w_attn_up,
           w_pool_grp, pool_scale, w_pool_up, w_out, w_mlp_up, w_mlp_down):
    B, S, _ = x_prompt.shape
    DB, T, _ = x_sample.shape
    depth = w_in.shape[0]
    assert depth == 1 and S % TOKEN_TILE == 0 and (DB * T) % TOKEN_TILE == 0
    assert T == DEC_SEQ and (DB * T) % SAMPLE_TILE == 0
    assert GATE_CHUNKS == TOKEN_TILE // WINDOW

    l = 0
    w_in_b = w_in[l].astype(BF16)
    w_grp_b = jax.scipy.linalg.block_diag(*[w_pool_grp[l, g] for g in range(len(POOL_WINDOWS))]
                                          ).astype(BF16)
    w_attn_up_b = w_attn_up[l].astype(BF16)
    w_pool_up_b = w_pool_up[l].astype(BF16)
    w_out_b = w_out[l].astype(BF16)
    w_mlp_up_b = w_mlp_up[l].astype(BF16)
    w_mlp_down_b = w_mlp_down[l].astype(BF16)
    b_in2 = b_in[l][None, :]
    sinks2 = attn_sinks[l][None, :]
    scale2 = pool_scale[l][None, :]
    g_pre, g_post = norm_pre_mix[l][None, :], norm_post_mix[l][None, :]
    g_pre_mlp, g_post_mlp = norm_pre_mlp[l][None, :], norm_post_mlp[l][None, :]

    weight_specs = [_full((POOL_WIDTH, POOL_WIDTH)), _full((1, POOL_WIDTH)),
                    _full((ATTN_WIDTH, D_MODEL)), _full((POOL_WIDTH, D_MODEL)),
                    _full((D_MODEL, D_MODEL)), _full((1, D_MODEL))]
    weights = (w_grp_b, scale2, w_attn_up_b, w_pool_up_b, w_out_b, g_post)

    tm = TOKEN_TILE
    h_p, kwin, vwin, pstate = pl.pallas_call(
        _mix_prompt_kernel,
        grid=(B, S // tm),
        in_specs=[pl.BlockSpec((None, tm, D_MODEL), lambda b, t: (b, t, 0)),
                  _full((1, D_MODEL)), _full((D_MODEL, IN_WIDTH)), _full((1, IN_WIDTH)),
                  _smem(), _smem(), _full((WINDOW, N_KEYS)), _full((3, tm, LANES))] + weight_specs,
        out_specs=[pl.BlockSpec((None, tm, D_MODEL), lambda b, t: (b, t, 0)),
                   pl.BlockSpec((None, WINDOW, KV_WIDTH), lambda b, t: (b, 0, 0)),
                   pl.BlockSpec((None, WINDOW, KV_WIDTH), lambda b, t: (b, 0, 0)),
                   pl.BlockSpec((None, POOL_PAD, POOL_WIDTH), lambda b, t: (b, 0, 0))],
        out_shape=[jax.ShapeDtypeStruct((B, S, D_MODEL), F32),
                   jax.ShapeDtypeStruct((B, WINDOW, KV_WIDTH), F32),
                   jax.ShapeDtypeStruct((B, WINDOW, KV_WIDTH), F32),
                   jax.ShapeDtypeStruct((B, POOL_PAD, POOL_WIDTH), F32)],
        scratch_shapes=[pltpu.VMEM((2, N_HEADS * WINDOW, N_KEYS), F32),
                        pltpu.VMEM((N_KV_HEADS, WINDOW + tm, LANES), BF16),
                        pltpu.VMEM((N_KV_HEADS, WINDOW + tm, LANES), BF16),
                        pltpu.VMEM((N_KV_HEADS, WINDOW + tm, 2 * LANES), BF16),
                        pltpu.VMEM((N_KV_HEADS, WINDOW + tm, 2 * LANES), BF16),
                        pltpu.VMEM((POOL_PAD + tm, POOL_WIDTH), F32),
                        pltpu.VMEM((tm, ATTN_WIDTH), BF16),
                        pltpu.VMEM((tm, ATTN_WIDTH), BF16),
                        pltpu.VMEM((GATE_CHUNKS, tm, GATE_CHUNK), F32),
                        pltpu.VMEM((tm, D_MODEL), BF16),
                        pltpu.VMEM((BLOCKS_PER_ITER, N_KV_HEADS, 2, 2 * WINDOW, N_KEYS), F32),
                        pltpu.VMEM((BLOCKS_PER_ITER, N_KV_HEADS, 2, 2 * WINDOW, N_KEYS), BF16),
                        pltpu.VMEM((BLOCKS_PER_ITER, ATTN_WIDTH // LANES, WINDOW, LANES), F32)],
        compiler_params=pltpu.CompilerParams(dimension_semantics=("arbitrary", "arbitrary"),
                                             vmem_limit_bytes=VMEM_LIMIT_BYTES),
        name="mix_prompt",
    )(x_prompt, g_pre, w_in_b, b_in2, rel_bias_table,
      sinks2, _bucket_matrix(WINDOW), _lane_patterns(tm), *weights)
    y_p = _mlp(h_p.reshape(B * S, D_MODEL), g_pre_mlp, w_mlp_up_b, w_mlp_down_b, g_post_mlp)

    tm, ns = SAMPLE_TILE, SEQ_PER_TILE
    n_tok = DB * T
    state16 = jnp.pad(state_pool[l], ((0, 0), (POOL_PAD - POOL_STATE, 0), (0, 0)))
    ck = cache_k_win[l].reshape(DB, WINDOW, KV_WIDTH)
    cv = cache_v_win[l].reshape(DB, WINDOW, KV_WIDTH)
    h_s, k_s, v_s, up_new = pl.pallas_call(
        _mix_sample_kernel,
        grid=(n_tok // tm,),
        in_specs=[pl.BlockSpec((tm, D_MODEL), lambda i: (i, 0)),
                  pl.BlockSpec((ns, WINDOW, KV_WIDTH), lambda i: (i, 0, 0)),
                  pl.BlockSpec((ns, WINDOW, KV_WIDTH), lambda i: (i, 0, 0)),
                  pl.BlockSpec((ns, POOL_PAD, POOL_WIDTH), lambda i: (i, 0, 0)),
                  _full((1, D_MODEL)), _full((D_MODEL, IN_WIDTH)), _full((1, IN_WIDTH)),
                  _smem(), _smem(), _full((T, N_KEYS))] + weight_specs,
        out_specs=[pl.BlockSpec((tm, D_MODEL), lambda i: (i, 0)),
                   pl.BlockSpec((ns, WINDOW, KV_WIDTH), lambda i: (i, 0, 0)),
                   pl.BlockSpec((ns, WINDOW, KV_WIDTH), lambda i: (i, 0, 0)),
                   pl.BlockSpec((tm, POOL_WIDTH), lambda i: (i, 0))],
        out_shape=[jax.ShapeDtypeStruct((n_tok, D_MODEL), F32),
                   jax.ShapeDtypeStruct((DB, WINDOW, KV_WIDTH), F32),
                   jax.ShapeDtypeStruct((DB, WINDOW, KV_WIDTH), F32),
                   jax.ShapeDtypeStruct((n_tok, POOL_WIDTH), F32)],
        scratch_shapes=[pltpu.VMEM((2, N_HEADS * T, N_KEYS), F32),
                        pltpu.VMEM((SEQ_GROUP * N_HEADS * T, N_KEYS), F32),
                        pltpu.VMEM((SEQ_GROUP * N_HEADS * T, LANES), F32),
                        pltpu.VMEM((ns, N_KEYS, LANES), BF16),
                        pltpu.VMEM((ns, N_KEYS, 2 * LANES), BF16),
                        pltpu.VMEM((ns, POOL_PAD + T, POOL_WIDTH), F32),
                        pltpu.VMEM((ns, N_HEADS * T, LANES), F32),
                        pltpu.VMEM((ns, N_HEADS * T, LANES), F32)],
        compiler_params=pltpu.CompilerParams(dimension_semantics=("arbitrary",),
                                             vmem_limit_bytes=VMEM_LIMIT_BYTES),
        name="mix_sample",
    )(x_sample.reshape(n_tok, D_MODEL), ck, cv, state16, g_pre, w_in_b, b_in2, rel_bias_table,
      sinks2, _bucket_matrix(T), *weights)
    y_s = _mlp(h_s, g_pre_mlp, w_mlp_up_b, w_mlp_down_b, g_post_mlp)

    kv_shape = (1, DB, WINDOW, N_KV_HEADS, HEAD_DIM)
    k_s = k_s.reshape(kv_shape)
    v_s = v_s.reshape(kv_shape)
    p_s = jnp.concatenate([state_pool[l][:, T:], up_new.reshape(DB, T, POOL_WIDTH)], axis=1)[None]

    return (y_p.reshape(B, S, D_MODEL), y_s.reshape(DB, T, D_MODEL),
            kwin.reshape(1, B, WINDOW, N_KV_HEADS, HEAD_DIM),
            vwin.reshape(1, B, WINDOW, N_KV_HEADS, HEAD_DIM),
            pstate[:, POOL_PAD - POOL_STATE:][None],
            k_s, v_s, p_s)
```

```python
import math

import jax
import jax.numpy as jnp
import numpy as np
from jax import lax
from jax.experimental import pallas as pl
from jax.experimental.pallas import tpu as pltpu

D_MODEL = 1024
N_HEADS = 8
HEAD_DIM = 64
N_KV_HEADS = 2
GROUP = N_HEADS // N_KV_HEADS
ATTN_WIDTH = N_HEADS * HEAD_DIM
KV_WIDTH = N_KV_HEADS * HEAD_DIM
WINDOW = 128
POOL_WIDTH = D_MODEL // 2
POOL_WINDOWS = (2, 4, 8, 16)
assert POOL_WINDOWS == tuple(2 << g for g in range(len(POOL_WINDOWS)))
POOL_GROUP_WIDTH = POOL_WIDTH // len(POOL_WINDOWS)
POOL_STATE = max(POOL_WINDOWS) - 1
POOL_PAD = POOL_STATE + 1
D_FF = 4 * D_MODEL
N_BUCKETS = 32
MAX_DISTANCE = 128
RMS_EPS = 1e-6
LOG2E = math.log2(math.e)
PAST_LEN = 16384
IN_WIDTH = ATTN_WIDTH + 2 * KV_WIDTH + POOL_WIDTH + 2 * D_MODEL
OFF_K = ATTN_WIDTH
OFF_V = OFF_K + KV_WIDTH
OFF_POOL = OFF_V + KV_WIDTH
OFF_GA = OFF_POOL + POOL_WIDTH
OFF_GP = OFF_GA + D_MODEL
N_KEYS = 2 * WINDOW
GATE_CHUNKS = 4
GATE_CHUNK = (IN_WIDTH - OFF_GA) // GATE_CHUNKS
BLOCKS_PER_ITER = 4

LANES = 128
VMEM_LIMIT_BYTES = 56 * 1024 * 1024

TOKEN_TILE = 512
MLP_TILE = 1024
DEC_SEQ = 8
SAMPLE_TILE = 256
SEQ_PER_TILE = SAMPLE_TILE // DEC_SEQ
SEQ_GROUP = 16

BF16 = jnp.bfloat16
F32 = jnp.float32


def _rms(x, g):
    return x * lax.rsqrt(jnp.mean(x * x, axis=-1, keepdims=True) + RMS_EPS) * g


def _sigmoid(x):
    return 0.5 * jnp.tanh(0.5 * x) + 0.5


def _dot(a, b):
    return jnp.dot(a, b, preferred_element_type=F32)


def _dot_nt(a, b):
    return lax.dot_general(a, b, (((1,), (1,)), ((), ())), preferred_element_type=F32)


def _head_queries(q):
    lane = lax.broadcasted_iota(jnp.int32, (q.shape[0], LANES), 1)
    lo = lane < HEAD_DIM
    out = []
    for c in range(ATTN_WIDTH // LANES):
        qc = q[:, c * LANES:(c + 1) * LANES]
        qr = pltpu.roll(qc, HEAD_DIM, axis=1)
        g = (2 * c) // GROUP
        if g == 0:
            out.append(jnp.where(lo, qc, 0.0))
            out.append(jnp.where(lo, qr, 0.0))
        else:
            out.append(jnp.where(lo, 0.0, qr))
            out.append(jnp.where(lo, 0.0, qc))
    return out


def _merge_heads(o_heads):
    lane = lax.broadcasted_iota(jnp.int32, o_heads[0].shape, 1)
    lo = lane < HEAD_DIM
    chunks = []
    for c in range(ATTN_WIDTH // LANES):
        e, o = o_heads[2 * c], o_heads[2 * c + 1]
        g = (2 * c) // GROUP
        if g == 0:
            chunks.append(jnp.where(lo, e, pltpu.roll(o, HEAD_DIM, axis=1)))
        else:
            chunks.append(jnp.where(lo, pltpu.roll(e, HEAD_DIM, axis=1), o))
    return jnp.concatenate(chunks, axis=1)


def _pool_sums(ext):
    parts = []
    for gi, w in enumerate(POOL_WINDOWS):
        acc = ext[:, gi * POOL_GROUP_WIDTH:(gi + 1) * POOL_GROUP_WIDTH]
        shift = 1
        while shift < w:
            acc = acc + pltpu.roll(acc, shift, axis=0)
            shift *= 2
        parts.append(acc)
    return jnp.concatenate(parts, axis=1)


def _pool_counts(pos):
    lane = lax.broadcasted_iota(jnp.int32, (pos.shape[0], POOL_WIDTH), 1)
    w = jnp.left_shift(2, lane // POOL_GROUP_WIDTH)
    return jnp.minimum(pos + 1, w).astype(F32)


def _pool_diff(sums, up, first_pos):
    lane = lax.broadcasted_iota(jnp.int32, (1, POOL_WIDTH), 1)
    w = jnp.left_shift(2, lane // POOL_GROUP_WIDTH)
    z = sums * (1.0 / w.astype(F32)) - up
    pos = first_pos + lax.broadcasted_iota(jnp.int32, (POOL_PAD, 1), 0)
    head = sums[0:POOL_PAD] / _pool_counts(pos) - up[0:POOL_PAD]
    return jnp.concatenate([head, z[POOL_PAD:]], axis=0)


def _mix_tail(x, xn, attn_o, pool_sum, up, cnt, w_in_ref, b_in_ref, w_grp_ref, scale_ref,
              w_attn_up_ref, w_pool_up_ref, w_out_ref, g_post_ref):
    z = (pool_sum / cnt - up).astype(BF16)
    pool_z = (_dot(z, w_grp_ref[...]) * scale_ref[...]).astype(BF16)
    ga = _dot(xn, w_in_ref[:, OFF_GA:OFF_GP]) + b_in_ref[:, OFF_GA:OFF_GP]
    m = _sigmoid(ga) * _dot(attn_o, w_attn_up_ref[...])
    gp = _dot(xn, w_in_ref[:, OFF_GP:IN_WIDTH]) + b_in_ref[:, OFF_GP:IN_WIDTH]
    m = m + _sigmoid(gp) * _dot(pool_z, w_pool_up_ref[...])
    mo = _dot(m.astype(BF16), w_out_ref[...])
    return x + _rms(mo, g_post_ref[...])


def _masked_bias_tables(rows, bucket_ref, table_ref, biasm_ref):
    bucket = bucket_ref[...]
    r = lax.broadcasted_iota(jnp.int32, (rows, N_KEYS), 0)
    j = lax.broadcasted_iota(jnp.int32, (rows, N_KEYS), 1)
    cur = jnp.where(j >= WINDOW, jnp.where(j - WINDOW <= r, 1.0, 0.0), 0.0)
    prev = jnp.where(j < WINDOW, jnp.where(j > r, 1.0, 0.0), 0.0)
    for h in range(N_HEADS):
        acc = jnp.zeros(bucket.shape, F32)
        for b in range(N_BUCKETS):
            acc = jnp.where(bucket == b, table_ref[b, h] * LOG2E, acc)
        biasm_ref[0, h * rows:(h + 1) * rows, :] = jnp.where(cur > 0.5, acc, -jnp.inf)
        biasm_ref[1, h * rows:(h + 1) * rows, :] = jnp.where(cur + prev > 0.5, acc, -jnp.inf)


def _scores_block(i, base, q_ref, ke_ref, ko_ref, s_ref):
    rows = pl.ds(base, WINDOW)
    keys = pl.ds(base, N_KEYS)
    for g in range(N_KV_HEADS):
        qg = jnp.concatenate([q_ref[rows, (2 * g) * LANES:(2 * g + 1) * LANES],
                              q_ref[rows, (2 * g + 1) * LANES:(2 * g + 2) * LANES]], axis=0)
        s_ref[i, g, 0] = _dot_nt(qg, ke_ref[g, keys, :])
        s_ref[i, g, 1] = _dot_nt(qg, ko_ref[g, keys, :])


def _softmax_block(i, has_prev, s_ref, biasm_ref, sinks_ref, p_ref, es_ref):
    lo = lax.broadcasted_iota(jnp.int32, (WINDOW, LANES), 1) < HEAD_DIM
    for g in range(N_KV_HEADS):
        for a in range(2):
            half = slice(a * WINDOW, (a + 1) * WINDOW)
            es = []
            for b in range(2):
                h = GROUP * g + 2 * a + b
                s = s_ref[i, g, b, half, :] + biasm_ref[has_prev, h * WINDOW:(h + 1) * WINDOW, :]
                sink = sinks_ref[0, h] * LOG2E
                m = jnp.maximum(jnp.max(s, axis=-1, keepdims=True), sink)
                p_ref[i, g, b, half, :] = jnp.exp2(s - m).astype(BF16)
                es.append(jnp.exp2(sink - m))
            es_ref[i, 2 * g + a] = jnp.where(lo, es[0], es[1])


def _values_block(i, base, p_ref, es_ref, ve_ref, vo_ref, ao_ref):
    rows = pl.ds(base, WINDOW)
    keys = pl.ds(base, N_KEYS)
    for g in range(N_KV_HEADS):
        o_ext = _dot(p_ref[i, g, 0], ve_ref[g, keys, :]) + _dot(p_ref[i, g, 1], vo_ref[g, keys, :])
        for a in range(2):
            c = 2 * g + a
            o = o_ext[a * WINDOW:(a + 1) * WINDOW]
            ao_ref[rows, c * LANES:(c + 1) * LANES] = (
                o[:, :LANES] / (o[:, LANES:] + es_ref[i, c])).astype(BF16)


def _mix_prompt_kernel(x_ref, g_pre_ref, w_in_ref, b_in_ref, table_ref,
                       sinks_ref, bucket_ref, pat_ref,
                       w_grp_ref, scale_ref, w_attn_up_ref, w_pool_up_ref, w_out_ref, g_post_ref,
                       h_ref, kwin_ref, vwin_ref, pstate_ref,
                       biasm_ref, ke_ref, ko_ref, ve_ref, vo_ref, ext_ref, q_ref, ao_ref,
                       sg_ref, xn_ref, s_ref, p_ref, es_ref):
    tm = TOKEN_TILE
    t = pl.program_id(1)

    @pl.when((pl.program_id(0) == 0) & (t == 0))
    def _():
        _masked_bias_tables(WINDOW, bucket_ref, table_ref, biasm_ref)

    @pl.when(t == 0)
    def _():
        zeros = pat_ref[0, 0:WINDOW, :]
        for g in range(N_KV_HEADS):
            ke_ref[g, 0:WINDOW, :] = zeros
            ko_ref[g, 0:WINDOW, :] = zeros
            ve_ref[g, 0:WINDOW, :] = jnp.concatenate([zeros, pat_ref[1, 0:WINDOW, :]], axis=1)
            vo_ref[g, 0:WINDOW, :] = jnp.concatenate([zeros, pat_ref[2, 0:WINDOW, :]], axis=1)
        ext_ref[0:POOL_PAD, :] = jnp.zeros((POOL_PAD, POOL_WIDTH), F32)

    hr = tm // 2
    lo = lax.broadcasted_iota(jnp.int32, (hr, LANES), 1) < HEAD_DIM
    ones_lo, ones_hi = pat_ref[1, 0:hr, :], pat_ref[2, 0:hr, :]
    for r in range(2):
        rows = slice(r * hr, (r + 1) * hr)
        kv_rows = slice(WINDOW + r * hr, WINDOW + (r + 1) * hr)
        xn = _rms(x_ref[rows, :], g_pre_ref[...]).astype(BF16)
        xn_ref[rows, :] = xn
        u = _dot(xn, w_in_ref[:, 0:OFF_GA]) + b_in_ref[:, 0:OFF_GA]
        k = u[:, OFF_K:OFF_V]
        v = u[:, OFF_V:OFF_POOL]
        up = u[:, OFF_POOL:OFF_GA]
        q_ref[rows, :] = (u[:, 0:OFF_K] * (HEAD_DIM ** -0.5 * LOG2E)).astype(BF16)
        ext_ref[POOL_PAD + r * hr:POOL_PAD + (r + 1) * hr, :] = up
        if r == 1:
            kwin_ref[...] = k[hr - WINDOW:, :]
            vwin_ref[...] = v[hr - WINDOW:, :]
            pstate_ref[...] = up[hr - POOL_PAD:, :]
        kr = pltpu.roll(k, HEAD_DIM, axis=1)
        vr = pltpu.roll(v, HEAD_DIM, axis=1)
        ke_ref[0, kv_rows, :] = jnp.where(lo, k, 0.0).astype(BF16)
        ko_ref[0, kv_rows, :] = jnp.where(lo, 0.0, kr).astype(BF16)
        ke_ref[1, kv_rows, :] = jnp.where(lo, kr, 0.0).astype(BF16)
        ko_ref[1, kv_rows, :] = jnp.where(lo, 0.0, k).astype(BF16)
        ve_ref[0, kv_rows, :] = jnp.concatenate([jnp.where(lo, v, 0.0).astype(BF16), ones_lo], axis=1)
        vo_ref[0, kv_rows, :] = jnp.concatenate([jnp.where(lo, 0.0, vr).astype(BF16), ones_hi], axis=1)
        ve_ref[1, kv_rows, :] = jnp.concatenate([jnp.where(lo, vr, 0.0).astype(BF16), ones_lo], axis=1)
        vo_ref[1, kv_rows, :] = jnp.concatenate([jnp.where(lo, 0.0, v).astype(BF16), ones_hi], axis=1)

    nb = BLOCKS_PER_ITER
    assert nb == GATE_CHUNKS

    def finish(_, carry):
        bases = [d * WINDOW for d in range(nb)]

        def gate_chunk(d):
            cols = slice(OFF_GA + d * GATE_CHUNK, OFF_GA + (d + 1) * GATE_CHUNK)
            sg_ref[d] = _sigmoid(_dot(xn_ref[...], w_in_ref[:, cols]) + b_in_ref[:, cols])

        for d in range(nb):
            _scores_block(d, bases[d], q_ref, ke_ref, ko_ref, s_ref)
        for d in range(GATE_CHUNKS):
            gate_chunk(d)
        for d in range(nb):
            has_prev = jnp.where(t == 0, 0, 1) if d == 0 else 1
            _softmax_block(d, has_prev, s_ref, biasm_ref, sinks_ref, p_ref, es_ref)
        ext = ext_ref[...]
        z = _pool_diff(_pool_sums(ext)[POOL_PAD:, :], ext[POOL_PAD:, :], t * tm).astype(BF16)
        pool_z = (_dot(z, w_grp_ref[...]) * scale_ref[...]).astype(BF16)
        pp = _dot(pool_z, w_pool_up_ref[...])
        for d in range(nb):
            _values_block(d, bases[d], p_ref, es_ref, ve_ref, vo_ref, ao_ref)
        a = _dot(ao_ref[...], w_attn_up_ref[...])
        half = D_MODEL // 2
        m = jnp.concatenate([sg_ref[0] * a[:, :half] + sg_ref[2] * pp[:, :half],
                             sg_ref[1] * a[:, half:] + sg_ref[3] * pp[:, half:]], axis=1)
        mo = _dot(m.astype(BF16), w_out_ref[...])
        h_ref[...] = x_ref[...] + _rms(mo, g_post_ref[...])
        return carry

    lax.fori_loop(0, 1 + jnp.minimum(t, 0), finish, 0)

    for g in range(N_KV_HEADS):
        ke_ref[g, 0:WINDOW, :] = ke_ref[g, tm:tm + WINDOW, :]
        ko_ref[g, 0:WINDOW, :] = ko_ref[g, tm:tm + WINDOW, :]
        ve_ref[g, 0:WINDOW, :] = ve_ref[g, tm:tm + WINDOW, :]
        vo_ref[g, 0:WINDOW, :] = vo_ref[g, tm:tm + WINDOW, :]
    ext_ref[0:POOL_PAD, :] = ext_ref[tm:tm + POOL_PAD, :]


def _mix_sample_kernel(x_ref, ck_ref, cv_ref, state_ref, g_pre_ref, w_in_ref, b_in_ref, table_ref,
                       sinks_ref, bucket_ref, w_grp_ref, scale_ref, w_attn_up_ref, w_pool_up_ref,
                       w_out_ref, g_post_ref,
                       h_ref, kout_ref, vout_ref, up_ref,
                       biasm_ref, bias_ref, sink_ref, kx_ref, vx_ref, ext_ref, qh_ref, oh_ref):
    tm = SAMPLE_TILE
    ns = SEQ_PER_TILE
    T = DEC_SEQ
    R = N_HEADS * T

    @pl.when(pl.program_id(0) == 0)
    def _():
        _masked_bias_tables(T, bucket_ref, table_ref, biasm_ref)
        for h in range(N_HEADS):
            sink_ref[h * T:(h + 1) * T, :] = jnp.full((T, LANES), sinks_ref[0, h] * LOG2E, F32)
        for j in range(SEQ_GROUP):
            bias_ref[j * R:(j + 1) * R, :] = biasm_ref[1]
            if j > 0:
                sink_ref[j * R:(j + 1) * R, :] = sink_ref[0:R, :]
        vx_ref[:, :, LANES:] = jnp.ones((ns, N_KEYS, LANES), BF16)

    x = x_ref[...]
    xn = _rms(x, g_pre_ref[...]).astype(BF16)
    u = _dot(xn, w_in_ref[:, 0:OFF_GA]) + b_in_ref[:, 0:OFF_GA]
    q = u[:, 0:OFF_K] * (HEAD_DIM ** -0.5 * LOG2E)
    k = u[:, OFF_K:OFF_V]
    v = u[:, OFF_V:OFF_POOL]
    up = u[:, OFF_POOL:OFF_GA]
    k3 = k.reshape(ns, T, LANES)
    v3 = v.reshape(ns, T, LANES)
    ck = ck_ref[...]
    cv = cv_ref[...]

    kout_ref[:, 0:WINDOW - T, :] = ck[:, T:, :]
    kout_ref[:, WINDOW - T:, :] = k3
    vout_ref[:, 0:WINDOW - T, :] = cv[:, T:, :]
    vout_ref[:, WINDOW - T:, :] = v3
    up_ref[...] = up

    pad = jnp.zeros((ns, WINDOW - T, LANES), F32)
    kx_ref[:, 0:WINDOW, :] = ck.astype(BF16)
    kx_ref[:, WINDOW:, :] = jnp.concatenate([k3, pad], axis=1).astype(BF16)
    vx_ref[:, 0:WINDOW, 0:LANES] = cv.astype(BF16)
    vx_ref[:, WINDOW:, 0:LANES] = jnp.concatenate([v3, pad], axis=1).astype(BF16)
    ext_ref[:, 0:POOL_PAD, :] = state_ref[...]
    ext_ref[:, POOL_PAD:, :] = up.reshape(ns, T, POOL_WIDTH)
    for h, qh in enumerate(_head_queries(q)):
        qh_ref[:, h * T:(h + 1) * T, :] = qh.reshape(ns, T, LANES)

    def group(gi, carry):
        s0 = gi * SEQ_GROUP
        s = jnp.concatenate([_dot_nt(qh_ref[s0 + j].astype(BF16), kx_ref[s0 + j])
                             for j in range(SEQ_GROUP)], axis=0)
        s = s + bias_ref[...]
        sink = sink_ref[:, 0:1]
        m = jnp.maximum(jnp.max(s, axis=-1, keepdims=True), sink)
        p = jnp.exp2(s - m).astype(BF16)
        es = jnp.exp2(sink - m)
        for j in range(SEQ_GROUP):
            o = _dot(p[j * R:(j + 1) * R], vx_ref[s0 + j])
            oh_ref[s0 + j] = o[:, :LANES] / (o[:, LANES:] + es[j * R:(j + 1) * R])
        return carry

    lax.fori_loop(0, ns // SEQ_GROUP, group, 0)

    o_heads = [oh_ref[:, h * T:(h + 1) * T, :].reshape(tm, LANES) for h in range(N_HEADS)]
    attn_o = _merge_heads(o_heads).astype(BF16)

    ext = ext_ref[...].reshape(ns * (POOL_PAD + T), POOL_WIDTH)
    pool_sum = _pool_sums(ext).reshape(ns, POOL_PAD + T, POOL_WIDTH)[:, POOL_PAD:, :]
    pool_sum = pool_sum.reshape(tm, POOL_WIDTH)
    row = lax.broadcasted_iota(jnp.int32, (tm, 1), 0)
    pos = PAST_LEN + (row & (T - 1))
    h_ref[...] = _mix_tail(x, xn, attn_o, pool_sum, up, _pool_counts(pos),
                           w_in_ref, b_in_ref, w_grp_ref, scale_ref,
                           w_attn_up_ref, w_pool_up_ref, w_out_ref, g_post_ref)


def _mlp_kernel(h_ref, g_pre_ref, w_up_ref, w_down_ref, g_post_ref, y_ref):
    rows = h_ref.shape[0] // 2
    chunk = D_FF // 4
    n_chunks = D_FF // chunk
    hn = [_rms(h_ref[r * rows:(r + 1) * rows, :], g_pre_ref[...]).astype(BF16) for r in range(2)]
    f = [jnp.zeros((rows, D_MODEL), F32) for _ in range(2)]
    for c in range(n_chunks):
        for r in range(2):
            a = jnp.maximum(_dot(hn[r], w_up_ref[:, c * chunk:(c + 1) * chunk]), 0.0)
            f[r] = f[r] + _dot((a * a).astype(BF16), w_down_ref[c * chunk:(c + 1) * chunk, :])
    for r in range(2):
        y_ref[r * rows:(r + 1) * rows, :] = (h_ref[r * rows:(r + 1) * rows, :]
                                             + _rms(f[r], g_post_ref[...]))


def _full(shape):
    return pl.BlockSpec(shape, lambda *_: (0,) * len(shape), pipeline_mode=pl.Buffered(1))


def _smem():
    return pl.BlockSpec(memory_space=pltpu.SMEM)


def _rel_bucket(dist):
    d = np.maximum(dist, 0)
    max_exact = N_BUCKETS // 2
    large = max_exact + (np.log(np.maximum(d, 1).astype(np.float32) / np.float32(max_exact))
                         / np.float32(math.log(MAX_DISTANCE / max_exact))
                         * np.float32(N_BUCKETS - max_exact)).astype(np.int32)
    large = np.minimum(large, N_BUCKETS - 1)
    return np.where(d < max_exact, d, large)


def _bucket_matrix(rows):
    r = np.arange(rows)[:, None]
    j = np.arange(N_KEYS)[None, :]
    return _rel_bucket(r + WINDOW - j).astype(np.int32)


def _lane_patterns(rows):
    lo = (jnp.arange(LANES) < HEAD_DIM).astype(BF16)
    pats = jnp.stack([jnp.zeros_like(lo), lo, 1 - lo])
    return jnp.broadcast_to(pats[:, None, :], (3, rows, LANES))


def _mlp(h2d, g_pre, w_up, w_down, g_post):
    n = h2d.shape[0]
    tile = MLP_TILE
    return pl.pallas_call(
        _mlp_kernel,
        grid=(n // tile,),
        in_specs=[pl.BlockSpec((tile, D_MODEL), lambda i: (i, 0)),
                  _full((1, D_MODEL)), _full((D_MODEL, D_FF)), _full((D_FF, D_MODEL)),
                  _full((1, D_MODEL))],
        out_specs=pl.BlockSpec((tile, D_MODEL), lambda i: (i, 0)),
        out_shape=jax.ShapeDtypeStruct((n, D_MODEL), F32),
        compiler_params=pltpu.CompilerParams(dimension_semantics=("arbitrary",),
                                             vmem_limit_bytes=VMEM_LIMIT_BYTES),
        name="mlp",
    )(h2d, g_pre, w_up, w_down, g_post)


def kernel(x_prompt, x_sample, cache_k_win, cache_v_win, state_pool, norm_pre_mix, norm_post_mix,
           norm_pre_mlp, norm_post_mlp, w_in, b_in, attn_sinks, rel_bias_table, w_attn_up,
           w_pool_grp, pool_scale, w_pool_up, w_out, w_mlp_up, w_mlp_down):
    B, S, _ = x_prompt.shape
    DB, T, _ = x_sample.shape
    depth = w_in.shape[0]
    assert depth == 1 and S % TOKEN_TILE == 0 and (DB * T) % TOKEN_TILE == 0
    assert T == DEC_SEQ and (DB * T) % SAMPLE_TILE == 0
    assert GATE_CHUNKS == TOKEN_TILE // WINDOW

    l = 0
    w_in_b = w_in[l].astype(BF16)
    w_grp_b = jax.scipy.linalg.block_diag(*[w_pool_grp[l, g] for g in range(len(POOL_WINDOWS))]
                                          ).astype(BF16)
    w_attn_up_b = w_attn_up[l].astype(BF16)
    w_pool_up_b = w_pool_up[l].astype(BF16)
    w_out_b = w_out[l].astype(BF16)
    w_mlp_up_b = w_mlp_up[l].astype(BF16)
    w_mlp_down_b = w_mlp_down[l].astype(BF16)
    b_in2 = b_in[l][None, :]
    sinks2 = attn_sinks[l][None, :]
    scale2 = pool_scale[l][None, :]
    g_pre, g_post = norm_pre_mix[l][None, :], norm_post_mix[l][None, :]
    g_pre_mlp, g_post_mlp = norm_pre_mlp[l][None, :], norm_post_mlp[l][None, :]

    weight_specs = [_full((POOL_WIDTH, POOL_WIDTH)), _full((1, POOL_WIDTH)),
                    _full((ATTN_WIDTH, D_MODEL)), _full((POOL_WIDTH, D_MODEL)),
                    _full((D_MODEL, D_MODEL)), _full((1, D_MODEL))]
    weights = (w_grp_b, scale2, w_attn_up_b, w_pool_up_b, w_out_b, g_post)

    tm = TOKEN_TILE
    h_p, kwin, vwin, pstate = pl.pallas_call(
        _mix_prompt_kernel,
        grid=(B, S // tm),
        in_specs=[pl.BlockSpec((None, tm, D_MODEL), lambda b, t: (b, t, 0)),
                  _full((1, D_MODEL)), _full((D_MODEL, IN_WIDTH)), _full((1, IN_WIDTH)),
                  _smem(), _smem(), _full((WINDOW, N_KEYS)), _full((3, tm, LANES))] + weight_specs,
        out_specs=[pl.BlockSpec((None, tm, D_MODEL), lambda b, t: (b, t, 0)),
                   pl.BlockSpec((None, WINDOW, KV_WIDTH), lambda b, t: (b, 0, 0)),
                   pl.BlockSpec((None, WINDOW, KV_WIDTH), lambda b, t: (b, 0, 0)),
                   pl.BlockSpec((None, POOL_PAD, POOL_WIDTH), lambda b, t: (b, 0, 0))],
        out_shape=[jax.ShapeDtypeStruct((B, S, D_MODEL), F32),
                   jax.ShapeDtypeStruct((B, WINDOW, KV_WIDTH), F32),
                   jax.ShapeDtypeStruct((B, WINDOW, KV_WIDTH), F32),
                   jax.ShapeDtypeStruct((B, POOL_PAD, POOL_WIDTH), F32)],
        scratch_shapes=[pltpu.VMEM((2, N_HEADS * WINDOW, N_KEYS), F32),
                        pltpu.VMEM((N_KV_HEADS, WINDOW + tm, LANES), BF16),
                        pltpu.VMEM((N_KV_HEADS, WINDOW + tm, LANES), BF16),
                        pltpu.VMEM((N_KV_HEADS, WINDOW + tm, 2 * LANES), BF16),
                        pltpu.VMEM((N_KV_HEADS, WINDOW + tm, 2 * LANES), BF16),
                        pltpu.VMEM((POOL_PAD + tm, POOL_WIDTH), F32),
                        pltpu.VMEM((tm, ATTN_WIDTH), BF16),
                        pltpu.VMEM((tm, ATTN_WIDTH), BF16),
                        pltpu.VMEM((GATE_CHUNKS, tm, GATE_CHUNK), F32),
                        pltpu.VMEM((tm, D_MODEL), BF16),
                        pltpu.VMEM((BLOCKS_PER_ITER, N_KV_HEADS, 2, 2 * WINDOW, N_KEYS), F32),
                        pltpu.VMEM((BLOCKS_PER_ITER, N_KV_HEADS, 2, 2 * WINDOW, N_KEYS), BF16),
                        pltpu.VMEM((BLOCKS_PER_ITER, ATTN_WIDTH // LANES, WINDOW, LANES), F32)],
        compiler_params=pltpu.CompilerParams(dimension_semantics=("arbitrary", "arbitrary"),
                                             vmem_limit_bytes=VMEM_LIMIT_BYTES),
        name="mix_prompt",
    )(x_prompt, g_pre, w_in_b, b_in2, rel_bias_table,
      sinks2, _bucket_matrix(WINDOW), _lane_patterns(tm), *weights)
    y_p = _mlp(h_p.reshape(B * S, D_MODEL), g_pre_mlp, w_mlp_up_b, w_mlp_down_b, g_post_mlp)

    tm, ns = SAMPLE_TILE, SEQ_PER_TILE
    n_tok = DB * T
    state16 = jnp.pad(state_pool[l], ((0, 0), (POOL_PAD - POOL_STATE, 0), (0, 0)))
    ck = cache_k_win[l].reshape(DB, WINDOW, KV_WIDTH)
    cv = cache_v_win[l].reshape(DB, WINDOW, KV_WIDTH)
    h_s, k_s, v_s, up_new = pl.pallas_call(
        _mix_sample_kernel,
        grid=(n_tok // tm,),
        in_specs=[pl.BlockSpec((tm, D_MODEL), lambda i: (i, 0)),
                  pl.BlockSpec((ns, WINDOW, KV_WIDTH), lambda i: (i, 0, 0)),
                  pl.BlockSpec((ns, WINDOW, KV_WIDTH), lambda i: (i, 0, 0)),
                  pl.BlockSpec((ns, POOL_PAD, POOL_WIDTH), lambda i: (i, 0, 0)),
                  _full((1, D_MODEL)), _full((D_MODEL, IN_WIDTH)), _full((1, IN_WIDTH)),
                  _smem(), _smem(), _full((T, N_KEYS))] + weight_specs,
        out_specs=[pl.BlockSpec((tm, D_MODEL), lambda i: (i, 0)),
                   pl.BlockSpec((ns, WINDOW, KV_WIDTH), lambda i: (i, 0, 0)),
                   pl.BlockSpec((ns, WINDOW, KV_WIDTH), lambda i: (i, 0, 0)),
                   pl.BlockSpec((tm, POOL_WIDTH), lambda i: (i, 0))],
        out_shape=[jax.ShapeDtypeStruct((n_tok, D_MODEL), F32),
                   jax.ShapeDtypeStruct((DB, WINDOW, KV_WIDTH), F32),
                   jax.ShapeDtypeStruct((DB, WINDOW, KV_WIDTH), F32),
                   jax.ShapeDtypeStruct((n_tok, POOL_WIDTH), F32)],
        scratch_shapes=[pltpu.VMEM((2, N_HEADS * T, N_KEYS), F32),
                        pltpu.VMEM((SEQ_GROUP * N_HEADS * T, N_KEYS), F32),
                        pltpu.VMEM((SEQ_GROUP * N_HEADS * T, LANES), F32),
                        pltpu.VMEM((ns, N_KEYS, LANES), BF16),
                        pltpu.VMEM((ns, N_KEYS, 2 * LANES), BF16),
                        pltpu.VMEM((ns, POOL_PAD + T, POOL_WIDTH), F32),
                        pltpu.VMEM((ns, N_HEADS * T, LANES), F32),
                        pltpu.VMEM((ns, N_HEADS * T, LANES), F32)],
        compiler_params=pltpu.CompilerParams(dimension_semantics=("arbitrary",),
                                             vmem_limit_bytes=VMEM_LIMIT_BYTES),
        name="mix_sample",
    )(x_sample.reshape(n_tok, D_MODEL), ck, cv, state16, g_pre, w_in_b, b_in2, rel_bias_table,
      sinks2, _bucket_matrix(T), *weights)
    y_s = _mlp(h_s, g_pre_mlp, w_mlp_up_b, w_mlp_down_b, g_post_mlp)

    kv_shape = (1, DB, WINDOW, N_KV_HEADS, HEAD_DIM)
    k_s = k_s.reshape(kv_shape)
    v_s = v_s.reshape(kv_shape)
    p_s = jnp.concatenate([state_pool[l][:, T:], up_new.reshape(DB, T, POOL_WIDTH)], axis=1)[None]

    return (y_p.reshape(B, S, D_MODEL), y_s.reshape(DB, T, D_MODEL),
            kwin.reshape(1, B, WINDOW, N_KV_HEADS, HEAD_DIM),
            vwin.reshape(1, B, WINDOW, N_KV_HEADS, HEAD_DIM),
            pstate[:, POOL_PAD - POOL_STATE:][None],
            k_s, v_s, p_s)
```

```python
import math

import jax
import jax.numpy as jnp
import numpy as np
from jax import lax
from jax.experimental import pallas as pl
from jax.experimental.pallas import tpu as pltpu

D_MODEL = 1024
N_HEADS = 8
HEAD_DIM = 64
N_KV_HEADS = 2
GROUP = N_HEADS // N_KV_HEADS
ATTN_WIDTH = N_HEADS * HEAD_DIM
KV_WIDTH = N_KV_HEADS * HEAD_DIM
WINDOW = 128
POOL_WIDTH = D_MODEL // 2
POOL_WINDOWS = (2, 4, 8, 16)
assert POOL_WINDOWS == tuple(2 << g for g in range(len(POOL_WINDOWS)))
POOL_GROUP_WIDTH = POOL_WIDTH // len(POOL_WINDOWS)
POOL_STATE = max(POOL_WINDOWS) - 1
POOL_PAD = POOL_STATE + 1
D_FF = 4 * D_MODEL
N_BUCKETS = 32
MAX_DISTANCE = 128
RMS_EPS = 1e-6
LOG2E = math.log2(math.e)
PAST_LEN = 16384
IN_WIDTH = ATTN_WIDTH + 2 * KV_WIDTH + POOL_WIDTH + 2 * D_MODEL
OFF_K = ATTN_WIDTH
OFF_V = OFF_K + KV_WIDTH
OFF_POOL = OFF_V + KV_WIDTH
OFF_GA = OFF_POOL + POOL_WIDTH
OFF_GP = OFF_GA + D_MODEL
N_KEYS = 2 * WINDOW
GATE_CHUNKS = 4
GATE_CHUNK = (IN_WIDTH - OFF_GA) // GATE_CHUNKS
BLOCKS_PER_ITER = 4

LANES = 128
VMEM_LIMIT_BYTES = 56 * 1024 * 1024

TOKEN_TILE = 512
TILES_PER_STEP = 2
MLP_TILE = 1024
DEC_SEQ = 8
SAMPLE_TILE = 256
SEQ_PER_TILE = SAMPLE_TILE // DEC_SEQ
SEQ_GROUP = 16

BF16 = jnp.bfloat16
F32 = jnp.float32


def _rms(x, g):
    return x * lax.rsqrt(jnp.mean(x * x, axis=-1, keepdims=True) + RMS_EPS) * g


def _sigmoid(x):
    return 0.5 * jnp.tanh(0.5 * x) + 0.5


def _dot(a, b):
    return jnp.dot(a, b, preferred_element_type=F32)


def _dot_nt(a, b):
    return lax.dot_general(a, b, (((1,), (1,)), ((), ())), preferred_element_type=F32)


def _head_queries(q):
    lane = lax.broadcasted_iota(jnp.int32, (q.shape[0], LANES), 1)
    lo = lane < HEAD_DIM
    out = []
    for c in range(ATTN_WIDTH // LANES):
        qc = q[:, c * LANES:(c + 1) * LANES]
        qr = pltpu.roll(qc, HEAD_DIM, axis=1)
        g = (2 * c) // GROUP
        if g == 0:
            out.append(jnp.where(lo, qc, 0.0))
            out.append(jnp.where(lo, qr, 0.0))
        else:
            out.append(jnp.where(lo, 0.0, qr))
            out.append(jnp.where(lo, 0.0, qc))
    return out


def _merge_heads(o_heads):
    lane = lax.broadcasted_iota(jnp.int32, o_heads[0].shape, 1)
    lo = lane < HEAD_DIM
    chunks = []
    for c in range(ATTN_WIDTH // LANES):
        e, o = o_heads[2 * c], o_heads[2 * c + 1]
        g = (2 * c) // GROUP
        if g == 0:
            chunks.append(jnp.where(lo, e, pltpu.roll(o, HEAD_DIM, axis=1)))
        else:
            chunks.append(jnp.where(lo, pltpu.roll(e, HEAD_DIM, axis=1), o))
    return jnp.concatenate(chunks, axis=1)


def _pool_sums(ext):
    parts = []
    for gi, w in enumerate(POOL_WINDOWS):
        acc = ext[:, gi * POOL_GROUP_WIDTH:(gi + 1) * POOL_GROUP_WIDTH]
        shift = 1
        while shift < w:
            acc = acc + pltpu.roll(acc, shift, axis=0)
            shift *= 2
        parts.append(acc)
    return jnp.concatenate(parts, axis=1)


def _pool_counts(pos):
    lane = lax.broadcasted_iota(jnp.int32, (pos.shape[0], POOL_WIDTH), 1)
    w = jnp.left_shift(2, lane // POOL_GROUP_WIDTH)
    return jnp.minimum(pos + 1, w).astype(F32)


def _pool_diff(sums, up, first_pos):
    lane = lax.broadcasted_iota(jnp.int32, (1, POOL_WIDTH), 1)
    w = jnp.left_shift(2, lane // POOL_GROUP_WIDTH)
    z = sums * (1.0 / w.astype(F32)) - up
    pos = first_pos + lax.broadcasted_iota(jnp.int32, (POOL_PAD, 1), 0)
    head = sums[0:POOL_PAD] / _pool_counts(pos) - up[0:POOL_PAD]
    return jnp.concatenate([head, z[POOL_PAD:]], axis=0)


def _mix_tail(x, xn, attn_o, pool_sum, up, cnt, w_in_ref, b_in_ref, w_grp_ref, scale_ref,
              w_attn_up_ref, w_pool_up_ref, w_out_ref, g_post_ref):
    z = (pool_sum / cnt - up).astype(BF16)
    pool_z = (_dot(z, w_grp_ref[...]) * scale_ref[...]).astype(BF16)
    ga = _dot(xn, w_in_ref[:, OFF_GA:OFF_GP]) + b_in_ref[:, OFF_GA:OFF_GP]
    m = _sigmoid(ga) * _dot(attn_o, w_attn_up_ref[...])
    gp = _dot(xn, w_in_ref[:, OFF_GP:IN_WIDTH]) + b_in_ref[:, OFF_GP:IN_WIDTH]
    m = m + _sigmoid(gp) * _dot(pool_z, w_pool_up_ref[...])
    mo = _dot(m.astype(BF16), w_out_ref[...])
    return x + _rms(mo, g_post_ref[...])


def _masked_bias_tables(rows, bucket_ref, table_ref, biasm_ref):
    bucket = bucket_ref[...]
    r = lax.broadcasted_iota(jnp.int32, (rows, N_KEYS), 0)
    j = lax.broadcasted_iota(jnp.int32, (rows, N_KEYS), 1)
    cur = jnp.where(j >= WINDOW, jnp.where(j - WINDOW <= r, 1.0, 0.0), 0.0)
    prev = jnp.where(j < WINDOW, jnp.where(j > r, 1.0, 0.0), 0.0)
    for h in range(N_HEADS):
        acc = jnp.zeros(bucket.shape, F32)
        for b in range(N_BUCKETS):
            acc = jnp.where(bucket == b, table_ref[b, h] * LOG2E, acc)
        biasm_ref[0, h * rows:(h + 1) * rows, :] = jnp.where(cur > 0.5, acc, -jnp.inf)
        biasm_ref[1, h * rows:(h + 1) * rows, :] = jnp.where(cur + prev > 0.5, acc, -jnp.inf)


def _scores_block(i, base, q_ref, ke_ref, ko_ref, s_ref):
    rows = pl.ds(base, WINDOW)
    keys = pl.ds(base, N_KEYS)
    for g in range(N_KV_HEADS):
        qg = jnp.concatenate([q_ref[rows, (2 * g) * LANES:(2 * g + 1) * LANES],
                              q_ref[rows, (2 * g + 1) * LANES:(2 * g + 2) * LANES]], axis=0)
        s_ref[i, g, 0] = _dot_nt(qg, ke_ref[g, keys, :])
        s_ref[i, g, 1] = _dot_nt(qg, ko_ref[g, keys, :])


def _softmax_block(i, has_prev, s_ref, biasm_ref, sinks_ref, p_ref, es_ref):
    lo = lax.broadcasted_iota(jnp.int32, (WINDOW, LANES), 1) < HEAD_DIM
    for g in range(N_KV_HEADS):
        for a in range(2):
            half = slice(a * WINDOW, (a + 1) * WINDOW)
            es = []
            for b in range(2):
                h = GROUP * g + 2 * a + b
                s = s_ref[i, g, b, half, :] + biasm_ref[has_prev, h * WINDOW:(h + 1) * WINDOW, :]
                sink = sinks_ref[0, h] * LOG2E
                m = jnp.maximum(jnp.max(s, axis=-1, keepdims=True), sink)
                p_ref[i, g, b, half, :] = jnp.exp2(s - m).astype(BF16)
                es.append(jnp.exp2(sink - m))
            es_ref[i, 2 * g + a] = jnp.where(lo, es[0], es[1])


def _values_block(i, base, p_ref, es_ref, ve_ref, vo_ref, ao_ref):
    rows = pl.ds(base, WINDOW)
    keys = pl.ds(base, N_KEYS)
    for g in range(N_KV_HEADS):
        o_ext = _dot(p_ref[i, g, 0], ve_ref[g, keys, :]) + _dot(p_ref[i, g, 1], vo_ref[g, keys, :])
        for a in range(2):
            c = 2 * g + a
            o = o_ext[a * WINDOW:(a + 1) * WINDOW]
            ao_ref[rows, c * LANES:(c + 1) * LANES] = (
                o[:, :LANES] / (o[:, LANES:] + es_ref[i, c])).astype(BF16)


def _mix_prompt_tile(t, x_ref, g_pre_ref, w_in_ref, b_in_ref, table_ref,
                     sinks_ref, bucket_ref, pat_ref,
                     w_grp_ref, scale_ref, w_attn_up_ref, w_pool_up_ref, w_out_ref, g_post_ref,
                     h_ref, kwin_ref, vwin_ref, pstate_ref,
                     biasm_ref, ke_ref, ko_ref, ve_ref, vo_ref, ext_ref, q_ref, ao_ref,
                     sg_ref, xn_ref, s_ref, p_ref, es_ref):
    tm = TOKEN_TILE

    @pl.when((pl.program_id(0) == 0) & (t == 0))
    def _():
        _masked_bias_tables(WINDOW, bucket_ref, table_ref, biasm_ref)

    @pl.when(t == 0)
    def _():
        zeros = pat_ref[0, 0:WINDOW, :]
        for g in range(N_KV_HEADS):
            ke_ref[g, 0:WINDOW, :] = zeros
            ko_ref[g, 0:WINDOW, :] = zeros
            ve_ref[g, 0:WINDOW, :] = jnp.concatenate([zeros, pat_ref[1, 0:WINDOW, :]], axis=1)
            vo_ref[g, 0:WINDOW, :] = jnp.concatenate([zeros, pat_ref[2, 0:WINDOW, :]], axis=1)
        ext_ref[0:POOL_PAD, :] = jnp.zeros((POOL_PAD, POOL_WIDTH), F32)

    hr = tm // 2
    lo = lax.broadcasted_iota(jnp.int32, (hr, LANES), 1) < HEAD_DIM
    ones_lo, ones_hi = pat_ref[1, 0:hr, :], pat_ref[2, 0:hr, :]
    for r in range(2):
        rows = slice(r * hr, (r + 1) * hr)
        kv_rows = slice(WINDOW + r * hr, WINDOW + (r + 1) * hr)
        xn = _rms(x_ref[rows, :], g_pre_ref[...]).astype(BF16)
        xn_ref[rows, :] = xn
        u = _dot(xn, w_in_ref[:, 0:OFF_GA]) + b_in_ref[:, 0:OFF_GA]
        k = u[:, OFF_K:OFF_V]
        v = u[:, OFF_V:OFF_POOL]
        up = u[:, OFF_POOL:OFF_GA]
        q_ref[rows, :] = (u[:, 0:OFF_K] * (HEAD_DIM ** -0.5 * LOG2E)).astype(BF16)
        ext_ref[POOL_PAD + r * hr:POOL_PAD + (r + 1) * hr, :] = up
        if r == 1:
            kwin_ref[...] = k[hr - WINDOW:, :]
            vwin_ref[...] = v[hr - WINDOW:, :]
            pstate_ref[...] = up[hr - POOL_PAD:, :]
        kr = pltpu.roll(k, HEAD_DIM, axis=1)
        vr = pltpu.roll(v, HEAD_DIM, axis=1)
        ke_ref[0, kv_rows, :] = jnp.where(lo, k, 0.0).astype(BF16)
        ko_ref[0, kv_rows, :] = jnp.where(lo, 0.0, kr).astype(BF16)
        ke_ref[1, kv_rows, :] = jnp.where(lo, kr, 0.0).astype(BF16)
        ko_ref[1, kv_rows, :] = jnp.where(lo, 0.0, k).astype(BF16)
        ve_ref[0, kv_rows, :] = jnp.concatenate([jnp.where(lo, v, 0.0).astype(BF16), ones_lo], axis=1)
        vo_ref[0, kv_rows, :] = jnp.concatenate([jnp.where(lo, 0.0, vr).astype(BF16), ones_hi], axis=1)
        ve_ref[1, kv_rows, :] = jnp.concatenate([jnp.where(lo, vr, 0.0).astype(BF16), ones_lo], axis=1)
        vo_ref[1, kv_rows, :] = jnp.concatenate([jnp.where(lo, 0.0, v).astype(BF16), ones_hi], axis=1)

    nb = BLOCKS_PER_ITER
    assert nb == GATE_CHUNKS

    def finish(_, carry):
        bases = [d * WINDOW for d in range(nb)]

        def gate_chunk(d):
            cols = slice(OFF_GA + d * GATE_CHUNK, OFF_GA + (d + 1) * GATE_CHUNK)
            sg_ref[d] = _sigmoid(_dot(xn_ref[...], w_in_ref[:, cols]) + b_in_ref[:, cols])

        for d in range(nb):
            _scores_block(d, bases[d], q_ref, ke_ref, ko_ref, s_ref)
        for d in range(GATE_CHUNKS):
            gate_chunk(d)
        for d in range(nb):
            has_prev = jnp.where(t == 0, 0, 1) if d == 0 else 1
            _softmax_block(d, has_prev, s_ref, biasm_ref, sinks_ref, p_ref, es_ref)
        ext = ext_ref[...]
        z = _pool_diff(_pool_sums(ext)[POOL_PAD:, :], ext[POOL_PAD:, :], t * tm).astype(BF16)
        pool_z = (_dot(z, w_grp_ref[...]) * scale_ref[...]).astype(BF16)
        pp = _dot(pool_z, w_pool_up_ref[...])
        for d in range(nb):
            _values_block(d, bases[d], p_ref, es_ref, ve_ref, vo_ref, ao_ref)
        a = _dot(ao_ref[...], w_attn_up_ref[...])
        half = D_MODEL // 2
        m = jnp.concatenate([sg_ref[0] * a[:, :half] + sg_ref[2] * pp[:, :half],
                             sg_ref[1] * a[:, half:] + sg_ref[3] * pp[:, half:]], axis=1)
        mo = _dot(m.astype(BF16), w_out_ref[...])
        h_ref[...] = x_ref[...] + _rms(mo, g_post_ref[...])
        return carry

    lax.fori_loop(0, 1 + jnp.minimum(t, 0), finish, 0)

    for g in range(N_KV_HEADS):
        ke_ref[g, 0:WINDOW, :] = ke_ref[g, tm:tm + WINDOW, :]
        ko_ref[g, 0:WINDOW, :] = ko_ref[g, tm:tm + WINDOW, :]
        ve_ref[g, 0:WINDOW, :] = ve_ref[g, tm:tm + WINDOW, :]
        vo_ref[g, 0:WINDOW, :] = vo_ref[g, tm:tm + WINDOW, :]
    ext_ref[0:POOL_PAD, :] = ext_ref[tm:tm + POOL_PAD, :]


def _mix_prompt_kernel(x_ref, *refs):
    n_in = 13
    h_ref, rest = refs[n_in], refs[n_in + 1:]

    def tile(j, carry):
        rows = pl.ds(pl.multiple_of(j * TOKEN_TILE, TOKEN_TILE), TOKEN_TILE)
        _mix_prompt_tile(pl.program_id(1) * TILES_PER_STEP + j, x_ref.at[rows, :], *refs[:n_in],
                         h_ref.at[rows, :], *rest)
        return carry

    lax.fori_loop(0, TILES_PER_STEP, tile, 0)


def _mix_sample_kernel(x_ref, ck_ref, cv_ref, state_ref, g_pre_ref, w_in_ref, b_in_ref, table_ref,
                       sinks_ref, bucket_ref, w_grp_ref, scale_ref, w_attn_up_ref, w_pool_up_ref,
                       w_out_ref, g_post_ref,
                       h_ref, kout_ref, vout_ref, up_ref,
                       biasm_ref, bias_ref, sink_ref, kx_ref, vx_ref, ext_ref, qh_ref, oh_ref):
    tm = SAMPLE_TILE
    ns = SEQ_PER_TILE
    T = DEC_SEQ
    R = N_HEADS * T

    @pl.when(pl.program_id(0) == 0)
    def _():
        _masked_bias_tables(T, bucket_ref, table_ref, biasm_ref)
        for h in range(N_HEADS):
            sink_ref[h * T:(h + 1) * T, :] = jnp.full((T, LANES), sinks_ref[0, h] * LOG2E, F32)
        for j in range(SEQ_GROUP):
            bias_ref[j * R:(j + 1) * R, :] = biasm_ref[1]
            if j > 0:
                sink_ref[j * R:(j + 1) * R, :] = sink_ref[0:R, :]
        vx_ref[:, :, LANES:] = jnp.ones((ns, N_KEYS, LANES), BF16)

    x = x_ref[...]
    xn = _rms(x, g_pre_ref[...]).astype(BF16)
    u = _dot(xn, w_in_ref[:, 0:OFF_GA]) + b_in_ref[:, 0:OFF_GA]
    q = u[:, 0:OFF_K] * (HEAD_DIM ** -0.5 * LOG2E)
    k = u[:, OFF_K:OFF_V]
    v = u[:, OFF_V:OFF_POOL]
    up = u[:, OFF_POOL:OFF_GA]
    k3 = k.reshape(ns, T, LANES)
    v3 = v.reshape(ns, T, LANES)
    ck = ck_ref[...]
    cv = cv_ref[...]

    kout_ref[:, 0:WINDOW - T, :] = ck[:, T:, :]
    kout_ref[:, WINDOW - T:, :] = k3
    vout_ref[:, 0:WINDOW - T, :] = cv[:, T:, :]
    vout_ref[:, WINDOW - T:, :] = v3
    up_ref[...] = up

    pad = jnp.zeros((ns, WINDOW - T, LANES), F32)
    kx_ref[:, 0:WINDOW, :] = ck.astype(BF16)
    kx_ref[:, WINDOW:, :] = jnp.concatenate([k3, pad], axis=1).astype(BF16)
    vx_ref[:, 0:WINDOW, 0:LANES] = cv.astype(BF16)
    vx_ref[:, WINDOW:, 0:LANES] = jnp.concatenate([v3, pad], axis=1).astype(BF16)
    ext_ref[:, 0:POOL_PAD, :] = state_ref[...]
    ext_ref[:, POOL_PAD:, :] = up.reshape(ns, T, POOL_WIDTH)
    for h, qh in enumerate(_head_queries(q)):
        qh_ref[:, h * T:(h + 1) * T, :] = qh.reshape(ns, T, LANES)

    def group(gi, carry):
        s0 = gi * SEQ_GROUP
        s = jnp.concatenate([_dot_nt(qh_ref[s0 + j].astype(BF16), kx_ref[s0 + j])
                             for j in range(SEQ_GROUP)], axis=0)
        s = s + bias_ref[...]
        sink = sink_ref[:, 0:1]
        m = jnp.maximum(jnp.max(s, axis=-1, keepdims=True), sink)
        p = jnp.exp2(s - m).astype(BF16)
        es = jnp.exp2(sink - m)
        for j in range(SEQ_GROUP):
            o = _dot(p[j * R:(j + 1) * R], vx_ref[s0 + j])
            oh_ref[s0 + j] = o[:, :LANES] / (o[:, LANES:] + es[j * R:(j + 1) * R])
        return carry

    lax.fori_loop(0, ns // SEQ_GROUP, group, 0)

    o_heads = [oh_ref[:, h * T:(h + 1) * T, :].reshape(tm, LANES) for h in range(N_HEADS)]
    attn_o = _merge_heads(o_heads).astype(BF16)

    ext = ext_ref[...].reshape(ns * (POOL_PAD + T), POOL_WIDTH)
    pool_sum = _pool_sums(ext).reshape(ns, POOL_PAD + T, POOL_WIDTH)[:, POOL_PAD:, :]
    pool_sum = pool_sum.reshape(tm, POOL_WIDTH)
    row = lax.broadcasted_iota(jnp.int32, (tm, 1), 0)
    pos = PAST_LEN + (row & (T - 1))
    h_ref[...] = _mix_tail(x, xn, attn_o, pool_sum, up, _pool_counts(pos),
                           w_in_ref, b_in_ref, w_grp_ref, scale_ref,
                           w_attn_up_ref, w_pool_up_ref, w_out_ref, g_post_ref)


def _mlp_kernel(h_ref, g_pre_ref, w_up_ref, w_down_ref, g_post_ref, y_ref):
    rows = h_ref.shape[0] // 2
    chunk = D_FF // 4
    n_chunks = D_FF // chunk
    hn = [_rms(h_ref[r * rows:(r + 1) * rows, :], g_pre_ref[...]).astype(BF16) for r in range(2)]
    f = [jnp.zeros((rows, D_MODEL), F32) for _ in range(2)]
    for c in range(n_chunks):
        for r in range(2):
            a = jnp.maximum(_dot(hn[r], w_up_ref[:, c * chunk:(c + 1) * chunk]), 0.0)
            f[r] = f[r] + _dot((a * a).astype(BF16), w_down_ref[c * chunk:(c + 1) * chunk, :])
    for r in range(2):
        y_ref[r * rows:(r + 1) * rows, :] = (h_ref[r * rows:(r + 1) * rows, :]
                                             + _rms(f[r], g_post_ref[...]))


def _full(shape):
    return pl.BlockSpec(shape, lambda *_: (0,) * len(shape), pipeline_mode=pl.Buffered(1))


def _smem():
    return pl.BlockSpec(memory_space=pltpu.SMEM)


def _rel_bucket(dist):
    d = np.maximum(dist, 0)
    max_exact = N_BUCKETS // 2
    large = max_exact + (np.log(np.maximum(d, 1).astype(np.float32) / np.float32(max_exact))
                         / np.float32(math.log(MAX_DISTANCE / max_exact))
                         * np.float32(N_BUCKETS - max_exact)).astype(np.int32)
    large = np.minimum(large, N_BUCKETS - 1)
    return np.where(d < max_exact, d, large)


def _bucket_matrix(rows):
    r = np.arange(rows)[:, None]
    j = np.arange(N_KEYS)[None, :]
    return _rel_bucket(r + WINDOW - j).astype(np.int32)


def _lane_patterns(rows):
    lo = (jnp.arange(LANES) < HEAD_DIM).astype(BF16)
    pats = jnp.stack([jnp.zeros_like(lo), lo, 1 - lo])
    return jnp.broadcast_to(pats[:, None, :], (3, rows, LANES))


def _mlp(h2d, g_pre, w_up, w_down, g_post):
    n = h2d.shape[0]
    tile = MLP_TILE
    return pl.pallas_call(
        _mlp_kernel,
        grid=(n // tile,),
        in_specs=[pl.BlockSpec((tile, D_MODEL), lambda i: (i, 0)),
                  _full((1, D_MODEL)), _full((D_MODEL, D_FF)), _full((D_FF, D_MODEL)),
                  _full((1, D_MODEL))],
        out_specs=pl.BlockSpec((tile, D_MODEL), lambda i: (i, 0)),
        out_shape=jax.ShapeDtypeStruct((n, D_MODEL), F32),
        compiler_params=pltpu.CompilerParams(dimension_semantics=("arbitrary",),
                                             vmem_limit_bytes=VMEM_LIMIT_BYTES),
        name="mlp",
    )(h2d, g_pre, w_up, w_down, g_post)


def kernel(x_prompt, x_sample, cache_k_win, cache_v_win, state_pool, norm_pre_mix, norm_post_mix,
           norm_pre_mlp, norm_post_mlp, w_in, b_in, attn_sinks, rel_bias_table, w_attn_up,
           w_pool_grp, pool_scale, w_pool_up, w_out, w_mlp_up, w_mlp_down):
    B, S, _ = x_prompt.shape
    DB, T, _ = x_sample.shape
    depth = w_in.shape[0]
    assert depth == 1 and S % TOKEN_TILE == 0 and (DB * T) % TOKEN_TILE == 0
    assert T == DEC_SEQ and (DB * T) % SAMPLE_TILE == 0
    assert GATE_CHUNKS == TOKEN_TILE // WINDOW

    l = 0
    w_in_b = w_in[l].astype(BF16)
    w_grp_b = jax.scipy.linalg.block_diag(*[w_pool_grp[l, g] for g in range(len(POOL_WINDOWS))]
                                          ).astype(BF16)
    w_attn_up_b = w_attn_up[l].astype(BF16)
    w_pool_up_b = w_pool_up[l].astype(BF16)
    w_out_b = w_out[l].astype(BF16)
    w_mlp_up_b = w_mlp_up[l].astype(BF16)
    w_mlp_down_b = w_mlp_down[l].astype(BF16)
    b_in2 = b_in[l][None, :]
    sinks2 = attn_sinks[l][None, :]
    scale2 = pool_scale[l][None, :]
    g_pre, g_post = norm_pre_mix[l][None, :], norm_post_mix[l][None, :]
    g_pre_mlp, g_post_mlp = norm_pre_mlp[l][None, :], norm_post_mlp[l][None, :]

    weight_specs = [_full((POOL_WIDTH, POOL_WIDTH)), _full((1, POOL_WIDTH)),
                    _full((ATTN_WIDTH, D_MODEL)), _full((POOL_WIDTH, D_MODEL)),
                    _full((D_MODEL, D_MODEL)), _full((1, D_MODEL))]
    weights = (w_grp_b, scale2, w_attn_up_b, w_pool_up_b, w_out_b, g_post)

    tm = TOKEN_TILE
    step = tm * TILES_PER_STEP
    assert S % step == 0
    h_p, kwin, vwin, pstate = pl.pallas_call(
        _mix_prompt_kernel,
        grid=(B, S // step),
        in_specs=[pl.BlockSpec((None, step, D_MODEL), lambda b, t: (b, t, 0)),
                  _full((1, D_MODEL)), _full((D_MODEL, IN_WIDTH)), _full((1, IN_WIDTH)),
                  _smem(), _smem(), _full((WINDOW, N_KEYS)), _full((3, tm, LANES))] + weight_specs,
        out_specs=[pl.BlockSpec((None, step, D_MODEL), lambda b, t: (b, t, 0)),
                   pl.BlockSpec((None, WINDOW, KV_WIDTH), lambda b, t: (b, 0, 0)),
                   pl.BlockSpec((None, WINDOW, KV_WIDTH), lambda b, t: (b, 0, 0)),
                   pl.BlockSpec((None, POOL_PAD, POOL_WIDTH), lambda b, t: (b, 0, 0))],
        out_shape=[jax.ShapeDtypeStruct((B, S, D_MODEL), F32),
                   jax.ShapeDtypeStruct((B, WINDOW, KV_WIDTH), F32),
                   jax.ShapeDtypeStruct((B, WINDOW, KV_WIDTH), F32),
                   jax.ShapeDtypeStruct((B, POOL_PAD, POOL_WIDTH), F32)],
        scratch_shapes=[pltpu.VMEM((2, N_HEADS * WINDOW, N_KEYS), F32),
                        pltpu.VMEM((N_KV_HEADS, WINDOW + tm, LANES), BF16),
                        pltpu.VMEM((N_KV_HEADS, WINDOW + tm, LANES), BF16),
                        pltpu.VMEM((N_KV_HEADS, WINDOW + tm, 2 * LANES), BF16),
                        pltpu.VMEM((N_KV_HEADS, WINDOW + tm, 2 * LANES), BF16),
                        pltpu.VMEM((POOL_PAD + tm, POOL_WIDTH), F32),
                        pltpu.VMEM((tm, ATTN_WIDTH), BF16),
                        pltpu.VMEM((tm, ATTN_WIDTH), BF16),
                        pltpu.VMEM((GATE_CHUNKS, tm, GATE_CHUNK), F32),
                        pltpu.VMEM((tm, D_MODEL), BF16),
                        pltpu.VMEM((BLOCKS_PER_ITER, N_KV_HEADS, 2, 2 * WINDOW, N_KEYS), F32),
                        pltpu.VMEM((BLOCKS_PER_ITER, N_KV_HEADS, 2, 2 * WINDOW, N_KEYS), BF16),
                        pltpu.VMEM((BLOCKS_PER_ITER, ATTN_WIDTH // LANES, WINDOW, LANES), F32)],
        compiler_params=pltpu.CompilerParams(dimension_semantics=("arbitrary", "arbitrary"),
                                             vmem_limit_bytes=VMEM_LIMIT_BYTES),
        name="mix_prompt",
    )(x_prompt, g_pre, w_in_b, b_in2, rel_bias_table,
      sinks2, _bucket_matrix(WINDOW), _lane_patterns(tm), *weights)
    y_p = _mlp(h_p.reshape(B * S, D_MODEL), g_pre_mlp, w_mlp_up_b, w_mlp_down_b, g_post_mlp)

    tm, ns = SAMPLE_TILE, SEQ_PER_TILE
    n_tok = DB * T
    state16 = jnp.pad(state_pool[l], ((0, 0), (POOL_PAD - POOL_STATE, 0), (0, 0)))
    ck = cache_k_win[l].reshape(DB, WINDOW, KV_WIDTH)
    cv = cache_v_win[l].reshape(DB, WINDOW, KV_WIDTH)
    h_s, k_s, v_s, up_new = pl.pallas_call(
        _mix_sample_kernel,
        grid=(n_tok // tm,),
        in_specs=[pl.BlockSpec((tm, D_MODEL), lambda i: (i, 0)),
                  pl.BlockSpec((ns, WINDOW, KV_WIDTH), lambda i: (i, 0, 0)),
                  pl.BlockSpec((ns, WINDOW, KV_WIDTH), lambda i: (i, 0, 0)),
                  pl.BlockSpec((ns, POOL_PAD, POOL_WIDTH), lambda i: (i, 0, 0)),
                  _full((1, D_MODEL)), _full((D_MODEL, IN_WIDTH)), _full((1, IN_WIDTH)),
                  _smem(), _smem(), _full((T, N_KEYS))] + weight_specs,
        out_specs=[pl.BlockSpec((tm, D_MODEL), lambda i: (i, 0)),
                   pl.BlockSpec((ns, WINDOW, KV_WIDTH), lambda i: (i, 0, 0)),
                   pl.BlockSpec((ns, WINDOW, KV_WIDTH), lambda i: (i, 0, 0)),
                   pl.BlockSpec((tm, POOL_WIDTH), lambda i: (i, 0))],
        out_shape=[jax.ShapeDtypeStruct((n_tok, D_MODEL), F32),
                   jax.ShapeDtypeStruct((DB, WINDOW, KV_WIDTH), F32),
                   jax.ShapeDtypeStruct((DB, WINDOW, KV_WIDTH), F32),
                   jax.ShapeDtypeStruct((n_tok, POOL_WIDTH), F32)],
        scratch_shapes=[pltpu.VMEM((2, N_HEADS * T, N_KEYS), F32),
                        pltpu.VMEM((SEQ_GROUP * N_HEADS * T, N_KEYS), F32),
                        pltpu.VMEM((SEQ_GROUP * N_HEADS * T, LANES), F32),
                        pltpu.VMEM((ns, N_KEYS, LANES), BF16),
                        pltpu.VMEM((ns, N_KEYS, 2 * LANES), BF16),
                        pltpu.VMEM((ns, POOL_PAD + T, POOL_WIDTH), F32),
                        pltpu.VMEM((ns, N_HEADS * T, LANES), F32),
                        pltpu.VMEM((ns, N_HEADS * T, LANES), F32)],
        compiler_params=pltpu.CompilerParams(dimension_semantics=("arbitrary",),
                                             vmem_limit_bytes=VMEM_LIMIT_BYTES),
        name="mix_sample",
    )(x_sample.reshape(n_tok, D_MODEL), ck, cv, state16, g_pre, w_in_b, b_in2, rel_bias_table,
      sinks2, _bucket_matrix(T), *weights)
    y_s = _mlp(h_s, g_pre_mlp, w_mlp_up_b, w_mlp_down_b, g_post_mlp)

    kv_shape = (1, DB, WINDOW, N_KV_HEADS, HEAD_DIM)
    k_s = k_s.reshape(kv_shape)
    v_s = v_s.reshape(kv_shape)
    p_s = jnp.concatenate([state_pool[l][:, T:], up_new.reshape(DB, T, POOL_WIDTH)], axis=1)[None]

    return (y_p.reshape(B, S, D_MODEL), y_s.reshape(DB, T, D_MODEL),
            kwin.reshape(1, B, WINDOW, N_KV_HEADS, HEAD_DIM),
            vwin.reshape(1, B, WINDOW, N_KV_HEADS, HEAD_DIM),
            pstate[:, POOL_PAD - POOL_STATE:][None],
            k_s, v_s, p_s)
```

```python
import functools
import math

import jax
import jax.numpy as jnp
import numpy as np
from jax import lax
from jax.experimental import pallas as pl
from jax.experimental.pallas import tpu as pltpu

D_MODEL = 1024
N_HEADS = 8
HEAD_DIM = 64
N_KV_HEADS = 2
GROUP = N_HEADS // N_KV_HEADS
ATTN_WIDTH = N_HEADS * HEAD_DIM
KV_WIDTH = N_KV_HEADS * HEAD_DIM
WINDOW = 128
POOL_WIDTH = D_MODEL // 2
POOL_WINDOWS = (2, 4, 8, 16)
assert POOL_WINDOWS == tuple(2 << g for g in range(len(POOL_WINDOWS)))
POOL_GROUP_WIDTH = POOL_WIDTH // len(POOL_WINDOWS)
POOL_STATE = max(POOL_WINDOWS) - 1
POOL_PAD = POOL_STATE + 1
D_FF = 4 * D_MODEL
N_BUCKETS = 32
MAX_DISTANCE = 128
RMS_EPS = 1e-6
LOG2E = math.log2(math.e)
PAST_LEN = 16384
IN_WIDTH = ATTN_WIDTH + 2 * KV_WIDTH + POOL_WIDTH + 2 * D_MODEL
OFF_K = ATTN_WIDTH
OFF_V = OFF_K + KV_WIDTH
OFF_POOL = OFF_V + KV_WIDTH
OFF_GA = OFF_POOL + POOL_WIDTH
OFF_GP = OFF_GA + D_MODEL
N_KEYS = 2 * WINDOW
GATE_CHUNKS = 4
GATE_CHUNK = (IN_WIDTH - OFF_GA) // GATE_CHUNKS
BLOCKS_PER_ITER = 4

LANES = 128
VMEM_LIMIT_BYTES = 56 * 1024 * 1024

TOKEN_TILE = 512
TILES_PER_STEP = 2
MLP_TILE = 1024
DEC_SEQ = 8
SAMPLE_TILE = 256
SEQ_PER_TILE = SAMPLE_TILE // DEC_SEQ
SEQ_GROUP = 16

BF16 = jnp.bfloat16
F32 = jnp.float32


def _rms(x, g):
    return x * lax.rsqrt(jnp.mean(x * x, axis=-1, keepdims=True) + RMS_EPS) * g


def _sigmoid(x):
    return 0.5 * jnp.tanh(0.5 * x) + 0.5


def _dot(a, b):
    return jnp.dot(a, b, preferred_element_type=F32)


def _dot_nt(a, b):
    return lax.dot_general(a, b, (((1,), (1,)), ((), ())), preferred_element_type=F32)


def _head_queries(q):
    lane = lax.broadcasted_iota(jnp.int32, (q.shape[0], LANES), 1)
    lo = lane < HEAD_DIM
    out = []
    for c in range(ATTN_WIDTH // LANES):
        qc = q[:, c * LANES:(c + 1) * LANES]
        qr = pltpu.roll(qc, HEAD_DIM, axis=1)
        g = (2 * c) // GROUP
        if g == 0:
            out.append(jnp.where(lo, qc, 0.0))
            out.append(jnp.where(lo, qr, 0.0))
        else:
            out.append(jnp.where(lo, 0.0, qr))
            out.append(jnp.where(lo, 0.0, qc))
    return out


def _merge_heads(o_heads):
    lane = lax.broadcasted_iota(jnp.int32, o_heads[0].shape, 1)
    lo = lane < HEAD_DIM
    chunks = []
    for c in range(ATTN_WIDTH // LANES):
        e, o = o_heads[2 * c], o_heads[2 * c + 1]
        g = (2 * c) // GROUP
        if g == 0:
            chunks.append(jnp.where(lo, e, pltpu.roll(o, HEAD_DIM, axis=1)))
        else:
            chunks.append(jnp.where(lo, pltpu.roll(e, HEAD_DIM, axis=1), o))
    return jnp.concatenate(chunks, axis=1)


def _pool_sums(ext):
    parts = []
    for gi, w in enumerate(POOL_WINDOWS):
        acc = ext[:, gi * POOL_GROUP_WIDTH:(gi + 1) * POOL_GROUP_WIDTH]
        shift = 1
        while shift < w:
            acc = acc + pltpu.roll(acc, shift, axis=0)
            shift *= 2
        parts.append(acc)
    return jnp.concatenate(parts, axis=1)


def _pool_counts(pos):
    lane = lax.broadcasted_iota(jnp.int32, (pos.shape[0], POOL_WIDTH), 1)
    w = jnp.left_shift(2, lane // POOL_GROUP_WIDTH)
    return jnp.minimum(pos + 1, w).astype(F32)


def _pool_diff(sums, up, first_pos):
    lane = lax.broadcasted_iota(jnp.int32, (1, POOL_WIDTH), 1)
    w = jnp.left_shift(2, lane // POOL_GROUP_WIDTH)
    z = sums * (1.0 / w.astype(F32)) - up
    pos = first_pos + lax.broadcasted_iota(jnp.int32, (POOL_PAD, 1), 0)
    head = sums[0:POOL_PAD] / _pool_counts(pos) - up[0:POOL_PAD]
    return jnp.concatenate([head, z[POOL_PAD:]], axis=0)


def _mix_tail(x, xn, attn_o, pool_sum, up, cnt, w_in_ref, b_in_ref, w_grp_ref, scale_ref,
              w_attn_up_ref, w_pool_up_ref):
    z = (pool_sum / cnt - up).astype(BF16)
    pool_z = (_dot(z, w_grp_ref[...]) * scale_ref[...]).astype(BF16)
    ga = _dot(xn, w_in_ref[:, OFF_GA:OFF_GP]) + b_in_ref[:, OFF_GA:OFF_GP]
    m = _sigmoid(ga) * _dot(attn_o, w_attn_up_ref[...])
    gp = _dot(xn, w_in_ref[:, OFF_GP:IN_WIDTH]) + b_in_ref[:, OFF_GP:IN_WIDTH]
    m = m + _sigmoid(gp) * _dot(pool_z, w_pool_up_ref[...])
    return m.astype(BF16)


def _masked_bias_tables(rows, bucket_ref, table_ref, biasm_ref):
    bucket = bucket_ref[...]
    r = lax.broadcasted_iota(jnp.int32, (rows, N_KEYS), 0)
    j = lax.broadcasted_iota(jnp.int32, (rows, N_KEYS), 1)
    cur = jnp.where(j >= WINDOW, jnp.where(j - WINDOW <= r, 1.0, 0.0), 0.0)
    prev = jnp.where(j < WINDOW, jnp.where(j > r, 1.0, 0.0), 0.0)
    for h in range(N_HEADS):
        acc = jnp.zeros(bucket.shape, F32)
        for b in range(N_BUCKETS):
            acc = jnp.where(bucket == b, table_ref[b, h] * LOG2E, acc)
        biasm_ref[0, h * rows:(h + 1) * rows, :] = jnp.where(cur > 0.5, acc, -jnp.inf)
        biasm_ref[1, h * rows:(h + 1) * rows, :] = jnp.where(cur + prev > 0.5, acc, -jnp.inf)


def _scores_block(i, base, q_ref, ke_ref, ko_ref, s_ref):
    rows = pl.ds(base, WINDOW)
    keys = pl.ds(base, N_KEYS)
    for g in range(N_KV_HEADS):
        qg = jnp.concatenate([q_ref[rows, (2 * g) * LANES:(2 * g + 1) * LANES],
                              q_ref[rows, (2 * g + 1) * LANES:(2 * g + 2) * LANES]], axis=0)
        s_ref[i, g, 0] = _dot(qg, ke_ref[g, :, keys])
        s_ref[i, g, 1] = _dot(qg, ko_ref[g, :, keys])


def _softmax_block(i, has_prev, s_ref, biasm_ref, sinks_ref, p_ref, es_ref):
    lo = lax.broadcasted_iota(jnp.int32, (WINDOW, LANES), 1) < HEAD_DIM
    for g in range(N_KV_HEADS):
        for a in range(2):
            half = slice(a * WINDOW, (a + 1) * WINDOW)
            es = []
            for b in range(2):
                h = GROUP * g + 2 * a + b
                s = s_ref[i, g, b, half, :] + biasm_ref[has_prev, h * WINDOW:(h + 1) * WINDOW, :]
                sink = sinks_ref[0, h] * LOG2E
                m = jnp.maximum(jnp.max(s, axis=-1, keepdims=True), sink)
                p_ref[i, g, b, half, :] = jnp.exp2(s - m).astype(BF16)
                es.append(jnp.exp2(sink - m))
            es_ref[i, 2 * g + a] = jnp.where(lo, es[0], es[1])


def _values_block(i, base, p_ref, es_ref, ve_ref, vo_ref, ao_ref):
    rows = pl.ds(base, WINDOW)
    keys = pl.ds(base, N_KEYS)
    for g in range(N_KV_HEADS):
        o_ext = _dot(p_ref[i, g, 0], ve_ref[g, keys, :]) + _dot(p_ref[i, g, 1], vo_ref[g, keys, :])
        for a in range(2):
            c = 2 * g + a
            o = o_ext[a * WINDOW:(a + 1) * WINDOW]
            ao_ref[rows, c * LANES:(c + 1) * LANES] = (
                o[:, :LANES] / (o[:, LANES:] + es_ref[i, c])).astype(BF16)


def _mix_prompt_tile(t, x_ref, g_pre_ref, w_in_ref, b_in_ref, table_ref,
                     sinks_ref, bucket_ref, pat_ref,
                     w_grp_ref, scale_ref, w_attn_up_ref, w_pool_up_ref,
                     h_ref, kwin_ref, vwin_ref, pstate_ref,
                     biasm_ref, ke_ref, ko_ref, ve_ref, vo_ref, ext_ref, q_ref, ao_ref,
                     sg_ref, xn_ref, s_ref, p_ref, es_ref):
    tm = TOKEN_TILE

    @pl.when((pl.program_id(0) == 0) & (t == 0))
    def _():
        _masked_bias_tables(WINDOW, bucket_ref, table_ref, biasm_ref)
        ke_ref[...] = jnp.zeros(ke_ref.shape, BF16)
        ko_ref[...] = jnp.zeros(ko_ref.shape, BF16)

    @pl.when(t == 0)
    def _():
        zeros = pat_ref[0, 0:WINDOW, :]
        for g in range(N_KV_HEADS):
            ke_ref[g, :, 0:WINDOW] = zeros
            ko_ref[g, :, 0:WINDOW] = zeros
            ve_ref[g, 0:WINDOW, :] = jnp.concatenate([zeros, pat_ref[1, 0:WINDOW, :]], axis=1)
            vo_ref[g, 0:WINDOW, :] = jnp.concatenate([zeros, pat_ref[2, 0:WINDOW, :]], axis=1)
        ext_ref[0:POOL_PAD, :] = jnp.zeros((POOL_PAD, POOL_WIDTH), F32)

    hr = tm // 2
    lo = lax.broadcasted_iota(jnp.int32, (hr, LANES), 1) < HEAD_DIM
    ones_lo, ones_hi = pat_ref[1, 0:hr, :], pat_ref[2, 0:hr, :]
    for r in range(2):
        rows = slice(r * hr, (r + 1) * hr)
        kv_rows = slice(WINDOW + r * hr, WINDOW + (r + 1) * hr)
        xn = _rms(x_ref[rows, :], g_pre_ref[...]).astype(BF16)
        xn_ref[rows, :] = xn
        u = _dot(xn, w_in_ref[:, 0:OFF_GA]) + b_in_ref[:, 0:OFF_GA]
        k = u[:, OFF_K:OFF_V]
        v = u[:, OFF_V:OFF_POOL]
        up = u[:, OFF_POOL:OFF_GA]
        q_ref[rows, :] = (u[:, 0:OFF_K] * (HEAD_DIM ** -0.5 * LOG2E)).astype(BF16)
        ext_ref[POOL_PAD + r * hr:POOL_PAD + (r + 1) * hr, :] = up
        if r == 1:
            kwin_ref[...] = k[hr - WINDOW:, :]
            vwin_ref[...] = v[hr - WINDOW:, :]
            pstate_ref[...] = up[hr - POOL_PAD:, :]
        vr = pltpu.roll(v, HEAD_DIM, axis=1)
        kt = k.T.astype(BF16)
        ke_ref[0, 0:HEAD_DIM, kv_rows] = kt[0:HEAD_DIM]
        ko_ref[0, HEAD_DIM:, kv_rows] = kt[0:HEAD_DIM]
        ke_ref[1, 0:HEAD_DIM, kv_rows] = kt[HEAD_DIM:]
        ko_ref[1, HEAD_DIM:, kv_rows] = kt[HEAD_DIM:]
        ve_ref[0, kv_rows, :] = jnp.concatenate([jnp.where(lo, v, 0.0).astype(BF16), ones_lo], axis=1)
        vo_ref[0, kv_rows, :] = jnp.concatenate([jnp.where(lo, 0.0, vr).astype(BF16), ones_hi], axis=1)
        ve_ref[1, kv_rows, :] = jnp.concatenate([jnp.where(lo, vr, 0.0).astype(BF16), ones_lo], axis=1)
        vo_ref[1, kv_rows, :] = jnp.concatenate([jnp.where(lo, 0.0, v).astype(BF16), ones_hi], axis=1)

    nb = BLOCKS_PER_ITER
    assert nb == GATE_CHUNKS

    def finish(_, carry):
        bases = [d * WINDOW for d in range(nb)]

        def gate_chunk(d):
            cols = slice(OFF_GA + d * GATE_CHUNK, OFF_GA + (d + 1) * GATE_CHUNK)
            sg_ref[d] = _sigmoid(_dot(xn_ref[...], w_in_ref[:, cols]) + b_in_ref[:, cols])

        for d in range(nb):
            _scores_block(d, bases[d], q_ref, ke_ref, ko_ref, s_ref)
        for d in range(GATE_CHUNKS):
            gate_chunk(d)
        for d in range(nb):
            has_prev = jnp.where(t == 0, 0, 1) if d == 0 else 1
            _softmax_block(d, has_prev, s_ref, biasm_ref, sinks_ref, p_ref, es_ref)
        ext = ext_ref[...]
        z = _pool_diff(_pool_sums(ext)[POOL_PAD:, :], ext[POOL_PAD:, :], t * tm).astype(BF16)
        pool_z = (_dot(z, w_grp_ref[...]) * scale_ref[...]).astype(BF16)
        pp = _dot(pool_z, w_pool_up_ref[...])
        for d in range(nb):
            _values_block(d, bases[d], p_ref, es_ref, ve_ref, vo_ref, ao_ref)
        a = _dot(ao_ref[...], w_attn_up_ref[...])
        half = D_MODEL // 2
        m = jnp.concatenate([sg_ref[0] * a[:, :half] + sg_ref[2] * pp[:, :half],
                             sg_ref[1] * a[:, half:] + sg_ref[3] * pp[:, half:]], axis=1)
        h_ref[...] = m.astype(BF16)
        return carry

    lax.fori_loop(0, 1 + jnp.minimum(t, 0), finish, 0)

    for g in range(N_KV_HEADS):
        ke_ref[g, :, 0:WINDOW] = ke_ref[g, :, tm:tm + WINDOW]
        ko_ref[g, :, 0:WINDOW] = ko_ref[g, :, tm:tm + WINDOW]
        ve_ref[g, 0:WINDOW, :] = ve_ref[g, tm:tm + WINDOW, :]
        vo_ref[g, 0:WINDOW, :] = vo_ref[g, tm:tm + WINDOW, :]
    ext_ref[0:POOL_PAD, :] = ext_ref[tm:tm + POOL_PAD, :]


def _mix_prompt_kernel(n_in, x_ref, *refs):
    h_ref, rest = refs[n_in], refs[n_in + 1:]

    def tile(j, carry):
        rows = pl.ds(pl.multiple_of(j * TOKEN_TILE, TOKEN_TILE), TOKEN_TILE)
        _mix_prompt_tile(pl.program_id(1) * TILES_PER_STEP + j, x_ref.at[rows, :], *refs[:n_in],
                         h_ref.at[rows, :], *rest)
        return carry

    lax.fori_loop(0, TILES_PER_STEP, tile, 0)


def _mix_sample_kernel(x_ref, ck_ref, cv_ref, state_ref, g_pre_ref, w_in_ref, b_in_ref, table_ref,
                       sinks_ref, bucket_ref, w_grp_ref, scale_ref, w_attn_up_ref, w_pool_up_ref,
                       h_ref, kout_ref, vout_ref, up_ref,
                       biasm_ref, bias_ref, sink_ref, kx_ref, vx_ref, ext_ref, qh_ref, oh_ref):
    tm = SAMPLE_TILE
    ns = SEQ_PER_TILE
    T = DEC_SEQ
    R = N_HEADS * T

    @pl.when(pl.program_id(0) == 0)
    def _():
        _masked_bias_tables(T, bucket_ref, table_ref, biasm_ref)
        for h in range(N_HEADS):
            sink_ref[h * T:(h + 1) * T, :] = jnp.full((T, LANES), sinks_ref[0, h] * LOG2E, F32)
        for j in range(SEQ_GROUP):
            bias_ref[j * R:(j + 1) * R, :] = biasm_ref[1]
            if j > 0:
                sink_ref[j * R:(j + 1) * R, :] = sink_ref[0:R, :]
        vx_ref[:, :, LANES:] = jnp.ones((ns, N_KEYS, LANES), BF16)

    x = x_ref[...]
    xn = _rms(x, g_pre_ref[...]).astype(BF16)
    u = _dot(xn, w_in_ref[:, 0:OFF_GA]) + b_in_ref[:, 0:OFF_GA]
    q = u[:, 0:OFF_K] * (HEAD_DIM ** -0.5 * LOG2E)
    k = u[:, OFF_K:OFF_V]
    v = u[:, OFF_V:OFF_POOL]
    up = u[:, OFF_POOL:OFF_GA]
    k3 = k.reshape(ns, T, LANES)
    v3 = v.reshape(ns, T, LANES)
    ck = ck_ref[...]
    cv = cv_ref[...]

    kout_ref[:, 0:WINDOW - T, :] = ck[:, T:, :]
    kout_ref[:, WINDOW - T:, :] = k3
    vout_ref[:, 0:WINDOW - T, :] = cv[:, T:, :]
    vout_ref[:, WINDOW - T:, :] = v3
    up_ref[...] = up

    pad = jnp.zeros((ns, WINDOW - T, LANES), F32)
    kx_ref[:, 0:WINDOW, :] = ck.astype(BF16)
    kx_ref[:, WINDOW:, :] = jnp.concatenate([k3, pad], axis=1).astype(BF16)
    vx_ref[:, 0:WINDOW, 0:LANES] = cv.astype(BF16)
    vx_ref[:, WINDOW:, 0:LANES] = jnp.concatenate([v3, pad], axis=1).astype(BF16)
    ext_ref[:, 0:POOL_PAD, :] = state_ref[...]
    ext_ref[:, POOL_PAD:, :] = up.reshape(ns, T, POOL_WIDTH)
    for h, qh in enumerate(_head_queries(q)):
        qh_ref[:, h * T:(h + 1) * T, :] = qh.reshape(ns, T, LANES)

    def group(gi, carry):
        s0 = gi * SEQ_GROUP
        s = jnp.concatenate([_dot_nt(qh_ref[s0 + j].astype(BF16), kx_ref[s0 + j])
                             for j in range(SEQ_GROUP)], axis=0)
        s = s + bias_ref[...]
        sink = sink_ref[:, 0:1]
        m = jnp.maximum(jnp.max(s, axis=-1, keepdims=True), sink)
        p = jnp.exp2(s - m).astype(BF16)
        es = jnp.exp2(sink - m)
        for j in range(SEQ_GROUP):
            o = _dot(p[j * R:(j + 1) * R], vx_ref[s0 + j])
            oh_ref[s0 + j] = o[:, :LANES] / (o[:, LANES:] + es[j * R:(j + 1) * R])
        return carry

    lax.fori_loop(0, ns // SEQ_GROUP, group, 0)

    o_heads = [oh_ref[:, h * T:(h + 1) * T, :].reshape(tm, LANES) for h in range(N_HEADS)]
    attn_o = _merge_heads(o_heads).astype(BF16)

    ext = ext_ref[...].reshape(ns * (POOL_PAD + T), POOL_WIDTH)
    pool_sum = _pool_sums(ext).reshape(ns, POOL_PAD + T, POOL_WIDTH)[:, POOL_PAD:, :]
    pool_sum = pool_sum.reshape(tm, POOL_WIDTH)
    row = lax.broadcasted_iota(jnp.int32, (tm, 1), 0)
    pos = PAST_LEN + (row & (T - 1))
    h_ref[...] = _mix_tail(x, xn, attn_o, pool_sum, up, _pool_counts(pos),
                           w_in_ref, b_in_ref, w_grp_ref, scale_ref,
                           w_attn_up_ref, w_pool_up_ref)


def _mlp_kernel(x_ref, m_ref, w_out_ref, g_post_mix_ref, g_pre_ref, w_up_ref, w_down_ref,
                g_post_ref, y_ref):
    rows = x_ref.shape[0] // 2
    chunk = D_FF // 4
    n_chunks = D_FF // chunk
    hn = []
    for r in range(2):
        sl = slice(r * rows, (r + 1) * rows)
        mo = _dot(m_ref[sl, :], w_out_ref[...])
        h = x_ref[sl, :] + _rms(mo, g_post_mix_ref[...])
        y_ref[sl, :] = h
        hn.append(_rms(h, g_pre_ref[...]).astype(BF16))
    f = [jnp.zeros((rows, D_MODEL), F32) for _ in range(2)]
    for c in range(n_chunks):
        for r in range(2):
            a = jnp.maximum(_dot(hn[r], w_up_ref[:, c * chunk:(c + 1) * chunk]), 0.0)
            f[r] = f[r] + _dot((a * a).astype(BF16), w_down_ref[c * chunk:(c + 1) * chunk, :])
    for r in range(2):
        sl = slice(r * rows, (r + 1) * rows)
        y_ref[sl, :] = y_ref[sl, :] + _rms(f[r], g_post_ref[...])


def _full(shape):
    return pl.BlockSpec(shape, lambda *_: (0,) * len(shape), pipeline_mode=pl.Buffered(1))


def _smem():
    return pl.BlockSpec(memory_space=pltpu.SMEM)


def _rel_bucket(dist):
    d = np.maximum(dist, 0)
    max_exact = N_BUCKETS // 2
    large = max_exact + (np.log(np.maximum(d, 1).astype(np.float32) / np.float32(max_exact))
                         / np.float32(math.log(MAX_DISTANCE / max_exact))
                         * np.float32(N_BUCKETS - max_exact)).astype(np.int32)
    large = np.minimum(large, N_BUCKETS - 1)
    return np.where(d < max_exact, d, large)


def _bucket_matrix(rows):
    r = np.arange(rows)[:, None]
    j = np.arange(N_KEYS)[None, :]
    return _rel_bucket(r + WINDOW - j).astype(np.int32)


def _lane_patterns(rows):
    lo = (jnp.arange(LANES) < HEAD_DIM).astype(BF16)
    pats = jnp.stack([jnp.zeros_like(lo), lo, 1 - lo])
    return jnp.broadcast_to(pats[:, None, :], (3, rows, LANES))


def _mlp(x2d, m2d, w_out, g_post_mix, g_pre, w_up, w_down, g_post):
    n = x2d.shape[0]
    tile = MLP_TILE
    return pl.pallas_call(
        _mlp_kernel,
        grid=(n // tile,),
        in_specs=[pl.BlockSpec((tile, D_MODEL), lambda i: (i, 0)),
                  pl.BlockSpec((tile, D_MODEL), lambda i: (i, 0)),
                  _full((D_MODEL, D_MODEL)), _full((1, D_MODEL)),
                  _full((1, D_MODEL)), _full((D_MODEL, D_FF)), _full((D_FF, D_MODEL)),
                  _full((1, D_MODEL))],
        out_specs=pl.BlockSpec((tile, D_MODEL), lambda i: (i, 0)),
        out_shape=jax.ShapeDtypeStruct((n, D_MODEL), F32),
        compiler_params=pltpu.CompilerParams(dimension_semantics=("arbitrary",),
                                             vmem_limit_bytes=VMEM_LIMIT_BYTES),
        name="mlp",
    )(x2d, m2d, w_out, g_post_mix, g_pre, w_up, w_down, g_post)


def kernel(x_prompt, x_sample, cache_k_win, cache_v_win, state_pool, norm_pre_mix, norm_post_mix,
           norm_pre_mlp, norm_post_mlp, w_in, b_in, attn_sinks, rel_bias_table, w_attn_up,
           w_pool_grp, pool_scale, w_pool_up, w_out, w_mlp_up, w_mlp_down):
    B, S, _ = x_prompt.shape
    DB, T, _ = x_sample.shape
    depth = w_in.shape[0]
    assert depth == 1 and S % TOKEN_TILE == 0 and (DB * T) % TOKEN_TILE == 0
    assert T == DEC_SEQ and (DB * T) % SAMPLE_TILE == 0
    assert GATE_CHUNKS == TOKEN_TILE // WINDOW

    l = 0
    w_in_b = w_in[l].astype(BF16)
    w_grp_b = jax.scipy.linalg.block_diag(*[w_pool_grp[l, g] for g in range(len(POOL_WINDOWS))]
                                          ).astype(BF16)
    w_attn_up_b = w_attn_up[l].astype(BF16)
    w_pool_up_b = w_pool_up[l].astype(BF16)
    w_out_b = w_out[l].astype(BF16)
    w_mlp_up_b = w_mlp_up[l].astype(BF16)
    w_mlp_down_b = w_mlp_down[l].astype(BF16)
    b_in2 = b_in[l][None, :]
    sinks2 = attn_sinks[l][None, :]
    scale2 = pool_scale[l][None, :]
    g_pre, g_post = norm_pre_mix[l][None, :], norm_post_mix[l][None, :]
    g_pre_mlp, g_post_mlp = norm_pre_mlp[l][None, :], norm_post_mlp[l][None, :]

    weight_specs = [_full((POOL_WIDTH, POOL_WIDTH)), _full((1, POOL_WIDTH)),
                    _full((ATTN_WIDTH, D_MODEL)), _full((POOL_WIDTH, D_MODEL))]
    weights = (w_grp_b, scale2, w_attn_up_b, w_pool_up_b)

    tm = TOKEN_TILE
    step = tm * TILES_PER_STEP
    assert S % step == 0
    prompt_in_specs = [pl.BlockSpec((None, step, D_MODEL), lambda b, t: (b, t, 0)),
                       _full((1, D_MODEL)), _full((D_MODEL, IN_WIDTH)), _full((1, IN_WIDTH)),
                       _smem(), _smem(), _full((WINDOW, N_KEYS)), _full((3, tm, LANES))] + weight_specs
    h_p, kwin, vwin, pstate = pl.pallas_call(
        functools.partial(_mix_prompt_kernel, len(prompt_in_specs) - 1),
        grid=(B, S // step),
        in_specs=prompt_in_specs,
        out_specs=[pl.BlockSpec((None, step, D_MODEL), lambda b, t: (b, t, 0)),
                   pl.BlockSpec((None, WINDOW, KV_WIDTH), lambda b, t: (b, 0, 0)),
                   pl.BlockSpec((None, WINDOW, KV_WIDTH), lambda b, t: (b, 0, 0)),
                   pl.BlockSpec((None, POOL_PAD, POOL_WIDTH), lambda b, t: (b, 0, 0))],
        out_shape=[jax.ShapeDtypeStruct((B, S, D_MODEL), BF16),
                   jax.ShapeDtypeStruct((B, WINDOW, KV_WIDTH), F32),
                   jax.ShapeDtypeStruct((B, WINDOW, KV_WIDTH), F32),
                   jax.ShapeDtypeStruct((B, POOL_PAD, POOL_WIDTH), F32)],
        scratch_shapes=[pltpu.VMEM((2, N_HEADS * WINDOW, N_KEYS), F32),
                        pltpu.VMEM((N_KV_HEADS, LANES, WINDOW + tm), BF16),
                        pltpu.VMEM((N_KV_HEADS, LANES, WINDOW + tm), BF16),
                        pltpu.VMEM((N_KV_HEADS, WINDOW + tm, 2 * LANES), BF16),
                        pltpu.VMEM((N_KV_HEADS, WINDOW + tm, 2 * LANES), BF16),
                        pltpu.VMEM((POOL_PAD + tm, POOL_WIDTH), F32),
                        pltpu.VMEM((tm, ATTN_WIDTH), BF16),
                        pltpu.VMEM((tm, ATTN_WIDTH), BF16),
                        pltpu.VMEM((GATE_CHUNKS, tm, GATE_CHUNK), F32),
                        pltpu.VMEM((tm, D_MODEL), BF16),
                        pltpu.VMEM((BLOCKS_PER_ITER, N_KV_HEADS, 2, 2 * WINDOW, N_KEYS), F32),
                        pltpu.VMEM((BLOCKS_PER_ITER, N_KV_HEADS, 2, 2 * WINDOW, N_KEYS), BF16),
                        pltpu.VMEM((BLOCKS_PER_ITER, ATTN_WIDTH // LANES, WINDOW, LANES), F32)],
        compiler_params=pltpu.CompilerParams(dimension_semantics=("arbitrary", "arbitrary"),
                                             vmem_limit_bytes=VMEM_LIMIT_BYTES),
        name="mix_prompt",
    )(x_prompt, g_pre, w_in_b, b_in2, rel_bias_table,
      sinks2, _bucket_matrix(WINDOW), _lane_patterns(tm), *weights)
    y_p = _mlp(x_prompt.reshape(B * S, D_MODEL), h_p.reshape(B * S, D_MODEL), w_out_b, g_post,
               g_pre_mlp, w_mlp_up_b, w_mlp_down_b, g_post_mlp)

    tm, ns = SAMPLE_TILE, SEQ_PER_TILE
    n_tok = DB * T
    state16 = jnp.pad(state_pool[l], ((0, 0), (POOL_PAD - POOL_STATE, 0), (0, 0)))
    ck = cache_k_win[l].reshape(DB, WINDOW, KV_WIDTH)
    cv = cache_v_win[l].reshape(DB, WINDOW, KV_WIDTH)
    h_s, k_s, v_s, up_new = pl.pallas_call(
        _mix_sample_kernel,
        grid=(n_tok // tm,),
        in_specs=[pl.BlockSpec((tm, D_MODEL), lambda i: (i, 0)),
                  pl.BlockSpec((ns, WINDOW, KV_WIDTH), lambda i: (i, 0, 0)),
                  pl.BlockSpec((ns, WINDOW, KV_WIDTH), lambda i: (i, 0, 0)),
                  pl.BlockSpec((ns, POOL_PAD, POOL_WIDTH), lambda i: (i, 0, 0)),
                  _full((1, D_MODEL)), _full((D_MODEL, IN_WIDTH)), _full((1, IN_WIDTH)),
                  _smem(), _smem(), _full((T, N_KEYS))] + weight_specs,
        out_specs=[pl.BlockSpec((tm, D_MODEL), lambda i: (i, 0)),
                   pl.BlockSpec((ns, WINDOW, KV_WIDTH), lambda i: (i, 0, 0)),
                   pl.BlockSpec((ns, WINDOW, KV_WIDTH), lambda i: (i, 0, 0)),
                   pl.BlockSpec((tm, POOL_WIDTH), lambda i: (i, 0))],
        out_shape=[jax.ShapeDtypeStruct((n_tok, D_MODEL), BF16),
                   jax.ShapeDtypeStruct((DB, WINDOW, KV_WIDTH), F32),
                   jax.ShapeDtypeStruct((DB, WINDOW, KV_WIDTH), F32),
                   jax.ShapeDtypeStruct((n_tok, POOL_WIDTH), F32)],
        scratch_shapes=[pltpu.VMEM((2, N_HEADS * T, N_KEYS), F32),
                        pltpu.VMEM((SEQ_GROUP * N_HEADS * T, N_KEYS), F32),
                        pltpu.VMEM((SEQ_GROUP * N_HEADS * T, LANES), F32),
                        pltpu.VMEM((ns, N_KEYS, LANES), BF16),
                        pltpu.VMEM((ns, N_KEYS, 2 * LANES), BF16),
                        pltpu.VMEM((ns, POOL_PAD + T, POOL_WIDTH), F32),
                        pltpu.VMEM((ns, N_HEADS * T, LANES), F32),
                        pltpu.VMEM((ns, N_HEADS * T, LANES), F32)],
        compiler_params=pltpu.CompilerParams(dimension_semantics=("arbitrary",),
                                             vmem_limit_bytes=VMEM_LIMIT_BYTES),
        name="mix_sample",
    )(x_sample.reshape(n_tok, D_MODEL), ck, cv, state16, g_pre, w_in_b, b_in2, rel_bias_table,
      sinks2, _bucket_matrix(T), *weights)
    y_s = _mlp(x_sample.reshape(n_tok, D_MODEL), h_s, w_out_b, g_post,
               g_pre_mlp, w_mlp_up_b, w_mlp_down_b, g_post_mlp)

    kv_shape = (1, DB, WINDOW, N_KV_HEADS, HEAD_DIM)
    k_s = k_s.reshape(kv_shape)
    v_s = v_s.reshape(kv_shape)
    p_s = jnp.concatenate([state_pool[l][:, T:], up_new.reshape(DB, T, POOL_WIDTH)], axis=1)[None]

    return (y_p.reshape(B, S, D_MODEL), y_s.reshape(DB, T, D_MODEL),
            kwin.reshape(1, B, WINDOW, N_KV_HEADS, HEAD_DIM),
            vwin.reshape(1, B, WINDOW, N_KV_HEADS, HEAD_DIM),
            pstate[:, POOL_PAD - POOL_STATE:][None],
            k_s, v_s, p_s)
```
